```python
import jax, jax.numpy as jnp
from jax import lax
import numpy as np

D_MODEL = 4096
BATCH = 1
SEQ = 8192
DEPTH = 1

CHUNK = 64
Q_BLOCK = 128
PLE_DIM = 256
EPS = 1e-6

POOL_WINDOWS = (2, 4, 8, 16)
POOL_GROUPS = len(POOL_WINDOWS)
POOL_WIDTH = D_MODEL // 2
POOL_GROUP_WIDTH = POOL_WIDTH // POOL_GROUPS

N_HEADS = 16
QK_NOPE_DIM = 128
QK_ROPE_DIM = 64
QK_DIM = QK_NOPE_DIM + QK_ROPE_DIM
V_HEAD_DIM = 128
MLA_WIDTH = N_HEADS * V_HEAD_DIM
Q_LORA_RANK = D_MODEL // 4
KV_LORA_RANK = 512
ROPE_THETA = 10000.0

IN_WIDTH = POOL_WIDTH + Q_LORA_RANK + KV_LORA_RANK + QK_ROPE_DIM
N_BRANCHES = 2

D_FF = -(-8 * D_MODEL // (3 * 256)) * 256

kernel_name = "hybrid_pool_mla_gated_encoder"


def rms_norm(x, g):
    xf = x.astype(jnp.float32)
    xf = xf * lax.rsqrt(jnp.mean(xf * xf, axis=-1, keepdims=True) + EPS)
    return (xf * g.astype(jnp.float32)).astype(x.dtype)


def rope_tables(positions):
    inv_freq = ROPE_THETA ** (-jnp.arange(0, QK_ROPE_DIM, 2, dtype=jnp.float32) / QK_ROPE_DIM)
    ang = positions.astype(jnp.float32)[..., None] * inv_freq
    return jnp.cos(ang), jnp.sin(ang)


def apply_rope(x, cos, sin):
    xf = x.astype(jnp.float32)
    x1, x2 = xf[..., :QK_ROPE_DIM // 2], xf[..., QK_ROPE_DIM // 2:]
    out = jnp.concatenate([x1 * cos - x2 * sin, x1 * sin + x2 * cos], axis=-1)
    return out.astype(x.dtype)


def pool_mixer(u, w_pool, pool_scale):
    b, s, _ = u.shape
    uf = u.astype(jnp.float32).reshape(b, s, POOL_GROUPS, POOL_GROUP_WIDTH)
    cs = jnp.pad(jnp.cumsum(uf, axis=1), ((0, 0), (1, 0), (0, 0), (0, 0)))
    t = jnp.arange(s)
    outs = []
    for g, w in enumerate(POOL_WINDOWS):
        start = jnp.maximum(t + 1 - w, 0)
        cnt = (t + 1 - start).astype(jnp.float32)[None, :, None]
        win_sum = cs[:, 1:, g] - cs[:, start, g]
        outs.append(win_sum / cnt - uf[:, :, g])
    pooled = jnp.stack(outs, axis=2).astype(u.dtype)
    mixed = jnp.einsum('bsgc,gcd->bsgd', pooled, w_pool)
    return mixed.reshape(b, s, POOL_WIDTH) * pool_scale


def mla_mixer(q_lat, kv_lat, k_rope, cos, sin, q_norm, kv_norm, w_q_b, w_kv_b):
    b, s, _ = q_lat.shape
    q = jnp.einsum('bsr,rhd->bshd', rms_norm(q_lat, q_norm), w_q_b)
    q_nope = q[..., :QK_NOPE_DIM]
    q_rope = apply_rope(q[..., QK_NOPE_DIM:], cos[:, :, None], sin[:, :, None])
    kv = jnp.einsum('bsr,rhd->bshd', rms_norm(kv_lat, kv_norm), w_kv_b)
    k_nope, v = kv[..., :QK_NOPE_DIM], kv[..., QK_NOPE_DIM:]
    k_rope = apply_rope(k_rope, cos, sin)
    scale = QK_DIM ** -0.5
    n_blocks = s // Q_BLOCK

    def to_blocks(a):
        return a.reshape(b, n_blocks, Q_BLOCK, *a.shape[2:]).swapaxes(0, 1)

    key_chunk = jnp.arange(s) // CHUNK
    q_starts = jnp.arange(n_blocks) * Q_BLOCK

    def attend(args):
        qn, qr, start = args
        sc = (jnp.einsum('bqhd,bkhd->bhqk', qn, k_nope, preferred_element_type=jnp.float32)
              + jnp.einsum('bqhd,bkd->bhqk', qr, k_rope, preferred_element_type=jnp.float32)) * scale
        q_chunk = (start + jnp.arange(Q_BLOCK)) // CHUNK
        mask = key_chunk[None, :] <= q_chunk[:, None]
        sc = jnp.where(mask[None, None], sc, -jnp.inf)
        pr = jax.nn.softmax(sc, axis=-1).astype(v.dtype)
        return jnp.einsum('bhqk,bkhd->bqhd', pr, v)

    out = lax.map(attend, (to_blocks(q_nope), to_blocks(q_rope), q_starts))
    return out.swapaxes(0, 1).reshape(b, s, MLA_WIDTH)


def setup_inputs(seed: int = 0) -> dict:
    key = jax.random.key(seed)
    ks = jax.random.split(key, 32)
    f32 = jnp.float32

    def dense(k, shape, fan_in):
        return jax.random.normal(k, shape, f32) * (fan_in ** -0.5)

    def gain(k, dim):
        return 1.0 + 0.1 * jax.random.normal(k, (DEPTH, dim), f32)

    offset = jax.random.randint(ks[2], (BATCH, 1), 0, 1024, dtype=jnp.int32)
    positions = offset + jnp.arange(SEQ, dtype=jnp.int32)[None, :]
    return {
        "x": jax.random.normal(ks[0], (BATCH, SEQ, D_MODEL), f32),
        "p": jax.random.normal(ks[1], (DEPTH, BATCH, SEQ, PLE_DIM), f32),
        "positions": positions,
        "norm_mix_pre": gain(ks[3], D_MODEL),
        "norm_mix_post": gain(ks[4], D_MODEL),
        "w_in": dense(ks[5], (DEPTH, D_MODEL, IN_WIDTH), D_MODEL),
        "q_norm": gain(ks[6], Q_LORA_RANK),
        "kv_norm": gain(ks[7], KV_LORA_RANK),
        "w_q_b": dense(ks[8], (DEPTH, Q_LORA_RANK, N_HEADS, QK_DIM), Q_LORA_RANK),
        "w_kv_b": dense(ks[9], (DEPTH, KV_LORA_RANK, N_HEADS, QK_NOPE_DIM + V_HEAD_DIM), KV_LORA_RANK),
        "w_pool": dense(ks[10], (DEPTH, POOL_GROUPS, POOL_GROUP_WIDTH, POOL_GROUP_WIDTH), POOL_GROUP_WIDTH),
        "pool_scale": gain(ks[11], POOL_WIDTH),
        "w_up_pool": dense(ks[12], (DEPTH, POOL_WIDTH, D_MODEL), POOL_WIDTH),
        "w_up_mla": dense(ks[13], (DEPTH, MLA_WIDTH, D_MODEL), MLA_WIDTH),
        "w_branch_gate": dense(ks[14], (DEPTH, D_MODEL, N_BRANCHES, D_MODEL), D_MODEL),
        "w_out": dense(ks[15], (DEPTH, D_MODEL, D_MODEL), D_MODEL),
        "norm_ffn_pre": gain(ks[16], D_MODEL),
        "norm_ffn_post": gain(ks[17], D_MODEL),
        "w_ffn_gate": dense(ks[18], (DEPTH, D_MODEL, D_FF), D_MODEL),
        "w_ffn_up": dense(ks[19], (DEPTH, D_MODEL, D_FF), D_MODEL),
        "w_ffn_down": dense(ks[20], (DEPTH, D_FF, D_MODEL), D_FF),
        "norm_ple_pre": gain(ks[21], D_MODEL),
        "w_ple_gate": dense(ks[22], (DEPTH, D_MODEL, D_MODEL), D_MODEL),
        "w_ple_proj": dense(ks[23], (DEPTH, PLE_DIM, D_MODEL), PLE_DIM),
        "norm_ple_post": gain(ks[24], D_MODEL),
    }


def reference(x, p, positions, norm_mix_pre, norm_mix_post, w_in, q_norm, kv_norm, w_q_b, w_kv_b,
              w_pool, pool_scale, w_up_pool, w_up_mla, w_branch_gate, w_out, norm_ffn_pre,
              norm_ffn_post, w_ffn_gate, w_ffn_up, w_ffn_down, norm_ple_pre, w_ple_gate,
              w_ple_proj, norm_ple_post):
    cos, sin = rope_tables(positions)
    o1 = POOL_WIDTH
    o2 = o1 + Q_LORA_RANK
    o3 = o2 + KV_LORA_RANK
    for i in range(DEPTH):
        h = rms_norm(x, norm_mix_pre[i])
        z = jnp.einsum('bsd,de->bse', h, w_in[i])
        u_pool, q_lat, kv_lat, k_rope = z[..., :o1], z[..., o1:o2], z[..., o2:o3], z[..., o3:]
        ya = jnp.einsum('bsc,cd->bsd', pool_mixer(u_pool, w_pool[i], pool_scale[i]), w_up_pool[i])
        yb = jnp.einsum('bsc,cd->bsd',
                        mla_mixer(q_lat, kv_lat, k_rope, cos, sin, q_norm[i], kv_norm[i],
                                  w_q_b[i], w_kv_b[i]),
                        w_up_mla[i])
        gates = jax.nn.sigmoid(jnp.einsum('bsd,dge->bsge', h, w_branch_gate[i]))
        merged = gates[:, :, 0] * ya + gates[:, :, 1] * yb
        mix = jnp.einsum('bsd,de->bse', merged, w_out[i])
        x = x + rms_norm(mix, norm_mix_post[i])
        h2 = rms_norm(x, norm_ffn_pre[i])
        act = jax.nn.silu(jnp.einsum('bsd,df->bsf', h2, w_ffn_gate[i])) * jnp.einsum('bsd,df->bsf', h2, w_ffn_up[i])
        ffn = jnp.einsum('bsf,fd->bsd', act, w_ffn_down[i])
        x = x + rms_norm(ffn, norm_ffn_post[i])
        gate = jax.nn.sigmoid(jnp.einsum('bsd,de->bse', rms_norm(x, norm_ple_pre[i]), w_ple_gate[i]))
        pe = jnp.einsum('bsr,rd->bsd', p[i].astype(x.dtype), w_ple_proj[i])
        x = x + rms_norm(pe * gate, norm_ple_post[i])
    return x
```

```python
import functools

import jax
import jax.numpy as jnp
from jax import lax
from jax.experimental import pallas as pl
from jax.experimental.pallas import tpu as pltpu

CHUNK = 64
EPS = 1e-6
POOL_WINDOWS = (2, 4, 8, 16)
QK_NOPE_DIM = 128
QK_ROPE_DIM = 64
V_HEAD_DIM = 128
ROPE_THETA = 10000.0

V7X_LANES = 128
V7X_MXU_DIM = 256
V7X_VMEM_BYTES = 64 * 1024 * 1024
V7X_VMEM_USABLE_BYTES = 60000 * 1024
V7X_VMEM_DEFAULT_SCOPED_BYTES = 32 * 1024 * 1024

POOL_HALO = 16
F32 = jnp.float32
BF16 = jnp.bfloat16


def _nbytes(shape, dtype):
    n = 1
    for s in shape:
        n *= s
    return n * jnp.dtype(dtype).itemsize


def _params(semantics, blocks, scratch=(), temps=()):
    need = 2 * sum(_nbytes(s, d) for s, d in blocks)
    need += sum(_nbytes(s, d) for s, d in scratch) + sum(_nbytes(s, d) for s, d in temps)
    need = max(int(need * 1.25) + (2 << 20), V7X_VMEM_DEFAULT_SCOPED_BYTES)
    return pltpu.CompilerParams(dimension_semantics=semantics,
                                vmem_limit_bytes=min(need, V7X_VMEM_USABLE_BYTES))


def _rms(xf, g):
    return xf * lax.rsqrt(jnp.mean(xf * xf, axis=-1, keepdims=True) + EPS) * g


def _rmsnorm_cast_kernel(x_ref, g_ref, o_ref):
    o_ref[...] = _rms(x_ref[...], g_ref[...]).astype(o_ref.dtype)


def _rmsnorm_cast(x, g, *, tm=256):
    s, d = x.shape
    return pl.pallas_call(
        _rmsnorm_cast_kernel,
        grid=(s // tm,),
        in_specs=[pl.BlockSpec((tm, d), lambda i: (i, 0)), pl.BlockSpec((1, d), lambda i: (0, 0))],
        out_specs=pl.BlockSpec((tm, d), lambda i: (i, 0)),
        out_shape=jax.ShapeDtypeStruct((s, d), BF16),
        compiler_params=_params(("parallel",), [((tm, d), F32), ((tm, d), BF16)], temps=[((tm, d), F32)]),
        name="rmsnorm_cast",
    )(x, g)


def _matmul_kernel(a_ref, b_ref, o_ref):
    o_ref[...] = jnp.dot(a_ref[...], b_ref[...], preferred_element_type=F32).astype(o_ref.dtype)


def _matmul(a, b, *, tm, tn, out_dtype, name):
    m, k = a.shape
    _, n = b.shape
    return pl.pallas_call(
        _matmul_kernel,
        grid=(m // tm, n // tn),
        in_specs=[pl.BlockSpec((tm, k), lambda i, j: (i, 0)), pl.BlockSpec((k, tn), lambda i, j: (0, j))],
        out_specs=pl.BlockSpec((tm, tn), lambda i, j: (i, j)),
        out_shape=jax.ShapeDtypeStruct((m, n), out_dtype),
        compiler_params=_params(("parallel", "parallel"),
                                [((tm, k), a.dtype), ((k, tn), b.dtype), ((tm, tn), out_dtype)],
                                temps=[((tm, tn), F32)]),
        name=name,
    )(a, b)


def _pool_mixer_kernel(halo_ref, u_ref, w_ref, scale_ref, o_ref, ext_ref, *, tm, gw):
    i = pl.program_id(0)
    ext_ref[0:POOL_HALO, :] = jnp.where(i > 0, halo_ref[...], 0.0)
    ext_ref[POOL_HALO:POOL_HALO + tm, :] = u_ref[...]
    row = lax.broadcasted_iota(jnp.int32, (tm, 1), 0) + i * tm
    for g, w in enumerate(POOL_WINDOWS):
        cols = slice(g * gw, (g + 1) * gw)
        u = u_ref[:, cols]
        win_sum = u
        for back in range(1, w):
            win_sum = win_sum + ext_ref[POOL_HALO - back:POOL_HALO - back + tm, cols]
        cnt = jnp.minimum(row + 1, w).astype(F32)
        pooled = win_sum / cnt - u
        mixed = jnp.dot(pooled.astype(BF16), w_ref[g], preferred_element_type=F32)
        o_ref[:, cols] = (mixed * scale_ref[:, cols]).astype(o_ref.dtype)


def _pool_mixer(z, w_pool, pool_scale, *, pool_width, tm=512):
    s = z.shape[0]
    groups, gw, _ = w_pool.shape
    halo_blocks = tm // POOL_HALO
    return pl.pallas_call(
        functools.partial(_pool_mixer_kernel, tm=tm, gw=gw),
        grid=(s // tm,),
        in_specs=[
            pl.BlockSpec((POOL_HALO, pool_width), lambda i: (jnp.maximum(i * halo_blocks - 1, 0), 0)),
            pl.BlockSpec((tm, pool_width), lambda i: (i, 0)),
            pl.BlockSpec((groups, gw, gw), lambda i: (0, 0, 0)),
            pl.BlockSpec((1, pool_width), lambda i: (0, 0)),
        ],
        out_specs=pl.BlockSpec((tm, pool_width), lambda i: (i, 0)),
        out_shape=jax.ShapeDtypeStruct((s, pool_width), BF16),
        scratch_shapes=[pltpu.VMEM((POOL_HALO + tm, pool_width), F32)],
        compiler_params=_params(("parallel",),
                                [((tm, pool_width), F32), ((groups, gw, gw), BF16), ((tm, pool_width), BF16)],
                                scratch=[((POOL_HALO + tm, pool_width), F32)],
                                temps=[((tm, pool_width), F32)]),
        name="pool_mixer",
    )(z, z, w_pool, pool_scale)


def _q_proj_kernel(ql_ref, g_ref, w_ref, pos_ref, freq_ref, o_ref, qn_ref, cos_ref, sin_ref, *, scale):
    @pl.when(pl.program_id(1) == 0)
    def _():
        qn_ref[...] = _rms(ql_ref[...], g_ref[...]).astype(qn_ref.dtype)
        ang = freq_ref[...] * pos_ref[...]
        cos_ref[...] = jnp.cos(ang)
        sin_ref[...] = jnp.sin(ang)

    qt = lax.dot_general(w_ref[...], qn_ref[...], (((1,), (1,)), ((), ())), preferred_element_type=F32)
    half = QK_ROPE_DIM // 2
    r0, r1, r2 = QK_NOPE_DIM, QK_NOPE_DIM + half, QK_NOPE_DIM + QK_ROPE_DIM
    x1, x2 = qt[r0:r1], qt[r1:r2]
    c, sn = cos_ref[...], sin_ref[...]
    o_ref[0:r0, :] = (qt[0:r0] * scale).astype(o_ref.dtype)
    o_ref[r0:r1, :] = ((x1 * c - x2 * sn) * scale).astype(o_ref.dtype)
    o_ref[r1:r2, :] = ((x1 * sn + x2 * c) * scale).astype(o_ref.dtype)
    o_ref[r2:, :] = qt[r2:].astype(o_ref.dtype)


def _q_proj(z, q_norm, wq_t, pos_row, freq_col, *, col_block, rank, scale, tm=512):
    s = z.shape[0]
    rows = V7X_MXU_DIM
    n_heads = wq_t.shape[0] // rows
    half = QK_ROPE_DIM // 2
    return pl.pallas_call(
        functools.partial(_q_proj_kernel, scale=scale),
        grid=(s // tm, n_heads),
        in_specs=[
            pl.BlockSpec((tm, rank), lambda i, h: (i, col_block)),
            pl.BlockSpec((1, rank), lambda i, h: (0, 0)),
            pl.BlockSpec((rows, rank), lambda i, h: (h, 0)),
            pl.BlockSpec((1, tm), lambda i, h: (0, i)),
            pl.BlockSpec((half, 1), lambda i, h: (0, 0)),
        ],
        out_specs=pl.BlockSpec((rows, tm), lambda i, h: (h, i)),
        out_shape=jax.ShapeDtypeStruct((n_heads * rows, s), BF16),
        scratch_shapes=[pltpu.VMEM((tm, rank), BF16), pltpu.VMEM((half, tm), F32), pltpu.VMEM((half, tm), F32)],
        compiler_params=_params(("parallel", "arbitrary"),
                                [((tm, rank), F32), ((rows, rank), BF16), ((rows, tm), BF16)],
                                scratch=[((tm, rank), BF16), ((2 * half, tm), F32)],
                                temps=[((tm, rank), F32), ((rows, tm), F32)]),
        name="q_proj",
    )(z, q_norm, wq_t, pos_row, freq_col)


def _kv_proj_kernel(kvl_ref, kr_ref, g_ref, wk_ref, wvt_ref, pos_ref, freq_ref, k_ref, vt_ref, kro_ref):
    kvn = _rms(kvl_ref[...], g_ref[...]).astype(BF16)
    k_ref[...] = jnp.dot(kvn, wk_ref[...], preferred_element_type=F32).astype(k_ref.dtype)
    vt = lax.dot_general(wvt_ref[...], kvn, (((1,), (1,)), ((), ())), preferred_element_type=F32)
    vt_ref[:, 0] = vt.reshape(vt_ref.shape[0], V_HEAD_DIM, vt.shape[1]).astype(vt_ref.dtype)
    half = QK_ROPE_DIM // 2
    x = kr_ref[...]
    ang = pos_ref[...] * freq_ref[...]
    lane = lax.broadcasted_iota(jnp.int32, x.shape, 1)
    x2_at_lo = pltpu.roll(x, V7X_LANES - half, axis=1)
    x1_at_hi = pltpu.roll(x, half, axis=1)
    partner = jnp.where(lane < half, -x2_at_lo, jnp.where(lane < QK_ROPE_DIM, x1_at_hi, 0.0))
    kro_ref[...] = (x * jnp.cos(ang) + partner * jnp.sin(ang)).astype(kro_ref.dtype)


def _kv_proj(z, kv_norm, wk, wv_t, pos_col, freq_row, *, lat_block, rope_block, rank, tm):
    s = z.shape[0]
    nk = wk.shape[1]
    nv = wv_t.shape[0]
    n_heads = nv // V_HEAD_DIM
    return pl.pallas_call(
        _kv_proj_kernel,
        grid=(s // tm,),
        in_specs=[
            pl.BlockSpec((tm, rank), lambda i: (i, lat_block)),
            pl.BlockSpec((tm, V7X_LANES), lambda i: (i, rope_block)),
            pl.BlockSpec((1, rank), lambda i: (0, 0)),
            pl.BlockSpec((rank, nk), lambda i: (0, 0)),
            pl.BlockSpec((nv, rank), lambda i: (0, 0)),
            pl.BlockSpec((tm, 1), lambda i: (i, 0)),
            pl.BlockSpec((1, V7X_LANES), lambda i: (0, 0)),
        ],
        out_specs=[
            pl.BlockSpec((tm, nk), lambda i: (i, 0)),
            pl.BlockSpec((n_heads, 1, V_HEAD_DIM, tm), lambda i: (0, i, 0, 0)),
            pl.BlockSpec((tm, V7X_LANES), lambda i: (i, 0)),
        ],
        out_shape=[
            jax.ShapeDtypeStruct((s, nk), BF16),
            jax.ShapeDtypeStruct((n_heads, s // tm, V_HEAD_DIM, tm), BF16),
            jax.ShapeDtypeStruct((s, V7X_LANES), BF16),
        ],
        compiler_params=_params(("parallel",),
                                [((tm, rank), F32), ((tm, V7X_LANES), F32), ((rank, nk), BF16), ((nv, rank), BF16),
                                 ((tm, nk), BF16), ((nv, tm), BF16), ((tm, V7X_LANES), BF16), ((tm, V7X_LANES), F32)],
                                temps=[((tm, nk), F32), ((nv, tm), F32)]),
        name="kv_proj",
    )(z, z, kv_norm, wk, wv_t, pos_col, freq_row)


def _flash_kernel(kn_ref, kr_ref, qt_ref, vt_ref, o_ref, m_ref, l_ref, acc_ref, *, t):
    qi = pl.program_id(1)
    qt = qt_ref[...]
    m_ref[...] = jnp.full(m_ref.shape, -jnp.inf, F32)
    l_ref[...] = jnp.zeros(l_ref.shape, F32)
    acc_ref[...] = jnp.zeros(acc_ref.shape, F32)

    def step(kb, masked):
        ks = pl.ds(pl.multiple_of(kb * t, t), t)
        kcat = jnp.concatenate([kn_ref[ks, :], kr_ref[ks, :]], axis=1)
        s = jnp.dot(kcat, qt, preferred_element_type=F32)
        if masked:
            kc = lax.broadcasted_iota(jnp.int32, (t, t), 0) // CHUNK
            qc = lax.broadcasted_iota(jnp.int32, (t, t), 1) // CHUNK
            s = jnp.where(kc <= qc, s, -jnp.inf)
        m_prev = m_ref[...]
        m_new = jnp.maximum(m_prev, jnp.max(s, axis=0, keepdims=True))
        alpha = jnp.exp(m_prev - m_new)
        p = jnp.exp(s - m_new)
        l_ref[...] = alpha * l_ref[...] + jnp.sum(p, axis=0, keepdims=True)
        pv = jnp.dot(vt_ref[0, kb], p.astype(BF16), preferred_element_type=F32)
        acc_ref[...] = alpha * acc_ref[...] + pv
        m_ref[...] = m_new

    def body(kb, carry):
        step(kb, masked=False)
        return carry

    lax.fori_loop(0, qi, body, 0)
    step(qi, masked=True)
    o_ref[...] = (acc_ref[...] / l_ref[...]).T.astype(o_ref.dtype)


def _flash_attention(k_nope, k_rope, q_t, v_t, *, n_heads, t):
    s = k_nope.shape[0]
    assert t % CHUNK == 0 and v_t.shape == (n_heads, s // t, V_HEAD_DIM, t)
    return pl.pallas_call(
        functools.partial(_flash_kernel, t=t),
        grid=(n_heads, s // t),
        in_specs=[
            pl.BlockSpec((s, QK_NOPE_DIM), lambda h, i: (0, h)),
            pl.BlockSpec((s, V7X_LANES), lambda h, i: (0, 0)),
            pl.BlockSpec((V7X_MXU_DIM, t), lambda h, i: (h, i)),
            pl.BlockSpec((1, s // t, V_HEAD_DIM, t), lambda h, i: (h, 0, 0, 0)),
        ],
        out_specs=pl.BlockSpec((t, V_HEAD_DIM), lambda h, i: (i, h)),
        out_shape=jax.ShapeDtypeStruct((s, n_heads * V_HEAD_DIM), BF16),
        scratch_shapes=[pltpu.VMEM((1, t), F32), pltpu.VMEM((1, t), F32), pltpu.VMEM((V_HEAD_DIM, t), F32)],
        compiler_params=_params(("arbitrary", "arbitrary"),
                                [((s, QK_NOPE_DIM), BF16), ((s, V7X_LANES), BF16), ((V7X_MXU_DIM, t), BF16),
                                 ((V_HEAD_DIM, s), BF16), ((t, V_HEAD_DIM), BF16)],
                                scratch=[((V_HEAD_DIM + 16, t), F32)],
                                temps=[((t, t), F32), ((t, t), F32), ((t, t), BF16), ((t, V7X_MXU_DIM), BF16)]),
        name="flash_attention",
    )(k_nope, k_rope, q_t, v_t)


def _gated_merge_kernel(pm_ref, at_ref, h_ref, wup_ref, wum_ref, wga_ref, wgb_ref, o_ref):
    h = h_ref[...]
    ya = jnp.dot(pm_ref[...], wup_ref[...], preferred_element_type=F32)
    yb = jnp.dot(at_ref[...], wum_ref[...], preferred_element_type=F32)
    ga = jax.nn.sigmoid(jnp.dot(h, wga_ref[...], preferred_element_type=F32))
    gb = jax.nn.sigmoid(jnp.dot(h, wgb_ref[...], preferred_element_type=F32))
    o_ref[...] = (ga * ya + gb * yb).astype(o_ref.dtype)


def _gated_merge(pm, attn, h, w_up_pool, w_up_mla, w_gate2d, *, tm=512, tn=512):
    s, d = h.shape
    kp, km = pm.shape[1], attn.shape[1]
    nb = d // tn
    return pl.pallas_call(
        _gated_merge_kernel,
        grid=(s // tm, nb),
        in_specs=[
            pl.BlockSpec((tm, kp), lambda i, j: (i, 0)),
            pl.BlockSpec((tm, km), lambda i, j: (i, 0)),
            pl.BlockSpec((tm, d), lambda i, j: (i, 0)),
            pl.BlockSpec((kp, tn), lambda i, j: (0, j)),
            pl.BlockSpec((km, tn), lambda i, j: (0, j)),
            pl.BlockSpec((d, tn), lambda i, j: (0, j)),
            pl.BlockSpec((d, tn), lambda i, j: (0, j + nb)),
        ],
        out_specs=pl.BlockSpec((tm, tn), lambda i, j: (i, j)),
        out_shape=jax.ShapeDtypeStruct((s, d), BF16),
        compiler_params=_params(("parallel", "parallel"),
                                [((tm, kp), BF16), ((tm, km), BF16), ((tm, d), BF16), ((kp, tn), BF16),
                                 ((km, tn), BF16), ((d, tn), BF16), ((d, tn), BF16), ((tm, tn), BF16)],
                                temps=[((tm, tn), F32)] * 4),
        name="gated_merge",
    )(pm, attn, h, w_up_pool, w_up_mla, w_gate2d, w_gate2d)


def _residual_norm_kernel(t_ref, x_ref, gpost_ref, gnext_ref, xo_ref, ho_ref):
    xo = x_ref[...] + _rms(t_ref[...].astype(F32), gpost_ref[...])
    xo_ref[...] = xo
    ho_ref[...] = _rms(xo, gnext_ref[...]).astype(ho_ref.dtype)


def _residual_final_kernel(t_ref, x_ref, gpost_ref, xo_ref):
    xo_ref[...] = x_ref[...] + _rms(t_ref[...].astype(F32), gpost_ref[...])


def _residual_norm(t, x, g_post, g_next=None, *, tm=256):
    s, d = x.shape
    row = pl.BlockSpec((tm, d), lambda i: (i, 0))
    vec = pl.BlockSpec((1, d), lambda i: (0, 0))
    blocks = [((tm, d), t.dtype), ((tm, d), F32), ((tm, d), F32)]
    if g_next is None:
        return pl.pallas_call(
            _residual_final_kernel, grid=(s // tm,), in_specs=[row, row, vec], out_specs=row,
            out_shape=jax.ShapeDtypeStruct((s, d), F32),
            compiler_params=_params(("parallel",), blocks, temps=[((tm, d), F32)] * 2),
            name="residual_final",
        )(t, x, g_post)
    return pl.pallas_call(
        _residual_norm_kernel, grid=(s // tm,), in_specs=[row, row, vec, vec], out_specs=[row, row],
        out_shape=[jax.ShapeDtypeStruct((s, d), F32), jax.ShapeDtypeStruct((s, d), BF16)],
        compiler_params=_params(("parallel",), blocks + [((tm, d), BF16)], temps=[((tm, d), F32)] * 2),
        name="residual_norm",
    )(t, x, g_post, g_next)


def _swiglu_up_kernel(h_ref, wg_ref, wu_ref, o_ref):
    h = h_ref[...]
    gate = jnp.dot(h, wg_ref[...], preferred_element_type=F32)
    up = jnp.dot(h, wu_ref[...], preferred_element_type=F32)
    o_ref[...] = (gate * jax.nn.sigmoid(gate) * up).astype(o_ref.dtype)


def _swiglu_up(h, w_gate, w_up, *, tm=1024, tn=256):
    s, d = h.shape
    f = w_gate.shape[1]
    return pl.pallas_call(
        _swiglu_up_kernel,
        grid=(s // tm, f // tn),
        in_specs=[
            pl.BlockSpec((tm, d), lambda i, j: (i, 0)),
            pl.BlockSpec((d, tn), lambda i, j: (0, j)),
            pl.BlockSpec((d, tn), lambda i, j: (0, j)),
        ],
        out_specs=pl.BlockSpec((tm, tn), lambda i, j: (i, j)),
        out_shape=jax.ShapeDtypeStruct((s, f), BF16),
        compiler_params=_params(("parallel", "parallel"),
                                [((tm, d), BF16), ((d, tn), BF16), ((d, tn), BF16), ((tm, tn), BF16)],
                                temps=[((tm, tn), F32)] * 3),
        name="swiglu_up",
    )(h, w_gate, w_up)


def _ple_gate_kernel(h_ref, p_ref, wg_ref, wp_ref, o_ref):
    gate = jax.nn.sigmoid(jnp.dot(h_ref[...], wg_ref[...], preferred_element_type=F32))
    pe = jnp.dot(p_ref[...].astype(BF16), wp_ref[...], preferred_element_type=F32)
    o_ref[...] = (pe * gate).astype(o_ref.dtype)


def _ple_gate(h, p, w_gate, w_proj, *, tm=1024, tn=512):
    s, d = h.shape
    r = p.shape[1]
    return pl.pallas_call(
        _ple_gate_kernel,
        grid=(s // tm, d // tn),
        in_specs=[
            pl.BlockSpec((tm, d), lambda i, j: (i, 0)),
            pl.BlockSpec((tm, r), lambda i, j: (i, 0)),
            pl.BlockSpec((d, tn), lambda i, j: (0, j)),
            pl.BlockSpec((r, tn), lambda i, j: (0, j)),
        ],
        out_specs=pl.BlockSpec((tm, tn), lambda i, j: (i, j)),
        out_shape=jax.ShapeDtypeStruct((s, d), F32),
        compiler_params=_params(("parallel", "parallel"),
                                [((tm, d), BF16), ((tm, r), F32), ((d, tn), BF16), ((r, tn), BF16), ((tm, tn), F32)],
                                temps=[((tm, tn), F32)] * 2),
        name="ple_gate",
    )(h, p, w_gate, w_proj)


def kernel(x, p, positions, norm_mix_pre, norm_mix_post, w_in, q_norm, kv_norm, w_q_b, w_kv_b, w_pool, pool_scale,
           w_up_pool, w_up_mla, w_branch_gate, w_out, norm_ffn_pre, norm_ffn_post, w_ffn_gate, w_ffn_up,
           w_ffn_down, norm_ple_pre, w_ple_gate, w_ple_proj, norm_ple_post):
    batch, seq, d_model = x.shape
    depth = w_in.shape[0]
    in_width = w_in.shape[2]
    q_rank, n_heads, qk_dim = w_q_b.shape[1:]
    kv_rank = w_kv_b.shape[1]
    pool_width = w_up_pool.shape[1]
    assert batch == 1 and qk_dim == QK_NOPE_DIM + QK_ROPE_DIM
    assert w_kv_b.shape[3] == QK_NOPE_DIM + V_HEAD_DIM
    o_q, o_kv, o_kr = pool_width, pool_width + q_rank, pool_width + q_rank + kv_rank
    assert o_kr + QK_ROPE_DIM == in_width
    assert o_q % q_rank == 0 and o_kv % kv_rank == 0 and o_kr % V7X_LANES == 0
    z_tn = 768
    attn_t = 512
    z_width = -(-(o_kr + V7X_LANES) // z_tn) * z_tn
    half = QK_ROPE_DIM // 2
    scale = float(qk_dim) ** -0.5

    inv_freq = ROPE_THETA ** (-jnp.arange(0, QK_ROPE_DIM, 2, dtype=F32) / QK_ROPE_DIM)
    freq_col = inv_freq.reshape(half, 1)
    freq_row = jnp.zeros((1, V7X_LANES), F32).at[0, :half].set(inv_freq).at[0, half:QK_ROPE_DIM].set(inv_freq)
    pos_f = positions.astype(F32)
    pos_row, pos_col = pos_f.reshape(1, seq), pos_f.reshape(seq, 1)

    xs = x.reshape(seq, d_model)
    for i in range(depth):
        w_in_p = jnp.pad(w_in[i].astype(BF16), ((0, 0), (0, z_width - in_width)))
        wq = w_q_b[i].astype(BF16)
        wq_t = jnp.pad(jnp.transpose(wq, (1, 2, 0)), ((0, 0), (0, V7X_MXU_DIM - qk_dim), (0, 0)))
        wq_t = wq_t.reshape(n_heads * V7X_MXU_DIM, q_rank)
        wkv = w_kv_b[i].astype(BF16)
        wk = wkv[:, :, :QK_NOPE_DIM].reshape(kv_rank, n_heads * QK_NOPE_DIM)
        wv_t = jnp.transpose(wkv[:, :, QK_NOPE_DIM:], (1, 2, 0)).reshape(n_heads * V_HEAD_DIM, kv_rank)
        w_gate2d = w_branch_gate[i].astype(BF16).reshape(d_model, 2 * d_model)

        def row(v):
            return v[i].reshape(1, -1)

        h = _rmsnorm_cast(xs, row(norm_mix_pre))
        z = _matmul(h, w_in_p, tm=1024, tn=z_tn, out_dtype=F32, name="in_proj")
        pm = _pool_mixer(z, w_pool[i].astype(BF16), row(pool_scale), pool_width=pool_width)
        q_t = _q_proj(z, row(q_norm), wq_t, pos_row, freq_col, col_block=o_q // q_rank, rank=q_rank, scale=scale)
        k_nope, v_t, k_rope = _kv_proj(z, row(kv_norm), wk, wv_t, pos_col, freq_row, lat_block=o_kv // kv_rank,
                                       rope_block=o_kr // V7X_LANES, rank=kv_rank, tm=attn_t)
        attn = _flash_attention(k_nope, k_rope, q_t, v_t, n_heads=n_heads, t=attn_t)
        merged = _gated_merge(pm, attn, h, w_up_pool[i].astype(BF16), w_up_mla[i].astype(BF16), w_gate2d)
        mix = _matmul(merged, w_out[i].astype(BF16), tm=1024, tn=1024, out_dtype=F32, name="out_proj")
        xs, h2 = _residual_norm(mix, xs, row(norm_mix_post), row(norm_ffn_pre))
        act = _swiglu_up(h2, w_ffn_gate[i].astype(BF16), w_ffn_up[i].astype(BF16))
        ffn = _matmul(act, w_ffn_down[i].astype(BF16), tm=512, tn=512, out_dtype=F32, name="ffn_down")
        xs, h3 = _residual_norm(ffn, xs, row(norm_ffn_post), row(norm_ple_pre))
        t = _ple_gate(h3, p[i].reshape(seq, -1), w_ple_gate[i].astype(BF16),
                      w_ple_proj[i].astype(BF16))
        xs = _residual_norm(t, xs, row(norm_ple_post))
    return xs.reshape(batch, seq, d_model)
```

```python
import functools

import jax
import jax.numpy as jnp
from jax import lax
from jax.experimental import pallas as pl
from jax.experimental.pallas import tpu as pltpu

CHUNK = 64
EPS = 1e-6
POOL_WINDOWS = (2, 4, 8, 16)
QK_NOPE_DIM = 128
QK_ROPE_DIM = 64
V_HEAD_DIM = 128
ROPE_THETA = 10000.0
LOG2_E = 1.4426950408889634

V7X_LANES = 128
V7X_MXU_DIM = 256
V7X_VMEM_BYTES = 64 * 1024 * 1024
V7X_VMEM_USABLE_BYTES = 60000 * 1024
V7X_VMEM_DEFAULT_SCOPED_BYTES = 32 * 1024 * 1024

POOL_HALO = 16
F32 = jnp.float32
BF16 = jnp.bfloat16


def _nbytes(shape, dtype):
    n = 1
    for s in shape:
        n *= s
    return n * jnp.dtype(dtype).itemsize


def _params(semantics, blocks, scratch=(), temps=()):
    need = 2 * sum(_nbytes(s, d) for s, d in blocks)
    need += sum(_nbytes(s, d) for s, d in scratch) + sum(_nbytes(s, d) for s, d in temps)
    need = max(int(need * 1.25) + (2 << 20), V7X_VMEM_DEFAULT_SCOPED_BYTES)
    return pltpu.CompilerParams(dimension_semantics=semantics,
                                vmem_limit_bytes=min(need, V7X_VMEM_USABLE_BYTES))


def _load_bf16(ref):
    v = ref[...]
    return v if v.dtype == BF16 else v.astype(BF16)


def _rms(xf, g):
    return xf * lax.rsqrt(jnp.mean(xf * xf, axis=-1, keepdims=True) + EPS) * g


def _rmsnorm_cast_kernel(x_ref, g_ref, o_ref):
    o_ref[...] = _rms(x_ref[...], g_ref[...]).astype(o_ref.dtype)


def _rmsnorm_cast(x, g, *, tm=256):
    s, d = x.shape
    return pl.pallas_call(
        _rmsnorm_cast_kernel,
        grid=(s // tm,),
        in_specs=[pl.BlockSpec((tm, d), lambda i: (i, 0)), pl.BlockSpec((1, d), lambda i: (0, 0))],
        out_specs=pl.BlockSpec((tm, d), lambda i: (i, 0)),
        out_shape=jax.ShapeDtypeStruct((s, d), BF16),
        compiler_params=_params(("parallel",), [((tm, d), F32), ((tm, d), BF16)], temps=[((tm, d), F32)]),
        name="rmsnorm_cast",
    )(x, g)


def _matmul_kernel(a_ref, b_ref, o_ref):
    o_ref[...] = jnp.dot(a_ref[...], _load_bf16(b_ref), preferred_element_type=F32).astype(o_ref.dtype)


def _matmul(a, b, *, tm, tn, out_dtype, name):
    m, k = a.shape
    _, n = b.shape
    return pl.pallas_call(
        _matmul_kernel,
        grid=(m // tm, n // tn),
        in_specs=[pl.BlockSpec((tm, k), lambda i, j: (i, 0)), pl.BlockSpec((k, tn), lambda i, j: (0, j))],
        out_specs=pl.BlockSpec((tm, tn), lambda i, j: (i, j)),
        out_shape=jax.ShapeDtypeStruct((m, n), out_dtype),
        compiler_params=_params(("parallel", "parallel"),
                                [((tm, k), a.dtype), ((k, tn), b.dtype), ((tm, tn), out_dtype)],
                                temps=[((tm, tn), F32), ((k, tn), BF16)]),
        name=name,
    )(a, b)


def _pool_mixer_kernel(halo_ref, u_ref, w_ref, scale_ref, o_ref, ext_ref, *, tm, gw):
    i = pl.program_id(0)
    ext_ref[0:POOL_HALO, :] = jnp.where(i > 0, halo_ref[...], 0.0)
    ext_ref[POOL_HALO:POOL_HALO + tm, :] = u_ref[...]
    row = lax.broadcasted_iota(jnp.int32, (tm, 1), 0) + i * tm
    for g, w in enumerate(POOL_WINDOWS):
        cols = slice(g * gw, (g + 1) * gw)
        u = u_ref[:, cols]
        win_sum = u
        for back in range(1, w):
            win_sum = win_sum + ext_ref[POOL_HALO - back:POOL_HALO - back + tm, cols]
        cnt = jnp.minimum(row + 1, w).astype(F32)
        pooled = win_sum / cnt - u
        mixed = jnp.dot(pooled.astype(BF16), w_ref[g], preferred_element_type=F32)
        o_ref[:, cols] = (mixed * scale_ref[:, cols]).astype(o_ref.dtype)


def _pool_mixer(z, w_pool, pool_scale, *, pool_width, tm=512):
    s = z.shape[0]
    groups, gw, _ = w_pool.shape
    halo_blocks = tm // POOL_HALO
    return pl.pallas_call(
        functools.partial(_pool_mixer_kernel, tm=tm, gw=gw),
        grid=(s // tm,),
        in_specs=[
            pl.BlockSpec((POOL_HALO, pool_width), lambda i: (jnp.maximum(i * halo_blocks - 1, 0), 0)),
            pl.BlockSpec((tm, pool_width), lambda i: (i, 0)),
            pl.BlockSpec((groups, gw, gw), lambda i: (0, 0, 0)),
            pl.BlockSpec((1, pool_width), lambda i: (0, 0)),
        ],
        out_specs=pl.BlockSpec((tm, pool_width), lambda i: (i, 0)),
        out_shape=jax.ShapeDtypeStruct((s, pool_width), BF16),
        scratch_shapes=[pltpu.VMEM((POOL_HALO + tm, pool_width), F32)],
        compiler_params=_params(("parallel",),
                                [((tm, pool_width), F32), ((groups, gw, gw), BF16), ((tm, pool_width), BF16)],
                                scratch=[((POOL_HALO + tm, pool_width), F32)],
                                temps=[((tm, pool_width), F32)]),
        name="pool_mixer",
    )(z, z, w_pool, pool_scale)


def _q_proj_kernel(ql_ref, g_ref, w_ref, pos_ref, freq_ref, o_ref, qn_ref, cos_ref, sin_ref, *, scale):
    @pl.when(pl.program_id(1) == 0)
    def _():
        qn_ref[...] = (_rms(ql_ref[...], g_ref[...]) * scale).astype(qn_ref.dtype)
        ang = freq_ref[...] * pos_ref[...]
        cos_ref[...] = jnp.cos(ang)
        sin_ref[...] = jnp.sin(ang)

    qt = lax.dot_general(w_ref[...], qn_ref[...], (((1,), (1,)), ((), ())), preferred_element_type=F32)
    half = QK_ROPE_DIM // 2
    r0, r1, r2 = QK_NOPE_DIM, QK_NOPE_DIM + half, QK_NOPE_DIM + QK_ROPE_DIM
    x1, x2 = qt[r0:r1], qt[r1:r2]
    c, sn = cos_ref[...], sin_ref[...]
    o_ref[0:r0, :] = qt[0:r0].astype(o_ref.dtype)
    o_ref[r0:r1, :] = (x1 * c - x2 * sn).astype(o_ref.dtype)
    o_ref[r1:r2, :] = (x1 * sn + x2 * c).astype(o_ref.dtype)
    o_ref[r2:, :] = qt[r2:].astype(o_ref.dtype)


def _q_proj(z, q_norm, wq_t, pos_row, freq_col, *, col_block, rank, scale, tm=512):
    s = z.shape[0]
    rows = V7X_MXU_DIM
    n_heads = wq_t.shape[0] // rows
    half = QK_ROPE_DIM // 2
    return pl.pallas_call(
        functools.partial(_q_proj_kernel, scale=scale),
        grid=(s // tm, n_heads),
        in_specs=[
            pl.BlockSpec((tm, rank), lambda i, h: (i, col_block)),
            pl.BlockSpec((1, rank), lambda i, h: (0, 0)),
            pl.BlockSpec((rows, rank), lambda i, h: (h, 0)),
            pl.BlockSpec((1, tm), lambda i, h: (0, i)),
            pl.BlockSpec((half, 1), lambda i, h: (0, 0)),
        ],
        out_specs=pl.BlockSpec((rows, tm), lambda i, h: (h, i)),
        out_shape=jax.ShapeDtypeStruct((n_heads * rows, s), BF16),
        scratch_shapes=[pltpu.VMEM((tm, rank), BF16), pltpu.VMEM((half, tm), F32), pltpu.VMEM((half, tm), F32)],
        compiler_params=_params(("parallel", "arbitrary"),
                                [((tm, rank), F32), ((rows, rank), BF16), ((rows, tm), BF16)],
                                scratch=[((tm, rank), BF16), ((2 * half, tm), F32)],
                                temps=[((tm, rank), F32), ((rows, tm), F32)]),
        name="q_proj",
    )(z, q_norm, wq_t, pos_row, freq_col)


def _kv_proj_kernel(kvl_ref, kr_ref, g_ref, wk_ref, wvt_ref, pos_ref, freq_ref, k_ref, vt_ref, kro_ref):
    kvn = _rms(kvl_ref[...], g_ref[...]).astype(BF16)
    k_ref[...] = jnp.dot(kvn, wk_ref[...], preferred_element_type=F32).astype(k_ref.dtype)
    vt = lax.dot_general(wvt_ref[...], kvn, (((1,), (1,)), ((), ())), preferred_element_type=F32)
    vt_ref[:, 0] = vt.reshape(vt_ref.shape[0], V_HEAD_DIM, vt.shape[1]).astype(vt_ref.dtype)
    half = QK_ROPE_DIM // 2
    x = kr_ref[...]
    ang = pos_ref[...] * freq_ref[...]
    lane = lax.broadcasted_iota(jnp.int32, x.shape, 1)
    x2_at_lo = pltpu.roll(x, V7X_LANES - half, axis=1)
    x1_at_hi = pltpu.roll(x, half, axis=1)
    partner = jnp.where(lane < half, -x2_at_lo, jnp.where(lane < QK_ROPE_DIM, x1_at_hi, 0.0))
    kro_ref[...] = (x * jnp.cos(ang) + partner * jnp.sin(ang)).astype(kro_ref.dtype)


def _kv_proj(z, kv_norm, wk, wv_t, pos_col, freq_row, *, lat_block, rope_block, rank, tm):
    s = z.shape[0]
    nk = wk.shape[1]
    nv = wv_t.shape[0]
    n_heads = nv // V_HEAD_DIM
    return pl.pallas_call(
        _kv_proj_kernel,
        grid=(s // tm,),
        in_specs=[
            pl.BlockSpec((tm, rank), lambda i: (i, lat_block)),
            pl.BlockSpec((tm, V7X_LANES), lambda i: (i, rope_block)),
            pl.BlockSpec((1, rank), lambda i: (0, 0)),
            pl.BlockSpec((rank, nk), lambda i: (0, 0)),
            pl.BlockSpec((nv, rank), lambda i: (0, 0)),
            pl.BlockSpec((tm, 1), lambda i: (i, 0)),
            pl.BlockSpec((1, V7X_LANES), lambda i: (0, 0)),
        ],
        out_specs=[
            pl.BlockSpec((tm, nk), lambda i: (i, 0)),
            pl.BlockSpec((n_heads, 1, V_HEAD_DIM, tm), lambda i: (0, i, 0, 0)),
            pl.BlockSpec((tm, V7X_LANES), lambda i: (i, 0)),
        ],
        out_shape=[
            jax.ShapeDtypeStruct((s, nk), BF16),
            jax.ShapeDtypeStruct((n_heads, s // tm, V_HEAD_DIM, tm), BF16),
            jax.ShapeDtypeStruct((s, V7X_LANES), BF16),
        ],
        compiler_params=_params(("parallel",),
                                [((tm, rank), F32), ((tm, V7X_LANES), F32), ((rank, nk), BF16), ((nv, rank), BF16),
                                 ((tm, nk), BF16), ((nv, tm), BF16), ((tm, V7X_LANES), BF16), ((tm, V7X_LANES), F32)],
                                temps=[((tm, nk), F32), ((nv, tm), F32)]),
        name="kv_proj",
    )(z, z, kv_norm, wk, wv_t, pos_col, freq_row)


def _flash_kernel(kn_ref, kr_ref, qt_ref, vt_ref, o_ref, s0_ref, s1_ref, m_ref, l_ref, acc_ref, *, t):
    qi = pl.program_id(1)
    qt = qt_ref[...]
    m_ref[...] = jnp.full(m_ref.shape, -jnp.inf, F32)
    l_ref[...] = jnp.zeros(l_ref.shape, F32)
    acc_ref[...] = jnp.zeros(acc_ref.shape, F32)

    def scores(kb, s_ref):
        ks = pl.ds(pl.multiple_of(kb * t, t), t)
        kcat = jnp.concatenate([kn_ref[ks, :], kr_ref[ks, :]], axis=1)
        s_ref[...] = jnp.dot(kcat, qt, preferred_element_type=F32)

    def softmax_pv(kb, s_ref, masked):
        s = s_ref[...]
        if masked:
            kc = lax.broadcasted_iota(jnp.int32, (t, t), 0) // CHUNK
            qc = lax.broadcasted_iota(jnp.int32, (t, t), 1) // CHUNK
            s = jnp.where(kc <= qc, s, -jnp.inf)
        m_prev = m_ref[...]
        m_new = jnp.maximum(m_prev, jnp.max(s, axis=0, keepdims=True))
        alpha = jnp.exp2(m_prev - m_new)
        p = jnp.exp2(s - m_new)
        l_ref[...] = alpha * l_ref[...] + jnp.sum(p, axis=0, keepdims=True)
        pv = jnp.dot(vt_ref[0, kb], p.astype(BF16), preferred_element_type=F32)
        acc_ref[...] = alpha * acc_ref[...] + pv
        m_ref[...] = m_new

    scores(0, s0_ref)

    def pair(pi, carry):
        kb = 2 * pi
        scores(kb + 1, s1_ref)
        softmax_pv(kb, s0_ref, masked=False)
        scores(kb + 2, s0_ref)
        softmax_pv(kb + 1, s1_ref, masked=False)
        return carry

    lax.fori_loop(0, qi // 2, pair, 0)

    @pl.when(qi % 2 == 1)
    def _():
        scores(qi, s1_ref)
        softmax_pv(qi - 1, s0_ref, masked=False)
        softmax_pv(qi, s1_ref, masked=True)

    @pl.when(qi % 2 == 0)
    def _():
        softmax_pv(qi, s0_ref, masked=True)

    o_ref[...] = (acc_ref[...] / l_ref[...]).T.astype(o_ref.dtype)


def _flash_attention(k_nope, k_rope, q_t, v_t, *, n_heads, t):
    s = k_nope.shape[0]
    assert t % CHUNK == 0 and v_t.shape == (n_heads, s // t, V_HEAD_DIM, t)
    return pl.pallas_call(
        functools.partial(_flash_kernel, t=t),
        grid=(n_heads, s // t),
        in_specs=[
            pl.BlockSpec((s, QK_NOPE_DIM), lambda h, i: (0, h)),
            pl.BlockSpec((s, V7X_LANES), lambda h, i: (0, 0)),
            pl.BlockSpec((V7X_MXU_DIM, t), lambda h, i: (h, i)),
            pl.BlockSpec((1, s // t, V_HEAD_DIM, t), lambda h, i: (h, 0, 0, 0)),
        ],
        out_specs=pl.BlockSpec((t, V_HEAD_DIM), lambda h, i: (i, h)),
        out_shape=jax.ShapeDtypeStruct((s, n_heads * V_HEAD_DIM), BF16),
        scratch_shapes=[pltpu.VMEM((t, t), F32), pltpu.VMEM((t, t), F32), pltpu.VMEM((1, t), F32),
                        pltpu.VMEM((1, t), F32), pltpu.VMEM((V_HEAD_DIM, t), F32)],
        compiler_params=_params(("arbitrary", "arbitrary"),
                                [((s, QK_NOPE_DIM), BF16), ((s, V7X_LANES), BF16), ((V7X_MXU_DIM, t), BF16),
                                 ((V_HEAD_DIM, s), BF16), ((t, V_HEAD_DIM), BF16)],
                                scratch=[((2 * t + V_HEAD_DIM + 16, t), F32)],
                                temps=[((t, t), F32), ((t, t), F32), ((t, t), BF16), ((t, V7X_MXU_DIM), BF16)]),
        name="flash_attention",
    )(k_nope, k_rope, q_t, v_t)


def _gated_merge_kernel(pm_ref, at_ref, h_ref, wup_ref, wum_ref, wga_ref, wgb_ref, o_ref):
    h = h_ref[...]
    ya = jnp.dot(pm_ref[...], wup_ref[...], preferred_element_type=F32)
    yb = jnp.dot(at_ref[...], wum_ref[...], preferred_element_type=F32)
    ga = jax.nn.sigmoid(jnp.dot(h, wga_ref[...], preferred_element_type=F32))
    gb = jax.nn.sigmoid(jnp.dot(h, wgb_ref[...], preferred_element_type=F32))
    o_ref[...] = (ga * ya + gb * yb).astype(o_ref.dtype)


def _gated_merge(pm, attn, h, w_up_pool, w_up_mla, w_gate2d, *, tm=512, tn=512):
    s, d = h.shape
    kp, km = pm.shape[1], attn.shape[1]
    nb = d // tn
    return pl.pallas_call(
        _gated_merge_kernel,
        grid=(s // tm, nb),
        in_specs=[
            pl.BlockSpec((tm, kp), lambda i, j: (i, 0)),
            pl.BlockSpec((tm, km), lambda i, j: (i, 0)),
            pl.BlockSpec((tm, d), lambda i, j: (i, 0)),
            pl.BlockSpec((kp, tn), lambda i, j: (0, j)),
            pl.BlockSpec((km, tn), lambda i, j: (0, j)),
            pl.BlockSpec((d, tn), lambda i, j: (0, j)),
            pl.BlockSpec((d, tn), lambda i, j: (0, j + nb)),
        ],
        out_specs=pl.BlockSpec((tm, tn), lambda i, j: (i, j)),
        out_shape=jax.ShapeDtypeStruct((s, d), BF16),
        compiler_params=_params(("parallel", "parallel"),
                                [((tm, kp), BF16), ((tm, km), BF16), ((tm, d), BF16), ((kp, tn), BF16),
                                 ((km, tn), BF16), ((d, tn), BF16), ((d, tn), BF16), ((tm, tn), BF16)],
                                temps=[((tm, tn), F32)] * 4),
        name="gated_merge",
    )(pm, attn, h, w_up_pool, w_up_mla, w_gate2d, w_gate2d)


def _residual_norm_kernel(t_ref, x_ref, gpost_ref, gnext_ref, xo_ref, ho_ref):
    xo = x_ref[...] + _rms(t_ref[...].astype(F32), gpost_ref[...])
    xo_ref[...] = xo
    ho_ref[...] = _rms(xo, gnext_ref[...]).astype(ho_ref.dtype)


def _residual_final_kernel(t_ref, x_ref, gpost_ref, xo_ref):
    xo_ref[...] = x_ref[...] + _rms(t_ref[...].astype(F32), gpost_ref[...])


def _residual_norm(t, x, g_post, g_next=None, *, tm=256):
    s, d = x.shape
    row = pl.BlockSpec((tm, d), lambda i: (i, 0))
    vec = pl.BlockSpec((1, d), lambda i: (0, 0))
    blocks = [((tm, d), t.dtype), ((tm, d), F32), ((tm, d), F32)]
    if g_next is None:
        return pl.pallas_call(
            _residual_final_kernel, grid=(s // tm,), in_specs=[row, row, vec], out_specs=row,
            out_shape=jax.ShapeDtypeStruct((s, d), F32),
            compiler_params=_params(("parallel",), blocks, temps=[((tm, d), F32)] * 2),
            name="residual_final",
        )(t, x, g_post)
    return pl.pallas_call(
        _residual_norm_kernel, grid=(s // tm,), in_specs=[row, row, vec, vec], out_specs=[row, row],
        out_shape=[jax.ShapeDtypeStruct((s, d), F32), jax.ShapeDtypeStruct((s, d), BF16)],
        compiler_params=_params(("parallel",), blocks + [((tm, d), BF16)], temps=[((tm, d), F32)] * 2),
        name="residual_norm",
    )(t, x, g_post, g_next)


def _swiglu_up_kernel(h_ref, wg_ref, wu_ref, o_ref):
    h = h_ref[...]
    gate = jnp.dot(h, _load_bf16(wg_ref), preferred_element_type=F32)
    up = jnp.dot(h, _load_bf16(wu_ref), preferred_element_type=F32)
    o_ref[...] = (gate * jax.nn.sigmoid(gate) * up).astype(o_ref.dtype)


def _swiglu_up(h, w_gate, w_up, *, tm=1024, tn=256):
    s, d = h.shape
    f = w_gate.shape[1]
    return pl.pallas_call(
        _swiglu_up_kernel,
        grid=(s // tm, f // tn),
        in_specs=[
            pl.BlockSpec((tm, d), lambda i, j: (i, 0)),
            pl.BlockSpec((d, tn), lambda i, j: (0, j)),
            pl.BlockSpec((d, tn), lambda i, j: (0, j)),
        ],
        out_specs=pl.BlockSpec((tm, tn), lambda i, j: (i, j)),
        out_shape=jax.ShapeDtypeStruct((s, f), BF16),
        compiler_params=_params(("parallel", "parallel"),
                                [((tm, d), BF16), ((d, tn), w_gate.dtype), ((d, tn), w_up.dtype), ((tm, tn), BF16)],
                                temps=[((tm, tn), F32)] * 3 + [((d, tn), BF16)] * 2),
        name="swiglu_up",
    )(h, w_gate, w_up)


def _ple_gate_kernel(h_ref, p_ref, wg_ref, wp_ref, o_ref):
    gate = jax.nn.sigmoid(jnp.dot(h_ref[...], _load_bf16(wg_ref), preferred_element_type=F32))
    pe = jnp.dot(_load_bf16(p_ref), _load_bf16(wp_ref), preferred_element_type=F32)
    o_ref[...] = (pe * gate).astype(o_ref.dtype)


def _ple_gate(h, p, w_gate, w_proj, *, tm=1024, tn=512):
    s, d = h.shape
    r = p.shape[1]
    return pl.pallas_call(
        _ple_gate_kernel,
        grid=(s // tm, d // tn),
        in_specs=[
            pl.BlockSpec((tm, d), lambda i, j: (i, 0)),
            pl.BlockSpec((tm, r), lambda i, j: (i, 0)),
            pl.BlockSpec((d, tn), lambda i, j: (0, j)),
            pl.BlockSpec((r, tn), lambda i, j: (0, j)),
        ],
        out_specs=pl.BlockSpec((tm, tn), lambda i, j: (i, j)),
        out_shape=jax.ShapeDtypeStruct((s, d), BF16),
        compiler_params=_params(("parallel", "parallel"),
                                [((tm, d), BF16), ((tm, r), p.dtype), ((d, tn), w_gate.dtype), ((r, tn), w_proj.dtype),
                                 ((tm, tn), BF16)],
                                temps=[((tm, tn), F32)] * 2 + [((d, tn), BF16)]),
        name="ple_gate",
    )(h, p, w_gate, w_proj)


def kernel(x, p, positions, norm_mix_pre, norm_mix_post, w_in, q_norm, kv_norm, w_q_b, w_kv_b, w_pool, pool_scale,
           w_up_pool, w_up_mla, w_branch_gate, w_out, norm_ffn_pre, norm_ffn_post, w_ffn_gate, w_ffn_up,
           w_ffn_down, norm_ple_pre, w_ple_gate, w_ple_proj, norm_ple_post):
    batch, seq, d_model = x.shape
    depth = w_in.shape[0]
    in_width = w_in.shape[2]
    q_rank, n_heads, qk_dim = w_q_b.shape[1:]
    kv_rank = w_kv_b.shape[1]
    pool_width = w_up_pool.shape[1]
    assert batch == 1 and qk_dim == QK_NOPE_DIM + QK_ROPE_DIM
    assert w_kv_b.shape[3] == QK_NOPE_DIM + V_HEAD_DIM
    o_q, o_kv, o_kr = pool_width, pool_width + q_rank, pool_width + q_rank + kv_rank
    assert o_kr + QK_ROPE_DIM == in_width
    assert o_q % q_rank == 0 and o_kv % kv_rank == 0 and o_kr % V7X_LANES == 0
    z_tn = 768
    attn_t = 512
    z_width = -(-(o_kr + V7X_LANES) // z_tn) * z_tn
    half = QK_ROPE_DIM // 2
    scale = float(qk_dim) ** -0.5 * LOG2_E

    inv_freq = ROPE_THETA ** (-jnp.arange(0, QK_ROPE_DIM, 2, dtype=F32) / QK_ROPE_DIM)
    freq_col = inv_freq.reshape(half, 1)
    freq_row = jnp.zeros((1, V7X_LANES), F32).at[0, :half].set(inv_freq).at[0, half:QK_ROPE_DIM].set(inv_freq)
    pos_f = positions.astype(F32)
    pos_row, pos_col = pos_f.reshape(1, seq), pos_f.reshape(seq, 1)

    xs = x.reshape(seq, d_model)
    for i in range(depth):
        w_in_p = jnp.pad(w_in[i].astype(BF16), ((0, 0), (0, z_width - in_width)))
        wq = w_q_b[i].astype(BF16)
        wq_t = jnp.pad(jnp.transpose(wq, (1, 2, 0)), ((0, 0), (0, V7X_MXU_DIM - qk_dim), (0, 0)))
        wq_t = wq_t.reshape(n_heads * V7X_MXU_DIM, q_rank)
        wkv = w_kv_b[i].astype(BF16)
        wk = wkv[:, :, :QK_NOPE_DIM].reshape(kv_rank, n_heads * QK_NOPE_DIM)
        wv_t = jnp.transpose(wkv[:, :, QK_NOPE_DIM:], (1, 2, 0)).reshape(n_heads * V_HEAD_DIM, kv_rank)
        w_gate2d = w_branch_gate[i].astype(BF16).reshape(d_model, 2 * d_model)

        def row(v):
            return v[i].reshape(1, -1)

        h = _rmsnorm_cast(xs, row(norm_mix_pre))
        z = _matmul(h, w_in_p, tm=1024, tn=z_tn, out_dtype=F32, name="in_proj")
        pm = _pool_mixer(z, w_pool[i].astype(BF16), row(pool_scale), pool_width=pool_width)
        q_t = _q_proj(z, row(q_norm), wq_t, pos_row, freq_col, col_block=o_q // q_rank, rank=q_rank, scale=scale)
        k_nope, v_t, k_rope = _kv_proj(z, row(kv_norm), wk, wv_t, pos_col, freq_row, lat_block=o_kv // kv_rank,
                                       rope_block=o_kr // V7X_LANES, rank=kv_rank, tm=attn_t)
        attn = _flash_attention(k_nope, k_rope, q_t, v_t, n_heads=n_heads, t=attn_t)
        merged = _gated_merge(pm, attn, h, w_up_pool[i].astype(BF16), w_up_mla[i].astype(BF16), w_gate2d)
        mix = _matmul(merged, w_out[i], tm=1024, tn=512, out_dtype=BF16, name="out_proj")
        xs, h2 = _residual_norm(mix, xs, row(norm_mix_post), row(norm_ffn_pre))
        act = _swiglu_up(h2, w_ffn_gate[i], w_ffn_up[i])
        ffn = _matmul(act, w_ffn_down[i].astype(BF16), tm=512, tn=512, out_dtype=BF16, name="ffn_down")
        xs, h3 = _residual_norm(ffn, xs, row(norm_ffn_post), row(norm_ple_pre))
        t = _ple_gate(h3, p[i].reshape(seq, -1), w_ple_gate[i], w_ple_proj[i])
        xs = _residual_norm(t, xs, row(norm_ple_post))
    return xs.reshape(batch, seq, d_model)
```

```python
import functools

import jax
import jax.numpy as jnp
from jax import lax
from jax.experimental import pallas as pl
from jax.experimental.pallas import tpu as pltpu

CHUNK = 64
EPS = 1e-6
POOL_WINDOWS = (2, 4, 8, 16)
QK_NOPE_DIM = 128
QK_ROPE_DIM = 64
V_HEAD_DIM = 128
ROPE_THETA = 10000.0
LOG2_E = 1.4426950408889634

V7X_LANES = 128
V7X_MXU_DIM = 256
V7X_VMEM_BYTES = 64 * 1024 * 1024
V7X_VMEM_USABLE_BYTES = 60000 * 1024
V7X_VMEM_DEFAULT_SCOPED_BYTES = 32 * 1024 * 1024

POOL_HALO = 16
F32 = jnp.float32
BF16 = jnp.bfloat16


def _nbytes(shape, dtype):
    n = 1
    for s in shape:
        n *= s
    return n * jnp.dtype(dtype).itemsize


def _params(semantics, blocks, scratch=(), temps=()):
    need = 2 * sum(_nbytes(s, d) for s, d in blocks)
    need += sum(_nbytes(s, d) for s, d in scratch) + sum(_nbytes(s, d) for s, d in temps)
    need = max(int(need * 1.25) + (2 << 20), V7X_VMEM_DEFAULT_SCOPED_BYTES)
    return pltpu.CompilerParams(dimension_semantics=semantics,
                                vmem_limit_bytes=min(need, V7X_VMEM_USABLE_BYTES))


def _load_bf16(ref):
    v = ref[...]
    return v if v.dtype == BF16 else v.astype(BF16)


def _rms(xf, g):
    return xf * lax.rsqrt(jnp.mean(xf * xf, axis=-1, keepdims=True) + EPS) * g


def _norm_matmul_kernel(x_ref, g_ref, b_ref, h_ref, o_ref):
    @pl.when(pl.program_id(1) == 0)
    def _():
        h_ref[...] = _rms(x_ref[...], g_ref[...]).astype(h_ref.dtype)

    o_ref[...] = jnp.dot(h_ref[...], b_ref[...], preferred_element_type=F32).astype(o_ref.dtype)


def _norm_matmul(x, g, b, *, tm, tn, out_dtype, name):
    s, d = x.shape
    n = b.shape[1]
    return pl.pallas_call(
        _norm_matmul_kernel,
        grid=(s // tm, n // tn),
        in_specs=[
            pl.BlockSpec((tm, d), lambda i, j: (i, 0)),
            pl.BlockSpec((1, d), lambda i, j: (0, 0)),
            pl.BlockSpec((d, tn), lambda i, j: (0, j)),
        ],
        out_specs=[pl.BlockSpec((tm, d), lambda i, j: (i, 0)), pl.BlockSpec((tm, tn), lambda i, j: (i, j))],
        out_shape=[jax.ShapeDtypeStruct((s, d), BF16), jax.ShapeDtypeStruct((s, n), out_dtype)],
        compiler_params=_params(("parallel", "arbitrary"),
                                [((tm, d), F32), ((d, tn), b.dtype), ((tm, d), BF16), ((tm, tn), out_dtype)],
                                temps=[((tm, d), F32), ((tm, tn), F32)]),
        name=name,
    )(x, g, b)


def _matmul_kernel(a_ref, b_ref, o_ref):
    o_ref[...] = jnp.dot(a_ref[...], _load_bf16(b_ref), preferred_element_type=F32).astype(o_ref.dtype)


def _matmul(a, b, *, tm, tn, out_dtype, name):
    m, k = a.shape
    _, n = b.shape
    return pl.pallas_call(
        _matmul_kernel,
        grid=(m // tm, n // tn),
        in_specs=[pl.BlockSpec((tm, k), lambda i, j: (i, 0)), pl.BlockSpec((k, tn), lambda i, j: (0, j))],
        out_specs=pl.BlockSpec((tm, tn), lambda i, j: (i, j)),
        out_shape=jax.ShapeDtypeStruct((m, n), out_dtype),
        compiler_params=_params(("parallel", "parallel"),
                                [((tm, k), a.dtype), ((k, tn), b.dtype), ((tm, tn), out_dtype)],
                                temps=[((tm, tn), F32), ((k, tn), BF16)]),
        name=name,
    )(a, b)


def _pool_mixer_kernel(halo_ref, u_ref, w_ref, scale_ref, o_ref, ext_ref, *, tm, gw):
    i = pl.program_id(0)
    ext_ref[0:POOL_HALO, :] = jnp.where(i > 0, halo_ref[...], 0.0)
    ext_ref[POOL_HALO:POOL_HALO + tm, :] = u_ref[...]
    row = lax.broadcasted_iota(jnp.int32, (tm, 1), 0) + i * tm
    for g, w in enumerate(POOL_WINDOWS):
        cols = slice(g * gw, (g + 1) * gw)
        u = u_ref[:, cols]
        win_sum = u
        for back in range(1, w):
            win_sum = win_sum + ext_ref[POOL_HALO - back:POOL_HALO - back + tm, cols]
        cnt = jnp.minimum(row + 1, w).astype(F32)
        pooled = win_sum / cnt - u
        mixed = jnp.dot(pooled.astype(BF16), w_ref[g], preferred_element_type=F32)
        o_ref[:, cols] = (mixed * scale_ref[:, cols]).astype(o_ref.dtype)


def _pool_mixer(z, w_pool, pool_scale, *, pool_width, tm=512):
    s = z.shape[0]
    groups, gw, _ = w_pool.shape
    halo_blocks = tm // POOL_HALO
    return pl.pallas_call(
        functools.partial(_pool_mixer_kernel, tm=tm, gw=gw),
        grid=(s // tm,),
        in_specs=[
            pl.BlockSpec((POOL_HALO, pool_width), lambda i: (jnp.maximum(i * halo_blocks - 1, 0), 0)),
            pl.BlockSpec((tm, pool_width), lambda i: (i, 0)),
            pl.BlockSpec((groups, gw, gw), lambda i: (0, 0, 0)),
            pl.BlockSpec((1, pool_width), lambda i: (0, 0)),
        ],
        out_specs=pl.BlockSpec((tm, pool_width), lambda i: (i, 0)),
        out_shape=jax.ShapeDtypeStruct((s, pool_width), BF16),
        scratch_shapes=[pltpu.VMEM((POOL_HALO + tm, pool_width), F32)],
        compiler_params=_params(("parallel",),
                                [((tm, pool_width), F32), ((groups, gw, gw), BF16), ((tm, pool_width), BF16)],
                                scratch=[((POOL_HALO + tm, pool_width), F32)],
                                temps=[((tm, pool_width), F32)]),
        name="pool_mixer",
    )(z, z, w_pool, pool_scale)


def _q_proj_kernel(ql_ref, g_ref, w_ref, pos_ref, freq_ref, o_ref, qn_ref, cos_ref, sin_ref, *, scale):
    @pl.when(pl.program_id(1) == 0)
    def _():
        qn_ref[...] = (_rms(ql_ref[...], g_ref[...]) * scale).astype(qn_ref.dtype)
        ang = freq_ref[...] * pos_ref[...]
        cos_ref[...] = jnp.cos(ang)
        sin_ref[...] = jnp.sin(ang)

    qt = lax.dot_general(w_ref[...], qn_ref[...], (((1,), (1,)), ((), ())), preferred_element_type=F32)
    half = QK_ROPE_DIM // 2
    r0, r1, r2 = QK_NOPE_DIM, QK_NOPE_DIM + half, QK_NOPE_DIM + QK_ROPE_DIM
    c, sn = cos_ref[...], sin_ref[...]
    rows = o_ref.shape[2]
    for hh in range(o_ref.shape[0]):
        q = qt[hh * rows:(hh + 1) * rows]
        x1, x2 = q[r0:r1], q[r1:r2]
        o_ref[hh, 0, 0:r0, :] = q[0:r0].astype(o_ref.dtype)
        o_ref[hh, 0, r0:r1, :] = (x1 * c - x2 * sn).astype(o_ref.dtype)
        o_ref[hh, 0, r1:r2, :] = (x1 * sn + x2 * c).astype(o_ref.dtype)
        o_ref[hh, 0, r2:, :] = q[r2:].astype(o_ref.dtype)


def _q_proj(z, q_norm, wq_t, pos_row, freq_col, *, col_block, rank, scale, tm, heads_per_step=4):
    s = z.shape[0]
    rows = V7X_MXU_DIM
    n_heads = wq_t.shape[0] // rows
    hps = heads_per_step
    half = QK_ROPE_DIM // 2
    return pl.pallas_call(
        functools.partial(_q_proj_kernel, scale=scale),
        grid=(s // tm, n_heads // hps),
        in_specs=[
            pl.BlockSpec((tm, rank), lambda i, h: (i, col_block)),
            pl.BlockSpec((1, rank), lambda i, h: (0, 0)),
            pl.BlockSpec((hps * rows, rank), lambda i, h: (h, 0)),
            pl.BlockSpec((1, tm), lambda i, h: (0, i)),
            pl.BlockSpec((half, 1), lambda i, h: (0, 0)),
        ],
        out_specs=pl.BlockSpec((hps, 1, rows, tm), lambda i, h: (h, i, 0, 0)),
        out_shape=jax.ShapeDtypeStruct((n_heads, s // tm, rows, tm), BF16),
        scratch_shapes=[pltpu.VMEM((tm, rank), BF16), pltpu.VMEM((half, tm), F32), pltpu.VMEM((half, tm), F32)],
        compiler_params=_params(("parallel", "arbitrary"),
                                [((tm, rank), F32), ((hps * rows, rank), BF16), ((hps * rows, tm), BF16)],
                                scratch=[((tm, rank), BF16), ((2 * half, tm), F32)],
                                temps=[((tm, rank), F32), ((hps * rows, tm), F32)]),
        name="q_proj",
    )(z, q_norm, wq_t, pos_row, freq_col)


def _kv_proj_kernel(kvl_ref, kr_ref, g_ref, wk_ref, wvt_ref, pos_ref, freq_ref, k_ref, vt_ref, kro_ref):
    kvn = _rms(kvl_ref[...], g_ref[...]).astype(BF16)
    k_ref[...] = jnp.dot(kvn, wk_ref[...], preferred_element_type=F32).astype(k_ref.dtype)
    vt = lax.dot_general(wvt_ref[...], kvn, (((1,), (1,)), ((), ())), preferred_element_type=F32)
    vt_ref[:, 0] = vt.reshape(vt_ref.shape[0], V_HEAD_DIM, vt.shape[1]).astype(vt_ref.dtype)
    half = QK_ROPE_DIM // 2
    x = kr_ref[...]
    ang = pos_ref[...] * freq_ref[...]
    lane = lax.broadcasted_iota(jnp.int32, x.shape, 1)
    x2_at_lo = pltpu.roll(x, V7X_LANES - half, axis=1)
    x1_at_hi = pltpu.roll(x, half, axis=1)
    partner = jnp.where(lane < half, -x2_at_lo, jnp.where(lane < QK_ROPE_DIM, x1_at_hi, 0.0))
    kro_ref[...] = (x * jnp.cos(ang) + partner * jnp.sin(ang)).astype(kro_ref.dtype)


def _kv_proj(z, kv_norm, wk, wv_t, pos_col, freq_row, *, lat_block, rope_block, rank, tm):
    s = z.shape[0]
    nk = wk.shape[1]
    nv = wv_t.shape[0]
    n_heads = nv // V_HEAD_DIM
    return pl.pallas_call(
        _kv_proj_kernel,
        grid=(s // tm,),
        in_specs=[
            pl.BlockSpec((tm, rank), lambda i: (i, lat_block)),
            pl.BlockSpec((tm, V7X_LANES), lambda i: (i, rope_block)),
            pl.BlockSpec((1, rank), lambda i: (0, 0)),
            pl.BlockSpec((rank, nk), lambda i: (0, 0)),
            pl.BlockSpec((nv, rank), lambda i: (0, 0)),
            pl.BlockSpec((tm, 1), lambda i: (i, 0)),
            pl.BlockSpec((1, V7X_LANES), lambda i: (0, 0)),
        ],
        out_specs=[
            pl.BlockSpec((tm, nk), lambda i: (i, 0)),
            pl.BlockSpec((n_heads, 1, V_HEAD_DIM, tm), lambda i: (0, i, 0, 0)),
            pl.BlockSpec((tm, V7X_LANES), lambda i: (i, 0)),
        ],
        out_shape=[
            jax.ShapeDtypeStruct((s, nk), BF16),
            jax.ShapeDtypeStruct((n_heads, s // tm, V_HEAD_DIM, tm), BF16),
            jax.ShapeDtypeStruct((s, V7X_LANES), BF16),
        ],
        compiler_params=_params(("parallel",),
                                [((tm, rank), F32), ((tm, V7X_LANES), F32), ((rank, nk), BF16), ((nv, rank), BF16),
                                 ((tm, nk), BF16), ((nv, tm), BF16), ((tm, V7X_LANES), BF16), ((tm, V7X_LANES), F32)],
                                temps=[((tm, nk), F32), ((nv, tm), F32)]),
        name="kv_proj",
    )(z, z, kv_norm, wk, wv_t, pos_col, freq_row)


def _flash_kernel(kn_ref, kr_ref, qt_ref, vt_ref, o_ref, s0_ref, s1_ref, s2_ref, bias_ref, m_ref, l_ref, acc_ref,
                  *, t, unroll):
    nq = qt_ref.shape[1]
    ring = (s0_ref, s1_ref)
    kc = lax.broadcasted_iota(jnp.int32, (t, t), 0) // CHUNK
    qc = lax.broadcasted_iota(jnp.int32, (t, t), 1) // CHUNK
    bias_ref[...] = jnp.where(kc <= qc, 0.0, -jnp.inf).astype(F32)

    def scores(qi, kb, s_ref):
        ks = pl.ds(pl.multiple_of(kb * t, t), t)
        kcat = jnp.concatenate([kn_ref[ks, :], kr_ref[ks, :]], axis=1)
        s_ref[...] = jnp.dot(kcat, qt_ref[0, qi], preferred_element_type=F32)

    def softmax_pv(kb, s_ref, masked):
        s = s_ref[...]
        if masked:
            s = s + bias_ref[...]
        m_prev = m_ref[...]
        m_new = jnp.maximum(m_prev, jnp.max(s, axis=0, keepdims=True))
        alpha = jnp.exp2(m_prev - m_new)
        p = jnp.exp2(s - m_new)
        l_ref[...] = alpha * l_ref[...] + jnp.sum(p, axis=0, keepdims=True)
        pv = jnp.dot(vt_ref[0, kb], p.astype(BF16), preferred_element_type=F32)
        acc_ref[...] = alpha * acc_ref[...] + pv
        m_ref[...] = m_new

    def query_block(qi, carry):
        m_ref[...] = jnp.full(m_ref.shape, -jnp.inf, F32)
        l_ref[...] = jnp.zeros(l_ref.shape, F32)
        acc_ref[...] = jnp.zeros(acc_ref.shape, F32)
        nxt = jnp.minimum(qi + 1, nq - 1)

        @pl.when(qi == 0)
        def _():
            softmax_pv(0, s2_ref, masked=True)
            scores(nxt, 0, s2_ref)

        @pl.when(qi > 0)
        def _():
            scores(qi, 1, s1_ref)
            softmax_pv(0, s2_ref, masked=False)
            n_groups = (qi - 1) // unroll

            def group(g, c):
                b0 = 1 + g * unroll
                for u in range(unroll):
                    scores(qi, b0 + u + 1, ring[u % 2])
                    softmax_pv(b0 + u, ring[(u + 1) % 2], masked=False)
                return c

            lax.fori_loop(0, n_groups, group, 0)
            base = 1 + n_groups * unroll
            rest = qi - base
            for k in range(0, unroll - 2, 2):
                @pl.when(rest >= k + 2)
                def _(k=k):
                    for u in range(2):
                        scores(qi, base + k + u + 1, ring[u % 2])
                        softmax_pv(base + k + u, ring[(u + 1) % 2], masked=False)

            @pl.when(rest % 2 == 1)
            def _():
                scores(qi, base + rest, ring[0])
                softmax_pv(base + rest - 1, ring[1], masked=False)

            for parity in (0, 1):
                @pl.when(rest % 2 == parity)
                def _(parity=parity):
                    scores(nxt, 0, s2_ref)
                    softmax_pv(qi, ring[(1 + parity) % 2], masked=True)

        rows = pl.ds(pl.multiple_of(qi * t, t), t)
        o_ref[rows, :] = (acc_ref[...] / l_ref[...]).T.astype(o_ref.dtype)
        return carry

    scores(0, 0, s2_ref)
    lax.fori_loop(0, nq, query_block, 0)


def _flash_attention(k_nope, k_rope, q_t, v_t, *, n_heads, t, unroll=4):
    s = k_nope.shape[0]
    nq = s // t
    assert t % CHUNK == 0 and unroll % 2 == 0 and nq >= 2
    assert v_t.shape == (n_heads, nq, V_HEAD_DIM, t) and q_t.shape == (n_heads, nq, V7X_MXU_DIM, t)
    return pl.pallas_call(
        functools.partial(_flash_kernel, t=t, unroll=unroll),
        grid=(n_heads,),
        in_specs=[
            pl.BlockSpec((s, QK_NOPE_DIM), lambda h: (0, h)),
            pl.BlockSpec((s, V7X_LANES), lambda h: (0, 0)),
            pl.BlockSpec((1, nq, V7X_MXU_DIM, t), lambda h: (h, 0, 0, 0)),
            pl.BlockSpec((1, nq, V_HEAD_DIM, t), lambda h: (h, 0, 0, 0)),
        ],
        out_specs=pl.BlockSpec((s, V_HEAD_DIM), lambda h: (0, h)),
        out_shape=jax.ShapeDtypeStruct((s, n_heads * V_HEAD_DIM), BF16),
        scratch_shapes=[pltpu.VMEM((t, t), F32)] * 4 + [pltpu.VMEM((1, t), F32), pltpu.VMEM((1, t), F32),
                                                        pltpu.VMEM((V_HEAD_DIM, t), F32)],
        compiler_params=_params(("arbitrary",),
                                [((s, QK_NOPE_DIM), BF16), ((s, V7X_LANES), BF16), ((V7X_MXU_DIM, s), BF16),
                                 ((V_HEAD_DIM, s), BF16), ((s, V_HEAD_DIM), BF16)],
                                scratch=[((4 * t + V_HEAD_DIM + 16, t), F32)],
                                temps=[((t, t), F32), ((t, t), F32), ((t, t), BF16), ((t, V7X_MXU_DIM), BF16)]),
        name="flash_attention",
    )(k_nope, k_rope, q_t, v_t)


def _gated_merge_kernel(pm_ref, at_ref, h_ref, wup_ref, wum_ref, wga_ref, wgb_ref, o_ref):
    h = h_ref[...]
    ya = jnp.dot(pm_ref[...], wup_ref[...], preferred_element_type=F32)
    yb = jnp.dot(at_ref[...], wum_ref[...], preferred_element_type=F32)
    ga = jax.nn.sigmoid(jnp.dot(h, wga_ref[...], preferred_element_type=F32))
    gb = jax.nn.sigmoid(jnp.dot(h, wgb_ref[...], preferred_element_type=F32))
    o_ref[...] = (ga * ya + gb * yb).astype(o_ref.dtype)


def _gated_merge(pm, attn, h, w_up_pool, w_up_mla, w_gate2d, *, tm=1024, tn=256):
    s, d = h.shape
    kp, km = pm.shape[1], attn.shape[1]
    nb = d // tn
    return pl.pallas_call(
        _gated_merge_kernel,
        grid=(s // tm, nb),
        in_specs=[
            pl.BlockSpec((tm, kp), lambda i, j: (i, 0)),
            pl.BlockSpec((tm, km), lambda i, j: (i, 0)),
            pl.BlockSpec((tm, d), lambda i, j: (i, 0)),
            pl.BlockSpec((kp, tn), lambda i, j: (0, j)),
            pl.BlockSpec((km, tn), lambda i, j: (0, j)),
            pl.BlockSpec((d, tn), lambda i, j: (0, j)),
            pl.BlockSpec((d, tn), lambda i, j: (0, j + nb)),
        ],
        out_specs=pl.BlockSpec((tm, tn), lambda i, j: (i, j)),
        out_shape=jax.ShapeDtypeStruct((s, d), BF16),
        compiler_params=_params(("parallel", "parallel"),
                                [((tm, kp), BF16), ((tm, km), BF16), ((tm, d), BF16), ((kp, tn), BF16),
                                 ((km, tn), BF16), ((d, tn), BF16), ((d, tn), BF16), ((tm, tn), BF16)],
                                temps=[((tm, tn), F32)] * 4),
        name="gated_merge",
    )(pm, attn, h, w_up_pool, w_up_mla, w_gate2d, w_gate2d)


def _residual_norm_kernel(t_ref, x_ref, gpost_ref, gnext_ref, xo_ref, ho_ref):
    xo = x_ref[...] + _rms(t_ref[...].astype(F32), gpost_ref[...])
    xo_ref[...] = xo
    ho_ref[...] = _rms(xo, gnext_ref[...]).astype(ho_ref.dtype)


def _residual_final_kernel(t_ref, x_ref, gpost_ref, xo_ref):
    xo_ref[...] = x_ref[...] + _rms(t_ref[...].astype(F32), gpost_ref[...])


def _residual_norm(t, x, g_post, g_next=None, *, tm=256):
    s, d = x.shape
    row = pl.BlockSpec((tm, d), lambda i: (i, 0))
    vec = pl.BlockSpec((1, d), lambda i: (0, 0))
    blocks = [((tm, d), t.dtype), ((tm, d), F32), ((tm, d), F32)]
    if g_next is None:
        return pl.pallas_call(
            _residual_final_kernel, grid=(s // tm,), in_specs=[row, row, vec], out_specs=row,
            out_shape=jax.ShapeDtypeStruct((s, d), F32),
            compiler_params=_params(("parallel",), blocks, temps=[((tm, d), F32)] * 2),
            name="residual_final",
        )(t, x, g_post)
    return pl.pallas_call(
        _residual_norm_kernel, grid=(s // tm,), in_specs=[row, row, vec, vec], out_specs=[row, row],
        out_shape=[jax.ShapeDtypeStruct((s, d), F32), jax.ShapeDtypeStruct((s, d), BF16)],
        compiler_params=_params(("parallel",), blocks + [((tm, d), BF16)], temps=[((tm, d), F32)] * 2),
        name="residual_norm",
    )(t, x, g_post, g_next)


def _swiglu_up_kernel(h_ref, wg_ref, wu_ref, o_ref):
    h = h_ref[...]
    gate = jnp.dot(h, _load_bf16(wg_ref), preferred_element_type=F32)
    up = jnp.dot(h, _load_bf16(wu_ref), preferred_element_type=F32)
    o_ref[...] = (gate * jax.nn.sigmoid(gate) * up).astype(o_ref.dtype)


def _swiglu_up(h, w_gate, w_up, *, tm=1024, tn=256):
    s, d = h.shape
    f = w_gate.shape[1]
    return pl.pallas_call(
        _swiglu_up_kernel,
        grid=(s // tm, f // tn),
        in_specs=[
            pl.BlockSpec((tm, d), lambda i, j: (i, 0)),
            pl.BlockSpec((d, tn), lambda i, j: (0, j)),
            pl.BlockSpec((d, tn), lambda i, j: (0, j)),
        ],
        out_specs=pl.BlockSpec((tm, tn), lambda i, j: (i, j)),
        out_shape=jax.ShapeDtypeStruct((s, f), BF16),
        compiler_params=_params(("parallel", "parallel"),
                                [((tm, d), BF16), ((d, tn), w_gate.dtype), ((d, tn), w_up.dtype), ((tm, tn), BF16)],
                                temps=[((tm, tn), F32)] * 3 + [((d, tn), BF16)] * 2),
        name="swiglu_up",
    )(h, w_gate, w_up)


def _ple_gate_kernel(h_ref, p_ref, wg_ref, wp_ref, o_ref):
    gate = jax.nn.sigmoid(jnp.dot(h_ref[...], _load_bf16(wg_ref), preferred_element_type=F32))
    pe = jnp.dot(_load_bf16(p_ref), _load_bf16(wp_ref), preferred_element_type=F32)
    o_ref[...] = (pe * gate).astype(o_ref.dtype)


def _ple_gate(h, p, w_gate, w_proj, *, tm=1024, tn=512):
    s, d = h.shape
    r = p.shape[1]
    return pl.pallas_call(
        _ple_gate_kernel,
        grid=(s // tm, d // tn),
        in_specs=[
            pl.BlockSpec((tm, d), lambda i, j: (i, 0)),
            pl.BlockSpec((tm, r), lambda i, j: (i, 0)),
            pl.BlockSpec((d, tn), lambda i, j: (0, j)),
            pl.BlockSpec((r, tn), lambda i, j: (0, j)),
        ],
        out_specs=pl.BlockSpec((tm, tn), lambda i, j: (i, j)),
        out_shape=jax.ShapeDtypeStruct((s, d), BF16),
        compiler_params=_params(("parallel", "parallel"),
                                [((tm, d), BF16), ((tm, r), p.dtype), ((d, tn), w_gate.dtype), ((r, tn), w_proj.dtype),
                                 ((tm, tn), BF16)],
                                temps=[((tm, tn), F32)] * 2 + [((d, tn), BF16)]),
        name="ple_gate",
    )(h, p, w_gate, w_proj)


def kernel(x, p, positions, norm_mix_pre, norm_mix_post, w_in, q_norm, kv_norm, w_q_b, w_kv_b, w_pool, pool_scale,
           w_up_pool, w_up_mla, w_branch_gate, w_out, norm_ffn_pre, norm_ffn_post, w_ffn_gate, w_ffn_up,
           w_ffn_down, norm_ple_pre, w_ple_gate, w_ple_proj, norm_ple_post):
    batch, seq, d_model = x.shape
    depth = w_in.shape[0]
    in_width = w_in.shape[2]
    q_rank, n_heads, qk_dim = w_q_b.shape[1:]
    kv_rank = w_kv_b.shape[1]
    pool_width = w_up_pool.shape[1]
    assert batch == 1 and qk_dim == QK_NOPE_DIM + QK_ROPE_DIM
    assert w_kv_b.shape[3] == QK_NOPE_DIM + V_HEAD_DIM
    o_q, o_kv, o_kr = pool_width, pool_width + q_rank, pool_width + q_rank + kv_rank
    assert o_kr + QK_ROPE_DIM == in_width
    assert o_q % q_rank == 0 and o_kv % kv_rank == 0 and o_kr % V7X_LANES == 0
    z_tn = 768
    attn_t = 512
    z_width = -(-(o_kr + V7X_LANES) // z_tn) * z_tn
    half = QK_ROPE_DIM // 2
    scale = float(qk_dim) ** -0.5 * LOG2_E

    inv_freq = ROPE_THETA ** (-jnp.arange(0, QK_ROPE_DIM, 2, dtype=F32) / QK_ROPE_DIM)
    freq_col = inv_freq.reshape(half, 1)
    freq_row = jnp.zeros((1, V7X_LANES), F32).at[0, :half].set(inv_freq).at[0, half:QK_ROPE_DIM].set(inv_freq)
    pos_f = positions.astype(F32)
    pos_row, pos_col = pos_f.reshape(1, seq), pos_f.reshape(seq, 1)

    xs = x.reshape(seq, d_model)
    for i in range(depth):
        w_in_p = jnp.pad(w_in[i].astype(BF16), ((0, 0), (0, z_width - in_width)))
        wq = w_q_b[i].astype(BF16)
        wq_t = jnp.pad(jnp.transpose(wq, (1, 2, 0)), ((0, 0), (0, V7X_MXU_DIM - qk_dim), (0, 0)))
        wq_t = wq_t.reshape(n_heads * V7X_MXU_DIM, q_rank)
        wkv = w_kv_b[i].astype(BF16)
        wk = wkv[:, :, :QK_NOPE_DIM].reshape(kv_rank, n_heads * QK_NOPE_DIM)
        wv_t = jnp.transpose(wkv[:, :, QK_NOPE_DIM:], (1, 2, 0)).reshape(n_heads * V_HEAD_DIM, kv_rank)
        w_gate2d = w_branch_gate[i].astype(BF16).reshape(d_model, 2 * d_model)

        def row(v):
            return v[i].reshape(1, -1)

        h, z = _norm_matmul(xs, row(norm_mix_pre), w_in_p, tm=512, tn=z_tn, out_dtype=F32, name="in_proj")
        pm = _pool_mixer(z, w_pool[i].astype(BF16), row(pool_scale), pool_width=pool_width)
        q_t = _q_proj(z, row(q_norm), wq_t, pos_row, freq_col, col_block=o_q // q_rank, rank=q_rank, scale=scale,
                      tm=attn_t)
        k_nope, v_t, k_rope = _kv_proj(z, row(kv_norm), wk, wv_t, pos_col, freq_row, lat_block=o_kv // kv_rank,
                                       rope_block=o_kr // V7X_LANES, rank=kv_rank, tm=attn_t)
        attn = _flash_attention(k_nope, k_rope, q_t, v_t, n_heads=n_heads, t=attn_t)
        merged = _gated_merge(pm, attn, h, w_up_pool[i].astype(BF16), w_up_mla[i].astype(BF16), w_gate2d)
        mix = _matmul(merged, w_out[i], tm=1024, tn=512, out_dtype=BF16, name="out_proj")
        xs, h2 = _residual_norm(mix, xs, row(norm_mix_post), row(norm_ffn_pre))
        act = _swiglu_up(h2, w_ffn_gate[i], w_ffn_up[i])
        ffn = _matmul(act, w_ffn_down[i].astype(BF16), tm=512, tn=512, out_dtype=BF16, name="ffn_down")
        xs, h3 = _residual_norm(ffn, xs, row(norm_ffn_post), row(norm_ple_pre))
        t = _ple_gate(h3, p[i].reshape(seq, -1), w_ple_gate[i], w_ple_proj[i])
        xs = _residual_norm(t, xs, row(norm_ple_post))
    return xs.reshape(batch, seq, d_model)
```

```python
import functools

import jax
import jax.numpy as jnp
from jax import lax
from jax.experimental import pallas as pl
from jax.experimental.pallas import tpu as pltpu

CHUNK = 64
EPS = 1e-6
POOL_WINDOWS = (2, 4, 8, 16)
QK_NOPE_DIM = 128
QK_ROPE_DIM = 64
V_HEAD_DIM = 128
ROPE_THETA = 10000.0
LOG2_E = 1.4426950408889634

V7X_LANES = 128
V7X_MXU_DIM = 256
V7X_VMEM_BYTES = 64 * 1024 * 1024
V7X_VMEM_USABLE_BYTES = 60000 * 1024
V7X_VMEM_DEFAULT_SCOPED_BYTES = 32 * 1024 * 1024

POOL_HALO = 16
F32 = jnp.float32
BF16 = jnp.bfloat16


def _gcd(a, b):
    while b:
        a, b = b, a % b
    return a


def _nbytes(shape, dtype):
    n = 1
    for s in shape:
        n *= s
    return n * jnp.dtype(dtype).itemsize


def _params(semantics, blocks, scratch=(), temps=()):
    need = 2 * sum(_nbytes(s, d) for s, d in blocks)
    need += sum(_nbytes(s, d) for s, d in scratch) + sum(_nbytes(s, d) for s, d in temps)
    need = max(int(need * 1.25) + (2 << 20), V7X_VMEM_DEFAULT_SCOPED_BYTES)
    return pltpu.CompilerParams(dimension_semantics=semantics,
                                vmem_limit_bytes=min(need, V7X_VMEM_USABLE_BYTES))


def _load_bf16(ref):
    v = ref[...]
    return v if v.dtype == BF16 else v.astype(BF16)


def _rms(xf, g):
    return xf * lax.rsqrt(jnp.mean(xf * xf, axis=-1, keepdims=True) + EPS) * g


def _rmsnorm_cast_kernel(x_ref, g_ref, o_ref):
    o_ref[...] = _rms(x_ref[...], g_ref[...]).astype(o_ref.dtype)


def _rmsnorm_cast(x, g, *, tm=256):
    s, d = x.shape
    return pl.pallas_call(
        _rmsnorm_cast_kernel,
        grid=(s // tm,),
        in_specs=[pl.BlockSpec((tm, d), lambda i: (i, 0)), pl.BlockSpec((1, d), lambda i: (0, 0))],
        out_specs=pl.BlockSpec((tm, d), lambda i: (i, 0)),
        out_shape=jax.ShapeDtypeStruct((s, d), BF16),
        compiler_params=_params(("parallel",), [((tm, d), F32), ((tm, d), BF16)], temps=[((tm, d), F32)]),
        name="rmsnorm_cast",
    )(x, g)


def _cast_pad_rows_kernel(w_ref, o_ref, *, n_in_blocks):
    live = pl.program_id(0) < n_in_blocks
    o_ref[...] = jnp.where(live, w_ref[...], 0.0).astype(o_ref.dtype)


def _cast_pad_rows(w, n_out):
    n, k = w.shape
    rows = _gcd(n, n_out)
    assert rows % 16 == 0
    n_in_blocks = n // rows
    return pl.pallas_call(
        functools.partial(_cast_pad_rows_kernel, n_in_blocks=n_in_blocks),
        grid=(n_out // rows,),
        in_specs=[pl.BlockSpec((rows, k), lambda i: (jnp.minimum(i, n_in_blocks - 1), 0))],
        out_specs=pl.BlockSpec((rows, k), lambda i: (i, 0)),
        out_shape=jax.ShapeDtypeStruct((n_out, k), BF16),
        compiler_params=_params(("parallel",), [((rows, k), F32), ((rows, k), BF16)]),
        name="cast_pad_w_in",
    )(w)


def _cast_pair_kernel(a_ref, b_ref, ao_ref, bo_ref):
    ao_ref[...] = a_ref[...].astype(ao_ref.dtype)
    bo_ref[...] = b_ref[...].astype(bo_ref.dtype)


def _cast_pair(a, b, *, rows=256):
    k, n = a.shape
    assert b.shape == a.shape
    spec = pl.BlockSpec((rows, n), lambda i: (i, 0))
    return pl.pallas_call(
        _cast_pair_kernel,
        grid=(k // rows,),
        in_specs=[spec, spec],
        out_specs=[spec, spec],
        out_shape=[jax.ShapeDtypeStruct((k, n), BF16)] * 2,
        compiler_params=_params(("parallel",), [((rows, n), F32), ((rows, n), BF16)] * 2),
        name="cast_up_weights",
    )(a, b)


def _matmul_kernel(a_ref, b_ref, o_ref, *, b_is_nk):
    contract_b = 1 if b_is_nk else 0
    o_ref[...] = lax.dot_general(a_ref[...], _load_bf16(b_ref), (((1,), (contract_b,)), ((), ())),
                                 preferred_element_type=F32).astype(o_ref.dtype)


def _matmul(a, b, *, tm, tn, out_dtype, name, b_is_nk=False):
    m, k = a.shape
    n = b.shape[0] if b_is_nk else b.shape[1]
    if b_is_nk:
        b_block, b_spec = (tn, k), pl.BlockSpec((tn, k), lambda i, j: (j, 0))
    else:
        b_block, b_spec = (k, tn), pl.BlockSpec((k, tn), lambda i, j: (0, j))
    return pl.pallas_call(
        functools.partial(_matmul_kernel, b_is_nk=b_is_nk),
        grid=(m // tm, n // tn),
        in_specs=[pl.BlockSpec((tm, k), lambda i, j: (i, 0)), b_spec],
        out_specs=pl.BlockSpec((tm, tn), lambda i, j: (i, j)),
        out_shape=jax.ShapeDtypeStruct((m, n), out_dtype),
        compiler_params=_params(("parallel", "parallel"),
                                [((tm, k), a.dtype), (b_block, b.dtype), ((tm, tn), out_dtype)],
                                temps=[((tm, tn), F32), ((k, tn), BF16)]),
        name=name,
    )(a, b)


def _pool_mixer_kernel(halo_ref, u_ref, w_ref, scale_ref, o_ref, ext_ref, *, tm, gw):
    i = pl.program_id(0)
    ext_ref[0:POOL_HALO, :] = jnp.where(i > 0, halo_ref[...], 0.0)
    ext_ref[POOL_HALO:POOL_HALO + tm, :] = u_ref[...]
    row = lax.broadcasted_iota(jnp.int32, (tm, 1), 0) + i * tm
    for g, w in enumerate(POOL_WINDOWS):
        cols = slice(g * gw, (g + 1) * gw)
        u = u_ref[:, cols]
        win_sum = u
        for back in range(1, w):
            win_sum = win_sum + ext_ref[POOL_HALO - back:POOL_HALO - back + tm, cols]
        cnt = jnp.minimum(row + 1, w).astype(F32)
        pooled = win_sum / cnt - u
        mixed = jnp.dot(pooled.astype(BF16), w_ref[g], preferred_element_type=F32)
        o_ref[:, cols] = (mixed * scale_ref[:, cols]).astype(o_ref.dtype)


def _pool_mixer(z, w_pool, pool_scale, *, pool_width, tm=512):
    s = z.shape[0]
    groups, gw, _ = w_pool.shape
    halo_blocks = tm // POOL_HALO
    return pl.pallas_call(
        functools.partial(_pool_mixer_kernel, tm=tm, gw=gw),
        grid=(s // tm,),
        in_specs=[
            pl.BlockSpec((POOL_HALO, pool_width), lambda i: (jnp.maximum(i * halo_blocks - 1, 0), 0)),
            pl.BlockSpec((tm, pool_width), lambda i: (i, 0)),
            pl.BlockSpec((groups, gw, gw), lambda i: (0, 0, 0)),
            pl.BlockSpec((1, pool_width), lambda i: (0, 0)),
        ],
        out_specs=pl.BlockSpec((tm, pool_width), lambda i: (i, 0)),
        out_shape=jax.ShapeDtypeStruct((s, pool_width), BF16),
        scratch_shapes=[pltpu.VMEM((POOL_HALO + tm, pool_width), F32)],
        compiler_params=_params(("parallel",),
                                [((tm, pool_width), F32), ((groups, gw, gw), BF16), ((tm, pool_width), BF16)],
                                scratch=[((POOL_HALO + tm, pool_width), F32)],
                                temps=[((tm, pool_width), F32)]),
        name="pool_mixer",
    )(z, z, w_pool, pool_scale)


def _q_proj_kernel(ql_ref, g_ref, w_ref, pos_ref, freq_ref, o_ref, qn_ref, cos_ref, sin_ref, *, scale):
    @pl.when(pl.program_id(1) == 0)
    def _():
        qn_ref[...] = (_rms(ql_ref[...], g_ref[...]) * scale).astype(qn_ref.dtype)
        ang = freq_ref[...] * pos_ref[...]
        cos_ref[...] = jnp.cos(ang)
        sin_ref[...] = jnp.sin(ang)

    qt = lax.dot_general(w_ref[...], qn_ref[...], (((1,), (1,)), ((), ())), preferred_element_type=F32)
    half = QK_ROPE_DIM // 2
    r0, r1, r2 = QK_NOPE_DIM, QK_NOPE_DIM + half, QK_NOPE_DIM + QK_ROPE_DIM
    c, sn = cos_ref[...], sin_ref[...]
    rows = o_ref.shape[2]
    for hh in range(o_ref.shape[0]):
        q = qt[hh * rows:(hh + 1) * rows]
        x1, x2 = q[r0:r1], q[r1:r2]
        o_ref[hh, 0, 0:r0, :] = q[0:r0].astype(o_ref.dtype)
        o_ref[hh, 0, r0:r1, :] = (x1 * c - x2 * sn).astype(o_ref.dtype)
        o_ref[hh, 0, r1:r2, :] = (x1 * sn + x2 * c).astype(o_ref.dtype)
        o_ref[hh, 0, r2:, :] = q[r2:].astype(o_ref.dtype)


def _q_proj(z, q_norm, wq_t, pos_row, freq_col, *, col_block, rank, scale, tm, heads_per_step=4):
    s = z.shape[0]
    rows = V7X_MXU_DIM
    n_heads = wq_t.shape[0] // rows
    hps = heads_per_step
    half = QK_ROPE_DIM // 2
    return pl.pallas_call(
        functools.partial(_q_proj_kernel, scale=scale),
        grid=(s // tm, n_heads // hps),
        in_specs=[
            pl.BlockSpec((tm, rank), lambda i, h: (i, col_block)),
            pl.BlockSpec((1, rank), lambda i, h: (0, 0)),
            pl.BlockSpec((hps * rows, rank), lambda i, h: (h, 0)),
            pl.BlockSpec((1, tm), lambda i, h: (0, i)),
            pl.BlockSpec((half, 1), lambda i, h: (0, 0)),
        ],
        out_specs=pl.BlockSpec((hps, 1, rows, tm), lambda i, h: (h, i, 0, 0)),
        out_shape=jax.ShapeDtypeStruct((n_heads, s // tm, rows, tm), BF16),
        scratch_shapes=[pltpu.VMEM((tm, rank), BF16), pltpu.VMEM((half, tm), F32), pltpu.VMEM((half, tm), F32)],
        compiler_params=_params(("parallel", "arbitrary"),
                                [((tm, rank), F32), ((hps * rows, rank), BF16), ((hps * rows, tm), BF16)],
                                scratch=[((tm, rank), BF16), ((2 * half, tm), F32)],
                                temps=[((tm, rank), F32), ((hps * rows, tm), F32)]),
        name="q_proj",
    )(z, q_norm, wq_t, pos_row, freq_col)


def _kv_proj_kernel(kvl_ref, kr_ref, g_ref, wk_ref, wvt_ref, pos_ref, freq_ref, k_ref, vt_ref, kro_ref):
    kvn = _rms(kvl_ref[...], g_ref[...]).astype(BF16)
    k_ref[...] = jnp.dot(kvn, wk_ref[...], preferred_element_type=F32).astype(k_ref.dtype)
    vt = lax.dot_general(wvt_ref[...], kvn, (((1,), (1,)), ((), ())), preferred_element_type=F32)
    vt_ref[:, 0] = vt.reshape(vt_ref.shape[0], V_HEAD_DIM, vt.shape[1]).astype(vt_ref.dtype)
    half = QK_ROPE_DIM // 2
    x = kr_ref[...]
    ang = pos_ref[...] * freq_ref[...]
    lane = lax.broadcasted_iota(jnp.int32, x.shape, 1)
    x2_at_lo = pltpu.roll(x, V7X_LANES - half, axis=1)
    x1_at_hi = pltpu.roll(x, half, axis=1)
    partner = jnp.where(lane < half, -x2_at_lo, jnp.where(lane < QK_ROPE_DIM, x1_at_hi, 0.0))
    kro_ref[...] = (x * jnp.cos(ang) + partner * jnp.sin(ang)).astype(kro_ref.dtype)


def _kv_proj(z, kv_norm, wk, wv_t, pos_col, freq_row, *, lat_block, rope_block, rank, tm):
    s = z.shape[0]
    nk = wk.shape[1]
    nv = wv_t.shape[0]
    n_heads = nv // V_HEAD_DIM
    return pl.pallas_call(
        _kv_proj_kernel,
        grid=(s // tm,),
        in_specs=[
            pl.BlockSpec((tm, rank), lambda i: (i, lat_block)),
            pl.BlockSpec((tm, V7X_LANES), lambda i: (i, rope_block)),
            pl.BlockSpec((1, rank), lambda i: (0, 0)),
            pl.BlockSpec((rank, nk), lambda i: (0, 0)),
            pl.BlockSpec((nv, rank), lambda i: (0, 0)),
            pl.BlockSpec((tm, 1), lambda i: (i, 0)),
            pl.BlockSpec((1, V7X_LANES), lambda i: (0, 0)),
        ],
        out_specs=[
            pl.BlockSpec((tm, nk), lambda i: (i, 0)),
            pl.BlockSpec((n_heads, 1, V_HEAD_DIM, tm), lambda i: (0, i, 0, 0)),
            pl.BlockSpec((tm, V7X_LANES), lambda i: (i, 0)),
        ],
        out_shape=[
            jax.ShapeDtypeStruct((s, nk), BF16),
            jax.ShapeDtypeStruct((n_heads, s // tm, V_HEAD_DIM, tm), BF16),
            jax.ShapeDtypeStruct((s, V7X_LANES), BF16),
        ],
        compiler_params=_params(("parallel",),
                                [((tm, rank), F32), ((tm, V7X_LANES), F32), ((rank, nk), BF16), ((nv, rank), BF16),
                                 ((tm, nk), BF16), ((nv, tm), BF16), ((tm, V7X_LANES), BF16), ((tm, V7X_LANES), F32)],
                                temps=[((tm, nk), F32), ((nv, tm), F32)]),
        name="kv_proj",
    )(z, z, kv_norm, wk, wv_t, pos_col, freq_row)


def _flash_kernel(kn_ref, kr_ref, qt_ref, vt_ref, o_ref, s0_ref, s1_ref, s2_ref, bias_ref, m_ref, l_ref, acc_ref,
                  *, t, unroll):
    nq = qt_ref.shape[1]
    ring = (s0_ref, s1_ref)
    kc = lax.broadcasted_iota(jnp.int32, (t, t), 0) // CHUNK
    qc = lax.broadcasted_iota(jnp.int32, (t, t), 1) // CHUNK
    bias_ref[...] = jnp.where(kc <= qc, 0.0, -jnp.inf).astype(F32)

    def scores(qi, kb, s_ref):
        ks = pl.ds(pl.multiple_of(kb * t, t), t)
        kcat = jnp.concatenate([kn_ref[ks, :], kr_ref[ks, :]], axis=1)
        s_ref[...] = jnp.dot(kcat, qt_ref[0, qi], preferred_element_type=F32)

    def softmax_pv(kb, s_ref, masked):
        s = s_ref[...]
        if masked:
            s = s + bias_ref[...]
        m_prev = m_ref[...]
        m_new = jnp.maximum(m_prev, jnp.max(s, axis=0, keepdims=True))
        alpha = jnp.exp2(m_prev - m_new)
        p = jnp.exp2(s - m_new)
        l_ref[...] = alpha * l_ref[...] + jnp.sum(p, axis=0, keepdims=True)
        pv = jnp.dot(vt_ref[0, kb], p.astype(BF16), preferred_element_type=F32)
        acc_ref[...] = alpha * acc_ref[...] + pv
        m_ref[...] = m_new

    def query_block(qi, carry):
        m_ref[...] = jnp.full(m_ref.shape, -jnp.inf, F32)
        l_ref[...] = jnp.zeros(l_ref.shape, F32)
        acc_ref[...] = jnp.zeros(acc_ref.shape, F32)
        nxt = jnp.minimum(qi + 1, nq - 1)

        @pl.when(qi == 0)
        def _():
            softmax_pv(0, s2_ref, masked=True)
            scores(nxt, 0, s2_ref)

        @pl.when(qi > 0)
        def _():
            scores(qi, 1, s1_ref)
            softmax_pv(0, s2_ref, masked=False)
            n_groups = (qi - 1) // unroll

            def group(g, c):
                b0 = 1 + g * unroll
                for u in range(unroll):
                    scores(qi, b0 + u + 1, ring[u % 2])
                    softmax_pv(b0 + u, ring[(u + 1) % 2], masked=False)
                return c

            lax.fori_loop(0, n_groups, group, 0)
            base = 1 + n_groups * unroll
            rest = qi - base
            for k in range(0, unroll - 2, 2):
                @pl.when(rest >= k + 2)
                def _(k=k):
                    for u in range(2):
                        scores(qi, base + k + u + 1, ring[u % 2])
                        softmax_pv(base + k + u, ring[(u + 1) % 2], masked=False)

            @pl.when(rest % 2 == 1)
            def _():
                scores(qi, base + rest, ring[0])
                softmax_pv(base + rest - 1, ring[1], masked=False)

            for parity in (0, 1):
                @pl.when(rest % 2 == parity)
                def _(parity=parity):
                    scores(nxt, 0, s2_ref)
                    softmax_pv(qi, ring[(1 + parity) % 2], masked=True)

        rows = pl.ds(pl.multiple_of(qi * t, t), t)
        o_ref[rows, :] = (acc_ref[...] / l_ref[...]).T.astype(o_ref.dtype)
        return carry

    scores(0, 0, s2_ref)
    lax.fori_loop(0, nq, query_block, 0)


def _flash_attention(k_nope, k_rope, q_t, v_t, *, n_heads, t, unroll=4):
    s = k_nope.shape[0]
    nq = s // t
    assert t % CHUNK == 0 and unroll % 2 == 0 and nq >= 2
    assert v_t.shape == (n_heads, nq, V_HEAD_DIM, t) and q_t.shape == (n_heads, nq, V7X_MXU_DIM, t)
    return pl.pallas_call(
        functools.partial(_flash_kernel, t=t, unroll=unroll),
        grid=(n_heads,),
        in_specs=[
            pl.BlockSpec((s, QK_NOPE_DIM), lambda h: (0, h)),
            pl.BlockSpec((s, V7X_LANES), lambda h: (0, 0)),
            pl.BlockSpec((1, nq, V7X_MXU_DIM, t), lambda h: (h, 0, 0, 0)),
            pl.BlockSpec((1, nq, V_HEAD_DIM, t), lambda h: (h, 0, 0, 0)),
        ],
        out_specs=pl.BlockSpec((s, V_HEAD_DIM), lambda h: (0, h)),
        out_shape=jax.ShapeDtypeStruct((s, n_heads * V_HEAD_DIM), BF16),
        scratch_shapes=[pltpu.VMEM((t, t), F32)] * 4 + [pltpu.VMEM((1, t), F32), pltpu.VMEM((1, t), F32),
                                                        pltpu.VMEM((V_HEAD_DIM, t), F32)],
        compiler_params=_params(("arbitrary",),
                                [((s, QK_NOPE_DIM), BF16), ((s, V7X_LANES), BF16), ((V7X_MXU_DIM, s), BF16),
                                 ((V_HEAD_DIM, s), BF16), ((s, V_HEAD_DIM), BF16)],
                                scratch=[((4 * t + V_HEAD_DIM + 16, t), F32)],
                                temps=[((t, t), F32), ((t, t), F32), ((t, t), BF16), ((t, V7X_MXU_DIM), BF16)]),
        name="flash_attention",
    )(k_nope, k_rope, q_t, v_t)


def _gated_merge_kernel(pm_ref, at_ref, h_ref, wup_ref, wum_ref, wga_ref, wgb_ref, o_ref):
    h = h_ref[...]
    ya = jnp.dot(pm_ref[...], wup_ref[...], preferred_element_type=F32)
    yb = jnp.dot(at_ref[...], wum_ref[...], preferred_element_type=F32)
    ga = jax.nn.sigmoid(jnp.dot(h, wga_ref[...], preferred_element_type=F32))
    gb = jax.nn.sigmoid(jnp.dot(h, wgb_ref[...], preferred_element_type=F32))
    o_ref[...] = (ga * ya + gb * yb).astype(o_ref.dtype)


def _gated_merge(pm, attn, h, w_up_pool, w_up_mla, w_gate2d, *, tm=1024, tn=256):
    s, d = h.shape
    kp, km = pm.shape[1], attn.shape[1]
    nb = d // tn
    return pl.pallas_call(
        _gated_merge_kernel,
        grid=(s // tm, nb),
        in_specs=[
            pl.BlockSpec((tm, kp), lambda i, j: (i, 0)),
            pl.BlockSpec((tm, km), lambda i, j: (i, 0)),
            pl.BlockSpec((tm, d), lambda i, j: (i, 0)),
            pl.BlockSpec((kp, tn), lambda i, j: (0, j)),
            pl.BlockSpec((km, tn), lambda i, j: (0, j)),
            pl.BlockSpec((d, tn), lambda i, j: (0, j)),
            pl.BlockSpec((d, tn), lambda i, j: (0, j + nb)),
        ],
        out_specs=pl.BlockSpec((tm, tn), lambda i, j: (i, j)),
        out_shape=jax.ShapeDtypeStruct((s, d), BF16),
        compiler_params=_params(("parallel", "parallel"),
                                [((tm, kp), BF16), ((tm, km), BF16), ((tm, d), BF16), ((kp, tn), BF16),
                                 ((km, tn), BF16), ((d, tn), BF16), ((d, tn), BF16), ((tm, tn), BF16)],
                                temps=[((tm, tn), F32)] * 4),
        name="gated_merge",
    )(pm, attn, h, w_up_pool, w_up_mla, w_gate2d, w_gate2d)


def _residual_norm_kernel(t_ref, x_ref, gpost_ref, gnext_ref, xo_ref, ho_ref):
    xo = x_ref[...] + _rms(t_ref[...].astype(F32), gpost_ref[...])
    xo_ref[...] = xo
    ho_ref[...] = _rms(xo, gnext_ref[...]).astype(ho_ref.dtype)


def _residual_final_kernel(t_ref, x_ref, gpost_ref, xo_ref):
    xo_ref[...] = x_ref[...] + _rms(t_ref[...].astype(F32), gpost_ref[...])


def _residual_norm(t, x, g_post, g_next=None, *, tm=256):
    s, d = x.shape
    row = pl.BlockSpec((tm, d), lambda i: (i, 0))
    vec = pl.BlockSpec((1, d), lambda i: (0, 0))
    blocks = [((tm, d), t.dtype), ((tm, d), F32), ((tm, d), F32)]
    if g_next is None:
        return pl.pallas_call(
            _residual_final_kernel, grid=(s // tm,), in_specs=[row, row, vec], out_specs=row,
            out_shape=jax.ShapeDtypeStruct((s, d), F32),
            compiler_params=_params(("parallel",), blocks, temps=[((tm, d), F32)] * 2),
            name="residual_final",
        )(t, x, g_post)
    return pl.pallas_call(
        _residual_norm_kernel, grid=(s // tm,), in_specs=[row, row, vec, vec], out_specs=[row, row],
        out_shape=[jax.ShapeDtypeStruct((s, d), F32), jax.ShapeDtypeStruct((s, d), BF16)],
        compiler_params=_params(("parallel",), blocks + [((tm, d), BF16)], temps=[((tm, d), F32)] * 2),
        name="residual_norm",
    )(t, x, g_post, g_next)


def _swiglu_up_kernel(h_ref, wg_ref, wu_ref, wd_ref, o_ref, wd_o_ref):
    h = h_ref[...]
    gate = jnp.dot(h, _load_bf16(wg_ref), preferred_element_type=F32)
    up = jnp.dot(h, _load_bf16(wu_ref), preferred_element_type=F32)
    o_ref[...] = (gate * jax.nn.sigmoid(gate) * up).astype(o_ref.dtype)
    wd_o_ref[...] = wd_ref[...].astype(wd_o_ref.dtype)


def _swiglu_up(h, w_gate, w_up, w_down, *, tm=1024, tn=256):
    s, d = h.shape
    f = w_gate.shape[1]
    nj = f // tn
    steps = (s // tm) * nj
    slab = w_down.shape[0] // steps
    assert slab * steps == w_down.shape[0] and slab % 16 == 0
    dn = w_down.shape[1]
    return pl.pallas_call(
        _swiglu_up_kernel,
        grid=(s // tm, nj),
        in_specs=[
            pl.BlockSpec((tm, d), lambda i, j: (i, 0)),
            pl.BlockSpec((d, tn), lambda i, j: (0, j)),
            pl.BlockSpec((d, tn), lambda i, j: (0, j)),
            pl.BlockSpec((slab, dn), lambda i, j: (i * nj + j, 0)),
        ],
        out_specs=[pl.BlockSpec((tm, tn), lambda i, j: (i, j)),
                   pl.BlockSpec((slab, dn), lambda i, j: (i * nj + j, 0))],
        out_shape=[jax.ShapeDtypeStruct((s, f), BF16), jax.ShapeDtypeStruct(w_down.shape, BF16)],
        compiler_params=_params(("parallel", "parallel"),
                                [((tm, d), BF16), ((d, tn), w_gate.dtype), ((d, tn), w_up.dtype), ((tm, tn), BF16),
                                 ((slab, dn), w_down.dtype), ((slab, dn), BF16)],
                                temps=[((tm, tn), F32)] * 3 + [((d, tn), BF16)] * 2),
        name="swiglu_up",
    )(h, w_gate, w_up, w_down)


def _ple_gate_kernel(h_ref, p_ref, wg_ref, wp_ref, o_ref):
    gate = jax.nn.sigmoid(jnp.dot(h_ref[...], _load_bf16(wg_ref), preferred_element_type=F32))
    pe = jnp.dot(_load_bf16(p_ref), _load_bf16(wp_ref), preferred_element_type=F32)
    o_ref[...] = (pe * gate).astype(o_ref.dtype)


def _ple_gate(h, p, w_gate, w_proj, *, tm=1024, tn=512):
    s, d = h.shape
    r = p.shape[1]
    return pl.pallas_call(
        _ple_gate_kernel,
        grid=(s // tm, d // tn),
        in_specs=[
            pl.BlockSpec((tm, d), lambda i, j: (i, 0)),
            pl.BlockSpec((tm, r), lambda i, j: (i, 0)),
            pl.BlockSpec((d, tn), lambda i, j: (0, j)),
            pl.BlockSpec((r, tn), lambda i, j: (0, j)),
        ],
        out_specs=pl.BlockSpec((tm, tn), lambda i, j: (i, j)),
        out_shape=jax.ShapeDtypeStruct((s, d), BF16),
        compiler_params=_params(("parallel", "parallel"),
                                [((tm, d), BF16), ((tm, r), p.dtype), ((d, tn), w_gate.dtype), ((r, tn), w_proj.dtype),
                                 ((tm, tn), BF16)],
                                temps=[((tm, tn), F32)] * 2 + [((d, tn), BF16)]),
        name="ple_gate",
    )(h, p, w_gate, w_proj)


def kernel(x, p, positions, norm_mix_pre, norm_mix_post, w_in, q_norm, kv_norm, w_q_b, w_kv_b, w_pool, pool_scale,
           w_up_pool, w_up_mla, w_branch_gate, w_out, norm_ffn_pre, norm_ffn_post, w_ffn_gate, w_ffn_up,
           w_ffn_down, norm_ple_pre, w_ple_gate, w_ple_proj, norm_ple_post):
    batch, seq, d_model = x.shape
    depth = w_in.shape[0]
    in_width = w_in.shape[2]
    q_rank, n_heads, qk_dim = w_q_b.shape[1:]
    kv_rank = w_kv_b.shape[1]
    pool_width = w_up_pool.shape[1]
    assert batch == 1 and qk_dim == QK_NOPE_DIM + QK_ROPE_DIM
    assert w_kv_b.shape[3] == QK_NOPE_DIM + V_HEAD_DIM
    o_q, o_kv, o_kr = pool_width, pool_width + q_rank, pool_width + q_rank + kv_rank
    assert o_kr + QK_ROPE_DIM == in_width
    assert o_q % q_rank == 0 and o_kv % kv_rank == 0 and o_kr % V7X_LANES == 0
    z_tn = 768
    attn_t = 512
    z_width = -(-(o_kr + V7X_LANES) // z_tn) * z_tn
    half = QK_ROPE_DIM // 2
    scale = float(qk_dim) ** -0.5 * LOG2_E

    inv_freq = ROPE_THETA ** (-jnp.arange(0, QK_ROPE_DIM, 2, dtype=F32) / QK_ROPE_DIM)
    freq_col = inv_freq.reshape(half, 1)
    freq_row = jnp.zeros((1, V7X_LANES), F32).at[0, :half].set(inv_freq).at[0, half:QK_ROPE_DIM].set(inv_freq)
    pos_f = positions.astype(F32)
    pos_row, pos_col = pos_f.reshape(1, seq), pos_f.reshape(seq, 1)

    xs = x.reshape(seq, d_model)
    for i in range(depth):
        w_in_t = _cast_pad_rows(jnp.transpose(w_in[i]), z_width)
        wq = w_q_b[i].astype(BF16)
        wq_t = jnp.pad(jnp.transpose(wq, (1, 2, 0)), ((0, 0), (0, V7X_MXU_DIM - qk_dim), (0, 0)))
        wq_t = wq_t.reshape(n_heads * V7X_MXU_DIM, q_rank)
        wkv = w_kv_b[i].astype(BF16)
        wk = wkv[:, :, :QK_NOPE_DIM].reshape(kv_rank, n_heads * QK_NOPE_DIM)
        wv_t = jnp.transpose(wkv[:, :, QK_NOPE_DIM:], (1, 2, 0)).reshape(n_heads * V_HEAD_DIM, kv_rank)
        w_gate2d = w_branch_gate[i].astype(BF16).reshape(d_model, 2 * d_model)
        w_up_p, w_up_m = _cast_pair(w_up_pool[i], w_up_mla[i])

        def row(v):
            return v[i].reshape(1, -1)

        h = _rmsnorm_cast(xs, row(norm_mix_pre))
        z = _matmul(h, w_in_t, tm=1024, tn=z_tn, out_dtype=F32, name="in_proj", b_is_nk=True)
        pm = _pool_mixer(z, w_pool[i].astype(BF16), row(pool_scale), pool_width=pool_width)
        q_t = _q_proj(z, row(q_norm), wq_t, pos_row, freq_col, col_block=o_q // q_rank, rank=q_rank, scale=scale,
                      tm=attn_t)
        k_nope, v_t, k_rope = _kv_proj(z, row(kv_norm), wk, wv_t, pos_col, freq_row, lat_block=o_kv // kv_rank,
                                       rope_block=o_kr // V7X_LANES, rank=kv_rank, tm=attn_t)
        attn = _flash_attention(k_nope, k_rope, q_t, v_t, n_heads=n_heads, t=attn_t)
        merged = _gated_merge(pm, attn, h, w_up_p, w_up_m, w_gate2d)
        mix = _matmul(merged, w_out[i], tm=1024, tn=512, out_dtype=BF16, name="out_proj")
        xs, h2 = _residual_norm(mix, xs, row(norm_mix_post), row(norm_ffn_pre))
        act, w_down = _swiglu_up(h2, w_ffn_gate[i], w_ffn_up[i], w_ffn_down[i])
        ffn = _matmul(act, w_down, tm=512, tn=512, out_dtype=BF16, name="ffn_down")
        xs, h3 = _residual_norm(ffn, xs, row(norm_ffn_post), row(norm_ple_pre))
        t = _ple_gate(h3, p[i].reshape(seq, -1), w_ple_gate[i], w_ple_proj[i])
        xs = _residual_norm(t, xs, row(norm_ple_post))
    return xs.reshape(batch, seq, d_model)
```

```python
import functools

import jax
import jax.numpy as jnp
from jax import lax
from jax.experimental import pallas as pl
from jax.experimental.pallas import tpu as pltpu

CHUNK = 64
EPS = 1e-6
POOL_WINDOWS = (2, 4, 8, 16)
QK_NOPE_DIM = 128
QK_ROPE_DIM = 64
V_HEAD_DIM = 128
ROPE_THETA = 10000.0
LOG2_E = 1.4426950408889634

V7X_LANES = 128
V7X_MXU_DIM = 256
V7X_VMEM_BYTES = 64 * 1024 * 1024
V7X_VMEM_USABLE_BYTES = 60000 * 1024
V7X_VMEM_DEFAULT_SCOPED_BYTES = 32 * 1024 * 1024

POOL_HALO = 16
F32 = jnp.float32
BF16 = jnp.bfloat16


def _nbytes(shape, dtype):
    n = 1
    for s in shape:
        n *= s
    return n * jnp.dtype(dtype).itemsize


def _params(semantics, blocks, scratch=(), temps=()):
    need = 2 * sum(_nbytes(s, d) for s, d in blocks)
    need += sum(_nbytes(s, d) for s, d in scratch) + sum(_nbytes(s, d) for s, d in temps)
    need = max(int(need * 1.25) + (2 << 20), V7X_VMEM_DEFAULT_SCOPED_BYTES)
    return pltpu.CompilerParams(dimension_semantics=semantics,
                                vmem_limit_bytes=min(need, V7X_VMEM_USABLE_BYTES))


def _load_bf16(ref):
    v = ref[...]
    return v if v.dtype == BF16 else v.astype(BF16)


def _rms(xf, g):
    return xf * lax.rsqrt(jnp.mean(xf * xf, axis=-1, keepdims=True) + EPS) * g


def _rmsnorm_cast_kernel(x_ref, g_ref, o_ref):
    o_ref[...] = _rms(x_ref[...], g_ref[...]).astype(o_ref.dtype)


def _rmsnorm_cast(x, g, *, tm=256):
    s, d = x.shape
    return pl.pallas_call(
        _rmsnorm_cast_kernel,
        grid=(s // tm,),
        in_specs=[pl.BlockSpec((tm, d), lambda i: (i, 0)), pl.BlockSpec((1, d), lambda i: (0, 0))],
        out_specs=pl.BlockSpec((tm, d), lambda i: (i, 0)),
        out_shape=jax.ShapeDtypeStruct((s, d), BF16),
        compiler_params=_params(("parallel",), [((tm, d), F32), ((tm, d), BF16)], temps=[((tm, d), F32)]),
        name="rmsnorm_cast",
    )(x, g)


def _cast_pair_kernel(a_ref, b_ref, ao_ref, bo_ref):
    ao_ref[...] = a_ref[...].astype(ao_ref.dtype)
    bo_ref[...] = b_ref[...].astype(bo_ref.dtype)


def _cast_pair(a, b, *, rows=256):
    k, n = a.shape
    assert b.shape == a.shape
    spec = pl.BlockSpec((rows, n), lambda i: (i, 0))
    return pl.pallas_call(
        _cast_pair_kernel,
        grid=(k // rows,),
        in_specs=[spec, spec],
        out_specs=[spec, spec],
        out_shape=[jax.ShapeDtypeStruct((k, n), BF16)] * 2,
        compiler_params=_params(("parallel",), [((rows, n), F32), ((rows, n), BF16)] * 2),
        name="cast_up_weights",
    )(a, b)


def _slab_spec(w, steps, nj):
    slab = w.shape[0] // steps
    assert slab * steps == w.shape[0] and slab % 16 == 0
    return (slab, w.shape[1]), pl.BlockSpec((slab, w.shape[1]), lambda i, j: (i * nj + j, 0))


def _matmul_kernel(a_ref, b_ref, *rest, b_is_nk):
    o_ref = rest[-2] if len(rest) == 3 else rest[0]
    contract_b = 1 if b_is_nk else 0
    o_ref[...] = lax.dot_general(a_ref[...], _load_bf16(b_ref), (((1,), (contract_b,)), ((), ())),
                                 preferred_element_type=F32).astype(o_ref.dtype)
    if len(rest) == 3:
        rest[2][...] = rest[0][...].astype(rest[2].dtype)


def _matmul(a, b, *, tm, tn, out_dtype, name, b_is_nk=False, n=None, side_cast=None):
    m, k = a.shape
    if n is None:
        n = b.shape[0] if b_is_nk else b.shape[1]
    if b_is_nk:
        b_block, b_spec = (tn, k), pl.BlockSpec((tn, k), lambda i, j: (j, 0))
    else:
        b_block, b_spec = (k, tn), pl.BlockSpec((k, tn), lambda i, j: (0, j))
    grid = (m // tm, n // tn)
    in_specs = [pl.BlockSpec((tm, k), lambda i, j: (i, 0)), b_spec]
    out_specs = pl.BlockSpec((tm, tn), lambda i, j: (i, j))
    out_shape = jax.ShapeDtypeStruct((m, n), out_dtype)
    blocks = [((tm, k), a.dtype), (b_block, b.dtype), ((tm, tn), out_dtype)]
    args = (a, b)
    if side_cast is not None:
        slab_block, slab_spec = _slab_spec(side_cast, grid[0] * grid[1], grid[1])
        in_specs.append(slab_spec)
        out_specs = [out_specs, slab_spec]
        out_shape = [out_shape, jax.ShapeDtypeStruct(side_cast.shape, BF16)]
        blocks += [(slab_block, F32), (slab_block, BF16)]
        args = (a, b, side_cast)
    return pl.pallas_call(
        functools.partial(_matmul_kernel, b_is_nk=b_is_nk),
        grid=grid, in_specs=in_specs, out_specs=out_specs, out_shape=out_shape,
        compiler_params=_params(("parallel", "parallel"), blocks, temps=[((tm, tn), F32), ((k, tn), BF16)]),
        name=name,
    )(*args)


def _pool_mixer_kernel(halo_ref, u_ref, w_ref, scale_ref, o_ref, ext_ref, *, tm, gw):
    i = pl.program_id(0)
    ext_ref[0:POOL_HALO, :] = jnp.where(i > 0, halo_ref[...], 0.0)
    ext_ref[POOL_HALO:POOL_HALO + tm, :] = u_ref[...]
    row = lax.broadcasted_iota(jnp.int32, (tm, 1), 0) + i * tm
    for g, w in enumerate(POOL_WINDOWS):
        cols = slice(g * gw, (g + 1) * gw)
        u = u_ref[:, cols]
        win_sum = u
        for back in range(1, w):
            win_sum = win_sum + ext_ref[POOL_HALO - back:POOL_HALO - back + tm, cols]
        cnt = jnp.minimum(row + 1, w).astype(F32)
        pooled = win_sum / cnt - u
        mixed = jnp.dot(pooled.astype(BF16), w_ref[g], preferred_element_type=F32)
        o_ref[:, cols] = (mixed * scale_ref[:, cols]).astype(o_ref.dtype)


def _pool_mixer(z, w_pool, pool_scale, *, pool_width, tm=512):
    s = z.shape[0]
    groups, gw, _ = w_pool.shape
    halo_blocks = tm // POOL_HALO
    return pl.pallas_call(
        functools.partial(_pool_mixer_kernel, tm=tm, gw=gw),
        grid=(s // tm,),
        in_specs=[
            pl.BlockSpec((POOL_HALO, pool_width), lambda i: (jnp.maximum(i * halo_blocks - 1, 0), 0)),
            pl.BlockSpec((tm, pool_width), lambda i: (i, 0)),
            pl.BlockSpec((groups, gw, gw), lambda i: (0, 0, 0)),
            pl.BlockSpec((1, pool_width), lambda i: (0, 0)),
        ],
        out_specs=pl.BlockSpec((tm, pool_width), lambda i: (i, 0)),
        out_shape=jax.ShapeDtypeStruct((s, pool_width), BF16),
        scratch_shapes=[pltpu.VMEM((POOL_HALO + tm, pool_width), F32)],
        compiler_params=_params(("parallel",),
                                [((tm, pool_width), F32), ((groups, gw, gw), BF16), ((tm, pool_width), BF16)],
                                scratch=[((POOL_HALO + tm, pool_width), F32)],
                                temps=[((tm, pool_width), F32)]),
        name="pool_mixer",
    )(z, z, w_pool, pool_scale)


def _q_proj_kernel(ql_ref, g_ref, w_ref, pos_ref, freq_ref, o_ref, qn_ref, cos_ref, sin_ref, *, scale):
    @pl.when(pl.program_id(1) == 0)
    def _():
        qn_ref[...] = (_rms(ql_ref[...], g_ref[...]) * scale).astype(qn_ref.dtype)
        ang = freq_ref[...] * pos_ref[...]
        cos_ref[...] = jnp.cos(ang)
        sin_ref[...] = jnp.sin(ang)

    qt = lax.dot_general(w_ref[...], qn_ref[...], (((1,), (1,)), ((), ())), preferred_element_type=F32)
    half = QK_ROPE_DIM // 2
    r0, r1, r2 = QK_NOPE_DIM, QK_NOPE_DIM + half, QK_NOPE_DIM + QK_ROPE_DIM
    c, sn = cos_ref[...], sin_ref[...]
    rows = o_ref.shape[2]
    for hh in range(o_ref.shape[0]):
        q = qt[hh * rows:(hh + 1) * rows]
        x1, x2 = q[r0:r1], q[r1:r2]
        o_ref[hh, 0, 0:r0, :] = q[0:r0].astype(o_ref.dtype)
        o_ref[hh, 0, r0:r1, :] = (x1 * c - x2 * sn).astype(o_ref.dtype)
        o_ref[hh, 0, r1:r2, :] = (x1 * sn + x2 * c).astype(o_ref.dtype)
        o_ref[hh, 0, r2:, :] = q[r2:].astype(o_ref.dtype)


def _q_proj(z, q_norm, wq_t, pos_row, freq_col, *, col_block, rank, scale, tm, heads_per_step=4):
    s = z.shape[0]
    rows = V7X_MXU_DIM
    n_heads = wq_t.shape[0] // rows
    hps = heads_per_step
    half = QK_ROPE_DIM // 2
    return pl.pallas_call(
        functools.partial(_q_proj_kernel, scale=scale),
        grid=(s // tm, n_heads // hps),
        in_specs=[
            pl.BlockSpec((tm, rank), lambda i, h: (i, col_block)),
            pl.BlockSpec((1, rank), lambda i, h: (0, 0)),
            pl.BlockSpec((hps * rows, rank), lambda i, h: (h, 0)),
            pl.BlockSpec((1, tm), lambda i, h: (0, i)),
            pl.BlockSpec((half, 1), lambda i, h: (0, 0)),
        ],
        out_specs=pl.BlockSpec((hps, 1, rows, tm), lambda i, h: (h, i, 0, 0)),
        out_shape=jax.ShapeDtypeStruct((n_heads, s // tm, rows, tm), BF16),
        scratch_shapes=[pltpu.VMEM((tm, rank), BF16), pltpu.VMEM((half, tm), F32), pltpu.VMEM((half, tm), F32)],
        compiler_params=_params(("parallel", "arbitrary"),
                                [((tm, rank), F32), ((hps * rows, rank), BF16), ((hps * rows, tm), BF16)],
                                scratch=[((tm, rank), BF16), ((2 * half, tm), F32)],
                                temps=[((tm, rank), F32), ((hps * rows, tm), F32)]),
        name="q_proj",
    )(z, q_norm, wq_t, pos_row, freq_col)


def _kv_proj_kernel(kvl_ref, h_ref, wkr_ref, g_ref, wk_ref, wvt_ref, pos_ref, freq_ref, k_ref, vt_ref, kro_ref):
    kvn = _rms(kvl_ref[...], g_ref[...]).astype(BF16)
    k_ref[...] = jnp.dot(kvn, wk_ref[...], preferred_element_type=F32).astype(k_ref.dtype)
    vt = lax.dot_general(wvt_ref[...], kvn, (((1,), (1,)), ((), ())), preferred_element_type=F32)
    vt_ref[:, 0] = vt.reshape(vt_ref.shape[0], V_HEAD_DIM, vt.shape[1]).astype(vt_ref.dtype)
    half = QK_ROPE_DIM // 2
    wkr = _load_bf16(wkr_ref)
    wkr = jnp.concatenate([wkr, jnp.zeros((V7X_LANES - wkr.shape[0], wkr.shape[1]), BF16)], axis=0)
    x = lax.dot_general(h_ref[...], wkr, (((1,), (1,)), ((), ())), preferred_element_type=F32)
    ang = pos_ref[...] * freq_ref[...]
    lane = lax.broadcasted_iota(jnp.int32, x.shape, 1)
    x2_at_lo = pltpu.roll(x, V7X_LANES - half, axis=1)
    x1_at_hi = pltpu.roll(x, half, axis=1)
    partner = jnp.where(lane < half, -x2_at_lo, jnp.where(lane < QK_ROPE_DIM, x1_at_hi, 0.0))
    kro_ref[...] = (x * jnp.cos(ang) + partner * jnp.sin(ang)).astype(kro_ref.dtype)


def _kv_proj(z, h, w_in_t, kv_norm, wk, wv_t, pos_col, freq_row, *, lat_block, rope_row_block, rank, tm):
    s, d = h.shape
    nk = wk.shape[1]
    nv = wv_t.shape[0]
    n_heads = nv // V_HEAD_DIM
    return pl.pallas_call(
        _kv_proj_kernel,
        grid=(s // tm,),
        in_specs=[
            pl.BlockSpec((tm, rank), lambda i: (i, lat_block)),
            pl.BlockSpec((tm, d), lambda i: (i, 0)),
            pl.BlockSpec((QK_ROPE_DIM, d), lambda i: (rope_row_block, 0)),
            pl.BlockSpec((1, rank), lambda i: (0, 0)),
            pl.BlockSpec((rank, nk), lambda i: (0, 0)),
            pl.BlockSpec((nv, rank), lambda i: (0, 0)),
            pl.BlockSpec((tm, 1), lambda i: (i, 0)),
            pl.BlockSpec((1, V7X_LANES), lambda i: (0, 0)),
        ],
        out_specs=[
            pl.BlockSpec((tm, nk), lambda i: (i, 0)),
            pl.BlockSpec((n_heads, 1, V_HEAD_DIM, tm), lambda i: (0, i, 0, 0)),
            pl.BlockSpec((tm, V7X_LANES), lambda i: (i, 0)),
        ],
        out_shape=[
            jax.ShapeDtypeStruct((s, nk), BF16),
            jax.ShapeDtypeStruct((n_heads, s // tm, V_HEAD_DIM, tm), BF16),
            jax.ShapeDtypeStruct((s, V7X_LANES), BF16),
        ],
        compiler_params=_params(("parallel",),
                                [((tm, rank), F32), ((tm, d), BF16), ((QK_ROPE_DIM, d), w_in_t.dtype), ((rank, nk), BF16),
                                 ((nv, rank), BF16), ((tm, nk), BF16), ((nv, tm), BF16), ((tm, V7X_LANES), BF16),
                                 ((tm, V7X_LANES), F32)],
                                temps=[((tm, nk), F32), ((nv, tm), F32), ((V7X_LANES, d), BF16)]),
        name="kv_proj",
    )(z, h, w_in_t, kv_norm, wk, wv_t, pos_col, freq_row)


def _flash_kernel(kn_ref, kr_ref, qt_ref, vt_ref, o_ref, s0_ref, s1_ref, s2_ref, bias_ref, m_ref, l_ref, acc_ref,
                  *, t, unroll):
    nq = qt_ref.shape[1]
    ring = (s0_ref, s1_ref)
    kc = lax.broadcasted_iota(jnp.int32, (t, t), 0) // CHUNK
    qc = lax.broadcasted_iota(jnp.int32, (t, t), 1) // CHUNK
    bias_ref[...] = jnp.where(kc <= qc, 0.0, -jnp.inf).astype(F32)

    def scores(qi, kb, s_ref):
        ks = pl.ds(pl.multiple_of(kb * t, t), t)
        kcat = jnp.concatenate([kn_ref[ks, :], kr_ref[ks, :]], axis=1)
        s_ref[...] = jnp.dot(kcat, qt_ref[0, qi], preferred_element_type=F32)

    def softmax_pv(kb, s_ref, masked):
        s = s_ref[...]
        if masked:
            s = s + bias_ref[...]
        m_prev = m_ref[...]
        m_new = jnp.maximum(m_prev, jnp.max(s, axis=0, keepdims=True))
        alpha = jnp.exp2(m_prev - m_new)
        p = jnp.exp2(s - m_new)
        l_ref[...] = alpha * l_ref[...] + jnp.sum(p, axis=0, keepdims=True)
        pv = jnp.dot(vt_ref[0, kb], p.astype(BF16), preferred_element_type=F32)
        acc_ref[...] = alpha * acc_ref[...] + pv
        m_ref[...] = m_new

    def query_block(qi, carry):
        m_ref[...] = jnp.full(m_ref.shape, -jnp.inf, F32)
        l_ref[...] = jnp.zeros(l_ref.shape, F32)
        acc_ref[...] = jnp.zeros(acc_ref.shape, F32)
        nxt = jnp.minimum(qi + 1, nq - 1)

        @pl.when(qi == 0)
        def _():
            softmax_pv(0, s2_ref, masked=True)
            scores(nxt, 0, s2_ref)

        @pl.when(qi > 0)
        def _():
            scores(qi, 1, s1_ref)
            softmax_pv(0, s2_ref, masked=False)
            n_groups = (qi - 1) // unroll

            def group(g, c):
                b0 = 1 + g * unroll
                for u in range(unroll):
                    scores(qi, b0 + u + 1, ring[u % 2])
                    softmax_pv(b0 + u, ring[(u + 1) % 2], masked=False)
                return c

            lax.fori_loop(0, n_groups, group, 0)
            base = 1 + n_groups * unroll
            rest = qi - base
            for k in range(0, unroll - 2, 2):
                @pl.when(rest >= k + 2)
                def _(k=k):
                    for u in range(2):
                        scores(qi, base + k + u + 1, ring[u % 2])
                        softmax_pv(base + k + u, ring[(u + 1) % 2], masked=False)

            @pl.when(rest % 2 == 1)
            def _():
                scores(qi, base + rest, ring[0])
                softmax_pv(base + rest - 1, ring[1], masked=False)

            for parity in (0, 1):
                @pl.when(rest % 2 == parity)
                def _(parity=parity):
                    scores(nxt, 0, s2_ref)
                    softmax_pv(qi, ring[(1 + parity) % 2], masked=True)

        rows = pl.ds(pl.multiple_of(qi * t, t), t)
        o_ref[rows, :] = (acc_ref[...] / l_ref[...]).T.astype(o_ref.dtype)
        return carry

    scores(0, 0, s2_ref)
    lax.fori_loop(0, nq, query_block, 0)


def _flash_attention(k_nope, k_rope, q_t, v_t, *, n_heads, t, unroll=4):
    s = k_nope.shape[0]
    nq = s // t
    assert t % CHUNK == 0 and unroll % 2 == 0 and nq >= 2
    assert v_t.shape == (n_heads, nq, V_HEAD_DIM, t) and q_t.shape == (n_heads, nq, V7X_MXU_DIM, t)
    return pl.pallas_call(
        functools.partial(_flash_kernel, t=t, unroll=unroll),
        grid=(n_heads,),
        in_specs=[
            pl.BlockSpec((s, QK_NOPE_DIM), lambda h: (0, h)),
            pl.BlockSpec((s, V7X_LANES), lambda h: (0, 0)),
            pl.BlockSpec((1, nq, V7X_MXU_DIM, t), lambda h: (h, 0, 0, 0)),
            pl.BlockSpec((1, nq, V_HEAD_DIM, t), lambda h: (h, 0, 0, 0)),
        ],
        out_specs=pl.BlockSpec((s, V_HEAD_DIM), lambda h: (0, h)),
        out_shape=jax.ShapeDtypeStruct((s, n_heads * V_HEAD_DIM), BF16),
        scratch_shapes=[pltpu.VMEM((t, t), F32)] * 4 + [pltpu.VMEM((1, t), F32), pltpu.VMEM((1, t), F32),
                                                        pltpu.VMEM((V_HEAD_DIM, t), F32)],
        compiler_params=_params(("arbitrary",),
                                [((s, QK_NOPE_DIM), BF16), ((s, V7X_LANES), BF16), ((V7X_MXU_DIM, s), BF16),
                                 ((V_HEAD_DIM, s), BF16), ((s, V_HEAD_DIM), BF16)],
                                scratch=[((4 * t + V_HEAD_DIM + 16, t), F32)],
                                temps=[((t, t), F32), ((t, t), F32), ((t, t), BF16), ((t, V7X_MXU_DIM), BF16)]),
        name="flash_attention",
    )(k_nope, k_rope, q_t, v_t)


def _gated_merge_kernel(pm_ref, at_ref, h_ref, wup_ref, wum_ref, wga_ref, wgb_ref, wo_ref, o_ref, wo_o_ref):
    h = h_ref[...]
    ya = jnp.dot(pm_ref[...], wup_ref[...], preferred_element_type=F32)
    yb = jnp.dot(at_ref[...], wum_ref[...], preferred_element_type=F32)
    ga = jax.nn.sigmoid(jnp.dot(h, wga_ref[...], preferred_element_type=F32))
    gb = jax.nn.sigmoid(jnp.dot(h, wgb_ref[...], preferred_element_type=F32))
    o_ref[...] = (ga * ya + gb * yb).astype(o_ref.dtype)
    wo_o_ref[...] = wo_ref[...].astype(wo_o_ref.dtype)


def _gated_merge(pm, attn, h, w_up_pool, w_up_mla, w_gate2d, w_out, *, tm=1024, tn=256):
    s, d = h.shape
    kp, km = pm.shape[1], attn.shape[1]
    nb = d // tn
    slab_block, slab_spec = _slab_spec(w_out, (s // tm) * nb, nb)
    return pl.pallas_call(
        _gated_merge_kernel,
        grid=(s // tm, nb),
        in_specs=[
            pl.BlockSpec((tm, kp), lambda i, j: (i, 0)),
            pl.BlockSpec((tm, km), lambda i, j: (i, 0)),
            pl.BlockSpec((tm, d), lambda i, j: (i, 0)),
            pl.BlockSpec((kp, tn), lambda i, j: (0, j)),
            pl.BlockSpec((km, tn), lambda i, j: (0, j)),
            pl.BlockSpec((d, tn), lambda i, j: (0, j)),
            pl.BlockSpec((d, tn), lambda i, j: (0, j + nb)),
            slab_spec,
        ],
        out_specs=[pl.BlockSpec((tm, tn), lambda i, j: (i, j)), slab_spec],
        out_shape=[jax.ShapeDtypeStruct((s, d), BF16), jax.ShapeDtypeStruct(w_out.shape, BF16)],
        compiler_params=_params(("parallel", "parallel"),
                                [((tm, kp), BF16), ((tm, km), BF16), ((tm, d), BF16), ((kp, tn), BF16),
                                 ((km, tn), BF16), ((d, tn), BF16), ((d, tn), BF16), ((tm, tn), BF16),
                                 (slab_block, F32), (slab_block, BF16)],
                                temps=[((tm, tn), F32)] * 4),
        name="gated_merge",
    )(pm, attn, h, w_up_pool, w_up_mla, w_gate2d, w_gate2d, w_out)


def _residual_norm_kernel(t_ref, x_ref, gpost_ref, gnext_ref, xo_ref, ho_ref):
    xo = x_ref[...] + _rms(t_ref[...].astype(F32), gpost_ref[...])
    xo_ref[...] = xo
    ho_ref[...] = _rms(xo, gnext_ref[...]).astype(ho_ref.dtype)


def _residual_final_kernel(t_ref, x_ref, gpost_ref, xo_ref):
    xo_ref[...] = x_ref[...] + _rms(t_ref[...].astype(F32), gpost_ref[...])


def _residual_norm(t, x, g_post, g_next=None, *, tm=256):
    s, d = x.shape
    row = pl.BlockSpec((tm, d), lambda i: (i, 0))
    vec = pl.BlockSpec((1, d), lambda i: (0, 0))
    blocks = [((tm, d), t.dtype), ((tm, d), F32), ((tm, d), F32)]
    if g_next is None:
        return pl.pallas_call(
            _residual_final_kernel, grid=(s // tm,), in_specs=[row, row, vec], out_specs=row,
            out_shape=jax.ShapeDtypeStruct((s, d), F32),
            compiler_params=_params(("parallel",), blocks, temps=[((tm, d), F32)] * 2),
            name="residual_final",
        )(t, x, g_post)
    return pl.pallas_call(
        _residual_norm_kernel, grid=(s // tm,), in_specs=[row, row, vec, vec], out_specs=[row, row],
        out_shape=[jax.ShapeDtypeStruct((s, d), F32), jax.ShapeDtypeStruct((s, d), BF16)],
        compiler_params=_params(("parallel",), blocks + [((tm, d), BF16)], temps=[((tm, d), F32)] * 2),
        name="residual_norm",
    )(t, x, g_post, g_next)


def _swiglu_up_kernel(h_ref, wg_ref, wu_ref, wd_ref, o_ref, wd_o_ref):
    h = h_ref[...]
    gate = jnp.dot(h, _load_bf16(wg_ref), preferred_element_type=F32)
    up = jnp.dot(h, _load_bf16(wu_ref), preferred_element_type=F32)
    o_ref[...] = (gate * jax.nn.sigmoid(gate) * up).astype(o_ref.dtype)
    wd_o_ref[...] = wd_ref[...].astype(wd_o_ref.dtype)


def _swiglu_up(h, w_gate, w_up, w_down, *, tm=1024, tn=256):
    s, d = h.shape
    f = w_gate.shape[1]
    nj = f // tn
    steps = (s // tm) * nj
    slab = w_down.shape[0] // steps
    assert slab * steps == w_down.shape[0] and slab % 16 == 0
    dn = w_down.shape[1]
    return pl.pallas_call(
        _swiglu_up_kernel,
        grid=(s // tm, nj),
        in_specs=[
            pl.BlockSpec((tm, d), lambda i, j: (i, 0)),
            pl.BlockSpec((d, tn), lambda i, j: (0, j)),
            pl.BlockSpec((d, tn), lambda i, j: (0, j)),
            pl.BlockSpec((slab, dn), lambda i, j: (i * nj + j, 0)),
        ],
        out_specs=[pl.BlockSpec((tm, tn), lambda i, j: (i, j)),
                   pl.BlockSpec((slab, dn), lambda i, j: (i * nj + j, 0))],
        out_shape=[jax.ShapeDtypeStruct((s, f), BF16), jax.ShapeDtypeStruct(w_down.shape, BF16)],
        compiler_params=_params(("parallel", "parallel"),
                                [((tm, d), BF16), ((d, tn), w_gate.dtype), ((d, tn), w_up.dtype), ((tm, tn), BF16),
                                 ((slab, dn), w_down.dtype), ((slab, dn), BF16)],
                                temps=[((tm, tn), F32)] * 3 + [((d, tn), BF16)] * 2),
        name="swiglu_up",
    )(h, w_gate, w_up, w_down)


def _ple_gate_kernel(h_ref, p_ref, wg_ref, wp_ref, o_ref):
    gate = jax.nn.sigmoid(jnp.dot(h_ref[...], _load_bf16(wg_ref), preferred_element_type=F32))
    pe = jnp.dot(_load_bf16(p_ref), _load_bf16(wp_ref), preferred_element_type=F32)
    o_ref[...] = (pe * gate).astype(o_ref.dtype)


def _ple_gate(h, p, w_gate, w_proj, *, tm=1024, tn=512):
    s, d = h.shape
    r = p.shape[1]
    return pl.pallas_call(
        _ple_gate_kernel,
        grid=(s // tm, d // tn),
        in_specs=[
            pl.BlockSpec((tm, d), lambda i, j: (i, 0)),
            pl.BlockSpec((tm, r), lambda i, j: (i, 0)),
            pl.BlockSpec((d, tn), lambda i, j: (0, j)),
            pl.BlockSpec((r, tn), lambda i, j: (0, j)),
        ],
        out_specs=pl.BlockSpec((tm, tn), lambda i, j: (i, j)),
        out_shape=jax.ShapeDtypeStruct((s, d), BF16),
        compiler_params=_params(("parallel", "parallel"),
                                [((tm, d), BF16), ((tm, r), p.dtype), ((d, tn), w_gate.dtype), ((r, tn), w_proj.dtype),
                                 ((tm, tn), BF16)],
                                temps=[((tm, tn), F32)] * 2 + [((d, tn), BF16)]),
        name="ple_gate",
    )(h, p, w_gate, w_proj)


def kernel(x, p, positions, norm_mix_pre, norm_mix_post, w_in, q_norm, kv_norm, w_q_b, w_kv_b, w_pool, pool_scale,
           w_up_pool, w_up_mla, w_branch_gate, w_out, norm_ffn_pre, norm_ffn_post, w_ffn_gate, w_ffn_up,
           w_ffn_down, norm_ple_pre, w_ple_gate, w_ple_proj, norm_ple_post):
    batch, seq, d_model = x.shape
    depth = w_in.shape[0]
    in_width = w_in.shape[2]
    q_rank, n_heads, qk_dim = w_q_b.shape[1:]
    kv_rank = w_kv_b.shape[1]
    pool_width = w_up_pool.shape[1]
    assert batch == 1 and qk_dim == QK_NOPE_DIM + QK_ROPE_DIM
    assert w_kv_b.shape[3] == QK_NOPE_DIM + V_HEAD_DIM
    o_q, o_kv, o_kr = pool_width, pool_width + q_rank, pool_width + q_rank + kv_rank
    assert o_kr + QK_ROPE_DIM == in_width
    assert o_q % q_rank == 0 and o_kv % kv_rank == 0 and o_kr % QK_ROPE_DIM == 0
    z_tn = 512
    assert o_kr % z_tn == 0
    attn_t = 512
    half = QK_ROPE_DIM // 2
    scale = float(qk_dim) ** -0.5 * LOG2_E

    inv_freq = ROPE_THETA ** (-jnp.arange(0, QK_ROPE_DIM, 2, dtype=F32) / QK_ROPE_DIM)
    freq_col = inv_freq.reshape(half, 1)
    freq_row = jnp.zeros((1, V7X_LANES), F32).at[0, :half].set(inv_freq).at[0, half:QK_ROPE_DIM].set(inv_freq)
    pos_f = positions.astype(F32)
    pos_row, pos_col = pos_f.reshape(1, seq), pos_f.reshape(seq, 1)

    xs = x.reshape(seq, d_model)
    for i in range(depth):
        w_in_t = jnp.transpose(w_in[i])
        wq = w_q_b[i].astype(BF16)
        wq_t = jnp.pad(jnp.transpose(wq, (1, 2, 0)), ((0, 0), (0, V7X_MXU_DIM - qk_dim), (0, 0)))
        wq_t = wq_t.reshape(n_heads * V7X_MXU_DIM, q_rank)
        wkv = w_kv_b[i].astype(BF16)
        wk = wkv[:, :, :QK_NOPE_DIM].reshape(kv_rank, n_heads * QK_NOPE_DIM)
        wv_t = jnp.transpose(wkv[:, :, QK_NOPE_DIM:], (1, 2, 0)).reshape(n_heads * V_HEAD_DIM, kv_rank)
        w_up_p, w_up_m = _cast_pair(w_up_pool[i], w_up_mla[i])

        def row(v):
            return v[i].reshape(1, -1)

        h = _rmsnorm_cast(xs, row(norm_mix_pre))
        z = _matmul(h, w_in_t, tm=1024, tn=z_tn, out_dtype=F32, name="in_proj", b_is_nk=True, n=o_kr)
        pm = _pool_mixer(z, w_pool[i].astype(BF16), row(pool_scale), pool_width=pool_width)
        q_t = _q_proj(z, row(q_norm), wq_t, pos_row, freq_col, col_block=o_q // q_rank, rank=q_rank, scale=scale,
                      tm=attn_t)
        k_nope, v_t, k_rope = _kv_proj(z, h, w_in_t, row(kv_norm), wk, wv_t, pos_col, freq_row,
                                       lat_block=o_kv // kv_rank, rope_row_block=o_kr // QK_ROPE_DIM, rank=kv_rank,
                                       tm=attn_t)
        attn = _flash_attention(k_nope, k_rope, q_t, v_t, n_heads=n_heads, t=attn_t)
        w_gate2d = w_branch_gate[i].astype(BF16).reshape(d_model, 2 * d_model)
        merged, w_out_b = _gated_merge(pm, attn, h, w_up_p, w_up_m, w_gate2d, w_out[i])
        mix = _matmul(merged, w_out_b, tm=1024, tn=1024, out_dtype=BF16, name="out_proj")
        xs, h2 = _residual_norm(mix, xs, row(norm_mix_post), row(norm_ffn_pre))
        act, w_down = _swiglu_up(h2, w_ffn_gate[i], w_ffn_up[i], w_ffn_down[i])
        ffn, w_pg = _matmul(act, w_down, tm=512, tn=512, out_dtype=BF16, name="ffn_down", side_cast=w_ple_gate[i])
        xs, h3 = _residual_norm(ffn, xs, row(norm_ffn_post), row(norm_ple_pre))
        t = _ple_gate(h3, p[i].reshape(seq, -1), w_pg, w_ple_proj[i], tn=1024)
        xs = _residual_norm(t, xs, row(norm_ple_post))
    return xs.reshape(batch, seq, d_model)
```

```python
import functools

import jax
import jax.numpy as jnp
from jax import lax
from jax.experimental import pallas as pl
from jax.experimental.pallas import tpu as pltpu

CHUNK = 64
EPS = 1e-6
POOL_WINDOWS = (2, 4, 8, 16)
QK_NOPE_DIM = 128
QK_ROPE_DIM = 64
V_HEAD_DIM = 128
ROPE_THETA = 10000.0
LOG2_E = 1.4426950408889634

V7X_LANES = 128
V7X_MXU_DIM = 256
V7X_VMEM_BYTES = 64 * 1024 * 1024
V7X_VMEM_USABLE_BYTES = 60000 * 1024
V7X_VMEM_DEFAULT_SCOPED_BYTES = 32 * 1024 * 1024

POOL_HALO = 16
F32 = jnp.float32
BF16 = jnp.bfloat16


def _nbytes(shape, dtype):
    n = 1
    for s in shape:
        n *= s
    return n * jnp.dtype(dtype).itemsize


def _params(semantics, blocks, scratch=(), temps=()):
    need = 2 * sum(_nbytes(s, d) for s, d in blocks)
    need += sum(_nbytes(s, d) for s, d in scratch) + sum(_nbytes(s, d) for s, d in temps)
    need = max(int(need * 1.25) + (2 << 20), V7X_VMEM_DEFAULT_SCOPED_BYTES)
    return pltpu.CompilerParams(dimension_semantics=semantics,
                                vmem_limit_bytes=min(need, V7X_VMEM_USABLE_BYTES))


def _load_bf16(ref):
    v = ref[...]
    return v if v.dtype == BF16 else v.astype(BF16)


def _sigmoid(x):
    return 0.5 * jnp.tanh(0.5 * x) + 0.5


def _rms(xf, g):
    return xf * lax.rsqrt(jnp.mean(xf * xf, axis=-1, keepdims=True) + EPS) * g


def _rmsnorm_cast_kernel(x_ref, g_ref, o_ref):
    o_ref[...] = _rms(x_ref[...], g_ref[...]).astype(o_ref.dtype)


def _rmsnorm_cast(x, g, *, tm=256):
    s, d = x.shape
    return pl.pallas_call(
        _rmsnorm_cast_kernel,
        grid=(s // tm,),
        in_specs=[pl.BlockSpec((tm, d), lambda i: (i, 0)), pl.BlockSpec((1, d), lambda i: (0, 0))],
        out_specs=pl.BlockSpec((tm, d), lambda i: (i, 0)),
        out_shape=jax.ShapeDtypeStruct((s, d), BF16),
        compiler_params=_params(("parallel",), [((tm, d), F32), ((tm, d), BF16)], temps=[((tm, d), F32)]),
        name="rmsnorm_cast",
    )(x, g)


def _cast_pair_kernel(a_ref, b_ref, ao_ref, bo_ref):
    ao_ref[...] = a_ref[...].astype(ao_ref.dtype)
    bo_ref[...] = b_ref[...].astype(bo_ref.dtype)


def _cast_pair(a, b, *, rows=256):
    k, n = a.shape
    assert b.shape == a.shape
    spec = pl.BlockSpec((rows, n), lambda i: (i, 0))
    return pl.pallas_call(
        _cast_pair_kernel,
        grid=(k // rows,),
        in_specs=[spec, spec],
        out_specs=[spec, spec],
        out_shape=[jax.ShapeDtypeStruct((k, n), BF16)] * 2,
        compiler_params=_params(("parallel",), [((rows, n), F32), ((rows, n), BF16)] * 2),
        name="cast_up_weights",
    )(a, b)


def _slab_spec(w, steps, nj):
    slab = w.shape[0] // steps
    assert slab * steps == w.shape[0] and slab % 16 == 0
    return (slab, w.shape[1]), pl.BlockSpec((slab, w.shape[1]), lambda i, j: (i * nj + j, 0))


def _matmul_kernel(a_ref, b_ref, *rest, b_is_nk):
    o_ref = rest[-2] if len(rest) == 3 else rest[0]
    contract_b = 1 if b_is_nk else 0
    o_ref[...] = lax.dot_general(a_ref[...], _load_bf16(b_ref), (((1,), (contract_b,)), ((), ())),
                                 preferred_element_type=F32).astype(o_ref.dtype)
    if len(rest) == 3:
        rest[2][...] = rest[0][...].astype(rest[2].dtype)


def _matmul(a, b, *, tm, tn, out_dtype, name, b_is_nk=False, n=None, side_cast=None):
    m, k = a.shape
    if n is None:
        n = b.shape[0] if b_is_nk else b.shape[1]
    if b_is_nk:
        b_block, b_spec = (tn, k), pl.BlockSpec((tn, k), lambda i, j: (j, 0))
    else:
        b_block, b_spec = (k, tn), pl.BlockSpec((k, tn), lambda i, j: (0, j))
    grid = (m // tm, n // tn)
    in_specs = [pl.BlockSpec((tm, k), lambda i, j: (i, 0)), b_spec]
    out_specs = pl.BlockSpec((tm, tn), lambda i, j: (i, j))
    out_shape = jax.ShapeDtypeStruct((m, n), out_dtype)
    blocks = [((tm, k), a.dtype), (b_block, b.dtype), ((tm, tn), out_dtype)]
    args = (a, b)
    if side_cast is not None:
        slab_block, slab_spec = _slab_spec(side_cast, grid[0] * grid[1], grid[1])
        in_specs.append(slab_spec)
        out_specs = [out_specs, slab_spec]
        out_shape = [out_shape, jax.ShapeDtypeStruct(side_cast.shape, BF16)]
        blocks += [(slab_block, F32), (slab_block, BF16)]
        args = (a, b, side_cast)
    return pl.pallas_call(
        functools.partial(_matmul_kernel, b_is_nk=b_is_nk),
        grid=grid, in_specs=in_specs, out_specs=out_specs, out_shape=out_shape,
        compiler_params=_params(("parallel", "parallel"), blocks, temps=[((tm, tn), F32), ((k, tn), BF16)]),
        name=name,
    )(*args)


def _pool_mixer_kernel(halo_ref, u_ref, w_ref, scale_ref, o_ref, ext_ref, *, tm, gw):
    i = pl.program_id(0)
    ext_ref[0:POOL_HALO, :] = jnp.where(i > 0, halo_ref[...], 0.0)
    ext_ref[POOL_HALO:POOL_HALO + tm, :] = u_ref[...]
    row = lax.broadcasted_iota(jnp.int32, (tm, 1), 0) + i * tm
    for g, w in enumerate(POOL_WINDOWS):
        cols = slice(g * gw, (g + 1) * gw)
        u = u_ref[:, cols]
        win_sum = u
        for back in range(1, w):
            win_sum = win_sum + ext_ref[POOL_HALO - back:POOL_HALO - back + tm, cols]
        cnt = jnp.minimum(row + 1, w).astype(F32)
        pooled = win_sum / cnt - u
        mixed = jnp.dot(pooled.astype(BF16), w_ref[g], preferred_element_type=F32)
        o_ref[:, cols] = (mixed * scale_ref[:, cols]).astype(o_ref.dtype)


def _pool_mixer(z, w_pool, pool_scale, *, pool_width, tm=512):
    s = z.shape[0]
    groups, gw, _ = w_pool.shape
    halo_blocks = tm // POOL_HALO
    return pl.pallas_call(
        functools.partial(_pool_mixer_kernel, tm=tm, gw=gw),
        grid=(s // tm,),
        in_specs=[
            pl.BlockSpec((POOL_HALO, pool_width), lambda i: (jnp.maximum(i * halo_blocks - 1, 0), 0)),
            pl.BlockSpec((tm, pool_width), lambda i: (i, 0)),
            pl.BlockSpec((groups, gw, gw), lambda i: (0, 0, 0)),
            pl.BlockSpec((1, pool_width), lambda i: (0, 0)),
        ],
        out_specs=pl.BlockSpec((tm, pool_width), lambda i: (i, 0)),
        out_shape=jax.ShapeDtypeStruct((s, pool_width), BF16),
        scratch_shapes=[pltpu.VMEM((POOL_HALO + tm, pool_width), F32)],
        compiler_params=_params(("parallel",),
                                [((tm, pool_width), F32), ((groups, gw, gw), BF16), ((tm, pool_width), BF16)],
                                scratch=[((POOL_HALO + tm, pool_width), F32)],
                                temps=[((tm, pool_width), F32)]),
        name="pool_mixer",
    )(z, z, w_pool, pool_scale)


def _q_proj_kernel(ql_ref, g_ref, w_ref, pos_ref, freq_ref, o_ref, qn_ref, cos_ref, sin_ref, *, scale):
    @pl.when(pl.program_id(1) == 0)
    def _():
        qn_ref[...] = (_rms(ql_ref[...], g_ref[...]) * scale).astype(qn_ref.dtype)
        ang = freq_ref[...] * pos_ref[...]
        cos_ref[...] = jnp.cos(ang)
        sin_ref[...] = jnp.sin(ang)

    qt = lax.dot_general(w_ref[...], qn_ref[...], (((1,), (1,)), ((), ())), preferred_element_type=F32)
    half = QK_ROPE_DIM // 2
    r0, r1, r2 = QK_NOPE_DIM, QK_NOPE_DIM + half, QK_NOPE_DIM + QK_ROPE_DIM
    c, sn = cos_ref[...], sin_ref[...]
    rows = o_ref.shape[2]
    for hh in range(o_ref.shape[0]):
        q = qt[hh * rows:(hh + 1) * rows]
        x1, x2 = q[r0:r1], q[r1:r2]
        o_ref[hh, 0, 0:r0, :] = q[0:r0].astype(o_ref.dtype)
        o_ref[hh, 0, r0:r1, :] = (x1 * c - x2 * sn).astype(o_ref.dtype)
        o_ref[hh, 0, r1:r2, :] = (x1 * sn + x2 * c).astype(o_ref.dtype)
        o_ref[hh, 0, r2:, :] = q[r2:].astype(o_ref.dtype)


def _q_proj(z, q_norm, wq_t, pos_row, freq_col, *, col_block, rank, scale, tm, heads_per_step=8):
    s = z.shape[0]
    rows = V7X_MXU_DIM
    n_heads = wq_t.shape[0] // rows
    hps = heads_per_step
    half = QK_ROPE_DIM // 2
    return pl.pallas_call(
        functools.partial(_q_proj_kernel, scale=scale),
        grid=(s // tm, n_heads // hps),
        in_specs=[
            pl.BlockSpec((tm, rank), lambda i, h: (i, col_block)),
            pl.BlockSpec((1, rank), lambda i, h: (0, 0)),
            pl.BlockSpec((hps * rows, rank), lambda i, h: (h, 0)),
            pl.BlockSpec((1, tm), lambda i, h: (0, i)),
            pl.BlockSpec((half, 1), lambda i, h: (0, 0)),
        ],
        out_specs=pl.BlockSpec((hps, 1, rows, tm), lambda i, h: (h, i, 0, 0)),
        out_shape=jax.ShapeDtypeStruct((n_heads, s // tm, rows, tm), BF16),
        scratch_shapes=[pltpu.VMEM((tm, rank), BF16), pltpu.VMEM((half, tm), F32), pltpu.VMEM((half, tm), F32)],
        compiler_params=_params(("parallel", "arbitrary"),
                                [((tm, rank), F32), ((hps * rows, rank), BF16), ((hps * rows, tm), BF16)],
                                scratch=[((tm, rank), BF16), ((2 * half, tm), F32)],
                                temps=[((tm, rank), F32), ((hps * rows, tm), F32)]),
        name="q_proj",
    )(z, q_norm, wq_t, pos_row, freq_col)


def _kv_proj_kernel(kvl_ref, h_ref, wkr_ref, g_ref, wk_ref, wvt_ref, pos_ref, freq_ref, k_ref, vt_ref, kro_ref):
    kvn = _rms(kvl_ref[...], g_ref[...]).astype(BF16)
    k_ref[...] = jnp.dot(kvn, wk_ref[...], preferred_element_type=F32).astype(k_ref.dtype)
    vt = lax.dot_general(wvt_ref[...], kvn, (((1,), (1,)), ((), ())), preferred_element_type=F32)
    vt_ref[:, 0] = vt.reshape(vt_ref.shape[0], V_HEAD_DIM, vt.shape[1]).astype(vt_ref.dtype)
    half = QK_ROPE_DIM // 2
    wkr = _load_bf16(wkr_ref)
    wkr = jnp.concatenate([wkr, jnp.zeros((V7X_LANES - wkr.shape[0], wkr.shape[1]), BF16)], axis=0)
    x = lax.dot_general(h_ref[...], wkr, (((1,), (1,)), ((), ())), preferred_element_type=F32)
    ang = pos_ref[...] * freq_ref[...]
    lane = lax.broadcasted_iota(jnp.int32, x.shape, 1)
    x2_at_lo = pltpu.roll(x, V7X_LANES - half, axis=1)
    x1_at_hi = pltpu.roll(x, half, axis=1)
    partner = jnp.where(lane < half, -x2_at_lo, jnp.where(lane < QK_ROPE_DIM, x1_at_hi, 0.0))
    kro_ref[...] = (x * jnp.cos(ang) + partner * jnp.sin(ang)).astype(kro_ref.dtype)


def _kv_proj(z, h, w_in_t, kv_norm, wk, wv_t, pos_col, freq_row, *, lat_block, rope_row_block, rank, tm):
    s, d = h.shape
    nk = wk.shape[1]
    nv = wv_t.shape[0]
    n_heads = nv // V_HEAD_DIM
    return pl.pallas_call(
        _kv_proj_kernel,
        grid=(s // tm,),
        in_specs=[
            pl.BlockSpec((tm, rank), lambda i: (i, lat_block)),
            pl.BlockSpec((tm, d), lambda i: (i, 0)),
            pl.BlockSpec((QK_ROPE_DIM, d), lambda i: (rope_row_block, 0)),
            pl.BlockSpec((1, rank), lambda i: (0, 0)),
            pl.BlockSpec((rank, nk), lambda i: (0, 0)),
            pl.BlockSpec((nv, rank), lambda i: (0, 0)),
            pl.BlockSpec((tm, 1), lambda i: (i, 0)),
            pl.BlockSpec((1, V7X_LANES), lambda i: (0, 0)),
        ],
        out_specs=[
            pl.BlockSpec((tm, nk), lambda i: (i, 0)),
            pl.BlockSpec((n_heads, 1, V_HEAD_DIM, tm), lambda i: (0, i, 0, 0)),
            pl.BlockSpec((tm, V7X_LANES), lambda i: (i, 0)),
        ],
        out_shape=[
            jax.ShapeDtypeStruct((s, nk), BF16),
            jax.ShapeDtypeStruct((n_heads, s // tm, V_HEAD_DIM, tm), BF16),
            jax.ShapeDtypeStruct((s, V7X_LANES), BF16),
        ],
        compiler_params=_params(("parallel",),
                                [((tm, rank), F32), ((tm, d), BF16), ((QK_ROPE_DIM, d), w_in_t.dtype), ((rank, nk), BF16),
                                 ((nv, rank), BF16), ((tm, nk), BF16), ((nv, tm), BF16), ((tm, V7X_LANES), BF16),
                                 ((tm, V7X_LANES), F32)],
                                temps=[((tm, nk), F32), ((nv, tm), F32), ((V7X_LANES, d), BF16)]),
        name="kv_proj",
    )(z, h, w_in_t, kv_norm, wk, wv_t, pos_col, freq_row)


def _flash_kernel(kn_ref, kr_ref, qt_ref, vt_ref, o_ref, s0_ref, s1_ref, s2_ref, bias_ref, smax0_ref, smax1_ref,
                  smax2_ref, m_ref, l_ref, acc_ref, *, t, unroll):
    nq = qt_ref.shape[1]
    ring = ((s0_ref, smax0_ref), (s1_ref, smax1_ref))
    first = (s2_ref, smax2_ref)
    kc = lax.broadcasted_iota(jnp.int32, (t, t), 0) // CHUNK
    qc = lax.broadcasted_iota(jnp.int32, (t, t), 1) // CHUNK
    bias_ref[...] = jnp.where(kc <= qc, 0.0, -jnp.inf).astype(F32)

    def scores(qi, kb, buf):
        s_ref, smax_ref = buf
        ks = pl.ds(pl.multiple_of(kb * t, t), t)
        kcat = jnp.concatenate([kn_ref[ks, :], kr_ref[ks, :]], axis=1)
        s = jnp.dot(kcat, qt_ref[0, qi], preferred_element_type=F32)
        s_ref[...] = s
        smax_ref[...] = jnp.max(s, axis=0, keepdims=True)

    def softmax_pv(kb, buf, masked):
        s_ref, smax_ref = buf
        s = s_ref[...]
        if masked:
            s = s + bias_ref[...]
            block_max = jnp.max(s, axis=0, keepdims=True)
        else:
            block_max = smax_ref[...]
        m_prev = m_ref[...]
        m_new = jnp.maximum(m_prev, block_max)
        alpha = jnp.exp2(m_prev - m_new)
        p = jnp.exp2(s - m_new)
        l_ref[...] = alpha * l_ref[...] + jnp.sum(p, axis=0, keepdims=True)
        pv = jnp.dot(vt_ref[0, kb], p.astype(BF16), preferred_element_type=F32)
        acc_ref[...] = alpha * acc_ref[...] + pv
        m_ref[...] = m_new

    def query_block(qi, carry):
        m_ref[...] = jnp.full(m_ref.shape, -jnp.inf, F32)
        l_ref[...] = jnp.zeros(l_ref.shape, F32)
        acc_ref[...] = jnp.zeros(acc_ref.shape, F32)
        nxt = jnp.minimum(qi + 1, nq - 1)

        @pl.when(qi == 0)
        def _():
            softmax_pv(0, first, masked=True)
            scores(nxt, 0, first)

        @pl.when(qi > 0)
        def _():
            scores(qi, 1, ring[1])
            softmax_pv(0, first, masked=False)
            n_groups = (qi - 1) // unroll

            def group(g, c):
                b0 = 1 + g * unroll
                for u in range(unroll):
                    scores(qi, b0 + u + 1, ring[u % 2])
                    softmax_pv(b0 + u, ring[(u + 1) % 2], masked=False)
                return c

            lax.fori_loop(0, n_groups, group, 0)
            base = 1 + n_groups * unroll
            rest = qi - base
            for k in range(0, unroll - 2, 2):
                @pl.when(rest >= k + 2)
                def _(k=k):
                    for u in range(2):
                        scores(qi, base + k + u + 1, ring[u % 2])
                        softmax_pv(base + k + u, ring[(u + 1) % 2], masked=False)

            @pl.when(rest % 2 == 1)
            def _():
                scores(qi, base + rest, ring[0])
                softmax_pv(base + rest - 1, ring[1], masked=False)

            for parity in (0, 1):
                @pl.when(rest % 2 == parity)
                def _(parity=parity):
                    scores(nxt, 0, first)
                    softmax_pv(qi, ring[(1 + parity) % 2], masked=True)

        rows = pl.ds(pl.multiple_of(qi * t, t), t)
        o_ref[rows, :] = (acc_ref[...] / l_ref[...]).T.astype(o_ref.dtype)
        return carry

    scores(0, 0, first)
    lax.fori_loop(0, nq, query_block, 0)


def _flash_attention(k_nope, k_rope, q_t, v_t, *, n_heads, t, unroll=4):
    s = k_nope.shape[0]
    nq = s // t
    assert t % CHUNK == 0 and unroll % 2 == 0 and nq >= 2
    assert v_t.shape == (n_heads, nq, V_HEAD_DIM, t) and q_t.shape == (n_heads, nq, V7X_MXU_DIM, t)
    return pl.pallas_call(
        functools.partial(_flash_kernel, t=t, unroll=unroll),
        grid=(n_heads,),
        in_specs=[
            pl.BlockSpec((s, QK_NOPE_DIM), lambda h: (0, h)),
            pl.BlockSpec((s, V7X_LANES), lambda h: (0, 0)),
            pl.BlockSpec((1, nq, V7X_MXU_DIM, t), lambda h: (h, 0, 0, 0)),
            pl.BlockSpec((1, nq, V_HEAD_DIM, t), lambda h: (h, 0, 0, 0)),
        ],
        out_specs=pl.BlockSpec((s, V_HEAD_DIM), lambda h: (0, h)),
        out_shape=jax.ShapeDtypeStruct((s, n_heads * V_HEAD_DIM), BF16),
        scratch_shapes=[pltpu.VMEM((t, t), F32)] * 4 + [pltpu.VMEM((1, t), F32)] * 5 + [
            pltpu.VMEM((V_HEAD_DIM, t), F32)],
        compiler_params=_params(("arbitrary",),
                                [((s, QK_NOPE_DIM), BF16), ((s, V7X_LANES), BF16), ((V7X_MXU_DIM, s), BF16),
                                 ((V_HEAD_DIM, s), BF16), ((s, V_HEAD_DIM), BF16)],
                                scratch=[((4 * t + V_HEAD_DIM + 16, t), F32)],
                                temps=[((t, t), F32), ((t, t), F32), ((t, t), BF16), ((t, V7X_MXU_DIM), BF16)]),
        name="flash_attention",
    )(k_nope, k_rope, q_t, v_t)


def _gated_merge_kernel(pm_ref, at_ref, h_ref, wup_ref, wum_ref, wga_ref, wgb_ref, wo_ref, o_ref, wo_o_ref):
    h = h_ref[...]
    ya = jnp.dot(pm_ref[...], wup_ref[...], preferred_element_type=F32)
    yb = jnp.dot(at_ref[...], wum_ref[...], preferred_element_type=F32)
    ga = _sigmoid(jnp.dot(h, wga_ref[...], preferred_element_type=F32))
    gb = _sigmoid(jnp.dot(h, wgb_ref[...], preferred_element_type=F32))
    o_ref[...] = (ga * ya + gb * yb).astype(o_ref.dtype)
    wo_o_ref[...] = wo_ref[...].astype(wo_o_ref.dtype)


def _gated_merge(pm, attn, h, w_up_pool, w_up_mla, w_gate2d, w_out, *, tm=1024, tn=256):
    s, d = h.shape
    kp, km = pm.shape[1], attn.shape[1]
    nb = d // tn
    slab_block, slab_spec = _slab_spec(w_out, (s // tm) * nb, nb)
    return pl.pallas_call(
        _gated_merge_kernel,
        grid=(s // tm, nb),
        in_specs=[
            pl.BlockSpec((tm, kp), lambda i, j: (i, 0)),
            pl.BlockSpec((tm, km), lambda i, j: (i, 0)),
            pl.BlockSpec((tm, d), lambda i, j: (i, 0)),
            pl.BlockSpec((kp, tn), lambda i, j: (0, j)),
            pl.BlockSpec((km, tn), lambda i, j: (0, j)),
            pl.BlockSpec((d, tn), lambda i, j: (0, j)),
            pl.BlockSpec((d, tn), lambda i, j: (0, j + nb)),
            slab_spec,
        ],
        out_specs=[pl.BlockSpec((tm, tn), lambda i, j: (i, j)), slab_spec],
        out_shape=[jax.ShapeDtypeStruct((s, d), BF16), jax.ShapeDtypeStruct(w_out.shape, BF16)],
        compiler_params=_params(("parallel", "parallel"),
                                [((tm, kp), BF16), ((tm, km), BF16), ((tm, d), BF16), ((kp, tn), BF16),
                                 ((km, tn), BF16), ((d, tn), BF16), ((d, tn), BF16), ((tm, tn), BF16),
                                 (slab_block, F32), (slab_block, BF16)],
                                temps=[((tm, tn), F32)] * 4),
        name="gated_merge",
    )(pm, attn, h, w_up_pool, w_up_mla, w_gate2d, w_gate2d, w_out)


def _residual_norm_kernel(t_ref, x_ref, gpost_ref, gnext_ref, xo_ref, ho_ref):
    xo = x_ref[...] + _rms(t_ref[...].astype(F32), gpost_ref[...])
    xo_ref[...] = xo
    ho_ref[...] = _rms(xo, gnext_ref[...]).astype(ho_ref.dtype)


def _residual_final_kernel(t_ref, x_ref, gpost_ref, xo_ref):
    xo_ref[...] = x_ref[...] + _rms(t_ref[...].astype(F32), gpost_ref[...])


def _residual_norm(t, x, g_post, g_next=None, *, tm=256):
    s, d = x.shape
    row = pl.BlockSpec((tm, d), lambda i: (i, 0))
    vec = pl.BlockSpec((1, d), lambda i: (0, 0))
    blocks = [((tm, d), t.dtype), ((tm, d), F32), ((tm, d), F32)]
    if g_next is None:
        return pl.pallas_call(
            _residual_final_kernel, grid=(s // tm,), in_specs=[row, row, vec], out_specs=row,
            out_shape=jax.ShapeDtypeStruct((s, d), F32),
            compiler_params=_params(("parallel",), blocks, temps=[((tm, d), F32)] * 2),
            name="residual_final",
        )(t, x, g_post)
    return pl.pallas_call(
        _residual_norm_kernel, grid=(s // tm,), in_specs=[row, row, vec, vec], out_specs=[row, row],
        out_shape=[jax.ShapeDtypeStruct((s, d), F32), jax.ShapeDtypeStruct((s, d), BF16)],
        compiler_params=_params(("parallel",), blocks + [((tm, d), BF16)], temps=[((tm, d), F32)] * 2),
        name="residual_norm",
    )(t, x, g_post, g_next)


def _swiglu_up_kernel(h_ref, wg_ref, wu_ref, wd_ref, o_ref, wd_o_ref):
    h = h_ref[...]
    gate = jnp.dot(h, _load_bf16(wg_ref), preferred_element_type=F32)
    up = jnp.dot(h, _load_bf16(wu_ref), preferred_element_type=F32)
    o_ref[...] = (gate * _sigmoid(gate) * up).astype(o_ref.dtype)
    wd_o_ref[...] = wd_ref[...].astype(wd_o_ref.dtype)


def _swiglu_up(h, w_gate, w_up, w_down, *, tm=1024, tn=256):
    s, d = h.shape
    f = w_gate.shape[1]
    nj = f // tn
    steps = (s // tm) * nj
    slab = w_down.shape[0] // steps
    assert slab * steps == w_down.shape[0] and slab % 16 == 0
    dn = w_down.shape[1]
    return pl.pallas_call(
        _swiglu_up_kernel,
        grid=(s // tm, nj),
        in_specs=[
            pl.BlockSpec((tm, d), lambda i, j: (i, 0)),
            pl.BlockSpec((d, tn), lambda i, j: (0, j)),
            pl.BlockSpec((d, tn), lambda i, j: (0, j)),
            pl.BlockSpec((slab, dn), lambda i, j: (i * nj + j, 0)),
        ],
        out_specs=[pl.BlockSpec((tm, tn), lambda i, j: (i, j)),
                   pl.BlockSpec((slab, dn), lambda i, j: (i * nj + j, 0))],
        out_shape=[jax.ShapeDtypeStruct((s, f), BF16), jax.ShapeDtypeStruct(w_down.shape, BF16)],
        compiler_params=_params(("parallel", "parallel"),
                                [((tm, d), BF16), ((d, tn), w_gate.dtype), ((d, tn), w_up.dtype), ((tm, tn), BF16),
                                 ((slab, dn), w_down.dtype), ((slab, dn), BF16)],
                                temps=[((tm, tn), F32)] * 3 + [((d, tn), BF16)] * 2),
        name="swiglu_up",
    )(h, w_gate, w_up, w_down)


def _ple_gate_kernel(h_ref, p_ref, wg_ref, wp_ref, o_ref):
    gate = _sigmoid(jnp.dot(h_ref[...], _load_bf16(wg_ref), preferred_element_type=F32))
    pe = jnp.dot(_load_bf16(p_ref), _load_bf16(wp_ref), preferred_element_type=F32)
    o_ref[...] = (pe * gate).astype(o_ref.dtype)


def _ple_gate(h, p, w_gate, w_proj, *, tm=1024, tn=512):
    s, d = h.shape
    r = p.shape[1]
    return pl.pallas_call(
        _ple_gate_kernel,
        grid=(s // tm, d // tn),
        in_specs=[
            pl.BlockSpec((tm, d), lambda i, j: (i, 0)),
            pl.BlockSpec((tm, r), lambda i, j: (i, 0)),
            pl.BlockSpec((d, tn), lambda i, j: (0, j)),
            pl.BlockSpec((r, tn), lambda i, j: (0, j)),
        ],
        out_specs=pl.BlockSpec((tm, tn), lambda i, j: (i, j)),
        out_shape=jax.ShapeDtypeStruct((s, d), BF16),
        compiler_params=_params(("parallel", "parallel"),
                                [((tm, d), BF16), ((tm, r), p.dtype), ((d, tn), w_gate.dtype), ((r, tn), w_proj.dtype),
                                 ((tm, tn), BF16)],
                                temps=[((tm, tn), F32)] * 2 + [((d, tn), BF16)]),
        name="ple_gate",
    )(h, p, w_gate, w_proj)


def kernel(x, p, positions, norm_mix_pre, norm_mix_post, w_in, q_norm, kv_norm, w_q_b, w_kv_b, w_pool, pool_scale,
           w_up_pool, w_up_mla, w_branch_gate, w_out, norm_ffn_pre, norm_ffn_post, w_ffn_gate, w_ffn_up,
           w_ffn_down, norm_ple_pre, w_ple_gate, w_ple_proj, norm_ple_post):
    batch, seq, d_model = x.shape
    depth = w_in.shape[0]
    in_width = w_in.shape[2]
    q_rank, n_heads, qk_dim = w_q_b.shape[1:]
    kv_rank = w_kv_b.shape[1]
    pool_width = w_up_pool.shape[1]
    assert batch == 1 and qk_dim == QK_NOPE_DIM + QK_ROPE_DIM
    assert w_kv_b.shape[3] == QK_NOPE_DIM + V_HEAD_DIM
    o_q, o_kv, o_kr = pool_width, pool_width + q_rank, pool_width + q_rank + kv_rank
    assert o_kr + QK_ROPE_DIM == in_width
    assert o_q % q_rank == 0 and o_kv % kv_rank == 0 and o_kr % QK_ROPE_DIM == 0
    z_tn = 512
    assert o_kr % z_tn == 0
    attn_t = 512
    half = QK_ROPE_DIM // 2
    scale = float(qk_dim) ** -0.5 * LOG2_E

    inv_freq = ROPE_THETA ** (-jnp.arange(0, QK_ROPE_DIM, 2, dtype=F32) / QK_ROPE_DIM)
    freq_col = inv_freq.reshape(half, 1)
    freq_row = jnp.zeros((1, V7X_LANES), F32).at[0, :half].set(inv_freq).at[0, half:QK_ROPE_DIM].set(inv_freq)
    pos_f = positions.astype(F32)
    pos_row, pos_col = pos_f.reshape(1, seq), pos_f.reshape(seq, 1)

    xs = x.reshape(seq, d_model)
    for i in range(depth):
        w_in_t = jnp.transpose(w_in[i])
        wq = w_q_b[i].astype(BF16)
        wq_t = jnp.pad(jnp.transpose(wq, (1, 2, 0)), ((0, 0), (0, V7X_MXU_DIM - qk_dim), (0, 0)))
        wq_t = wq_t.reshape(n_heads * V7X_MXU_DIM, q_rank)
        wkv = w_kv_b[i].astype(BF16)
        wk = wkv[:, :, :QK_NOPE_DIM].reshape(kv_rank, n_heads * QK_NOPE_DIM)
        wv_t = jnp.transpose(wkv[:, :, QK_NOPE_DIM:], (1, 2, 0)).reshape(n_heads * V_HEAD_DIM, kv_rank)
        w_up_p, w_up_m = _cast_pair(w_up_pool[i], w_up_mla[i])

        def row(v):
            return v[i].reshape(1, -1)

        h = _rmsnorm_cast(xs, row(norm_mix_pre))
        z = _matmul(h, w_in_t, tm=1024, tn=z_tn, out_dtype=F32, name="in_proj", b_is_nk=True, n=o_kr)
        pm = _pool_mixer(z, w_pool[i].astype(BF16), row(pool_scale), pool_width=pool_width)
        q_t = _q_proj(z, row(q_norm), wq_t, pos_row, freq_col, col_block=o_q // q_rank, rank=q_rank, scale=scale,
                      tm=attn_t)
        k_nope, v_t, k_rope = _kv_proj(z, h, w_in_t, row(kv_norm), wk, wv_t, pos_col, freq_row,
                                       lat_block=o_kv // kv_rank, rope_row_block=o_kr // QK_ROPE_DIM, rank=kv_rank,
                                       tm=attn_t)
        attn = _flash_attention(k_nope, k_rope, q_t, v_t, n_heads=n_heads, t=attn_t)
        w_gate2d = w_branch_gate[i].astype(BF16).reshape(d_model, 2 * d_model)
        merged, w_out_b = _gated_merge(pm, attn, h, w_up_p, w_up_m, w_gate2d, w_out[i])
        mix = _matmul(merged, w_out_b, tm=1024, tn=1024, out_dtype=BF16, name="out_proj")
        xs, h2 = _residual_norm(mix, xs, row(norm_mix_post), row(norm_ffn_pre))
        act, w_down = _swiglu_up(h2, w_ffn_gate[i], w_ffn_up[i], w_ffn_down[i])
        ffn, w_pg = _matmul(act, w_down, tm=512, tn=512, out_dtype=BF16, name="ffn_down", side_cast=w_ple_gate[i])
        xs, h3 = _residual_norm(ffn, xs, row(norm_ffn_post), row(norm_ple_pre))
        t = _ple_gate(h3, p[i].reshape(seq, -1), w_pg, w_ple_proj[i], tn=1024)
        xs = _residual_norm(t, xs, row(norm_ple_post))
    return xs.reshape(batch, seq, d_model)
```

```python
import functools

import jax
import jax.numpy as jnp
from jax import lax
from jax.experimental import pallas as pl
from jax.experimental.pallas import tpu as pltpu

CHUNK = 64
EPS = 1e-6
POOL_WINDOWS = (2, 4, 8, 16)
QK_NOPE_DIM = 128
QK_ROPE_DIM = 64
V_HEAD_DIM = 128
ROPE_THETA = 10000.0
LOG2_E = 1.4426950408889634

V7X_LANES = 128
V7X_MXU_DIM = 256
V7X_VMEM_BYTES = 64 * 1024 * 1024
V7X_VMEM_USABLE_BYTES = 60000 * 1024
V7X_VMEM_DEFAULT_SCOPED_BYTES = 32 * 1024 * 1024

POOL_HALO = 16
POOL_PAD = 8
F32 = jnp.float32
BF16 = jnp.bfloat16


def _nbytes(shape, dtype):
    n = 1
    for s in shape:
        n *= s
    return n * jnp.dtype(dtype).itemsize


def _params(semantics, blocks, scratch=(), temps=()):
    need = 2 * sum(_nbytes(s, d) for s, d in blocks)
    need += sum(_nbytes(s, d) for s, d in scratch) + sum(_nbytes(s, d) for s, d in temps)
    need = max(int(need * 1.25) + (2 << 20), V7X_VMEM_DEFAULT_SCOPED_BYTES)
    return pltpu.CompilerParams(dimension_semantics=semantics,
                                vmem_limit_bytes=min(need, V7X_VMEM_USABLE_BYTES))


def _load_bf16(ref):
    v = ref[...]
    return v if v.dtype == BF16 else v.astype(BF16)


def _sigmoid(x):
    return 0.5 * jnp.tanh(0.5 * x) + 0.5


def _rms(xf, g):
    return xf * lax.rsqrt(jnp.mean(xf * xf, axis=-1, keepdims=True) + EPS) * g


def _rmsnorm_cast_kernel(x_ref, g_ref, o_ref):
    o_ref[...] = _rms(x_ref[...], g_ref[...]).astype(o_ref.dtype)


def _rmsnorm_cast(x, g, *, tm=256):
    s, d = x.shape
    return pl.pallas_call(
        _rmsnorm_cast_kernel,
        grid=(s // tm,),
        in_specs=[pl.BlockSpec((tm, d), lambda i: (i, 0)), pl.BlockSpec((1, d), lambda i: (0, 0))],
        out_specs=pl.BlockSpec((tm, d), lambda i: (i, 0)),
        out_shape=jax.ShapeDtypeStruct((s, d), BF16),
        compiler_params=_params(("parallel",), [((tm, d), F32), ((tm, d), BF16)], temps=[((tm, d), F32)]),
        name="rmsnorm_cast",
    )(x, g)


def _cast_pair_kernel(a_ref, b_ref, ao_ref, bo_ref):
    ao_ref[...] = a_ref[...].astype(ao_ref.dtype)
    bo_ref[...] = b_ref[...].astype(bo_ref.dtype)


def _cast_pair(a, b, *, rows=256):
    k, n = a.shape
    assert b.shape == a.shape
    spec = pl.BlockSpec((rows, n), lambda i: (i, 0))
    return pl.pallas_call(
        _cast_pair_kernel,
        grid=(k // rows,),
        in_specs=[spec, spec],
        out_specs=[spec, spec],
        out_shape=[jax.ShapeDtypeStruct((k, n), BF16)] * 2,
        compiler_params=_params(("parallel",), [((rows, n), F32), ((rows, n), BF16)] * 2),
        name="cast_up_weights",
    )(a, b)


def _slab_spec(w, steps, nj):
    slab = w.shape[0] // steps
    assert slab * steps == w.shape[0] and slab % 16 == 0
    return (slab, w.shape[1]), pl.BlockSpec((slab, w.shape[1]), lambda i, j: (i * nj + j, 0))


def _matmul_kernel(a_ref, b_ref, *rest, b_is_nk):
    o_ref = rest[-2] if len(rest) == 3 else rest[0]
    contract_b = 1 if b_is_nk else 0
    o_ref[...] = lax.dot_general(a_ref[...], _load_bf16(b_ref), (((1,), (contract_b,)), ((), ())),
                                 preferred_element_type=F32).astype(o_ref.dtype)
    if len(rest) == 3:
        rest[2][...] = rest[0][...].astype(rest[2].dtype)


def _matmul(a, b, *, tm, tn, out_dtype, name, b_is_nk=False, n=None, side_cast=None):
    m, k = a.shape
    if n is None:
        n = b.shape[0] if b_is_nk else b.shape[1]
    if b_is_nk:
        b_block, b_spec = (tn, k), pl.BlockSpec((tn, k), lambda i, j: (j, 0))
    else:
        b_block, b_spec = (k, tn), pl.BlockSpec((k, tn), lambda i, j: (0, j))
    grid = (m // tm, n // tn)
    in_specs = [pl.BlockSpec((tm, k), lambda i, j: (i, 0)), b_spec]
    out_specs = pl.BlockSpec((tm, tn), lambda i, j: (i, j))
    out_shape = jax.ShapeDtypeStruct((m, n), out_dtype)
    blocks = [((tm, k), a.dtype), (b_block, b.dtype), ((tm, tn), out_dtype)]
    args = (a, b)
    if side_cast is not None:
        slab_block, slab_spec = _slab_spec(side_cast, grid[0] * grid[1], grid[1])
        in_specs.append(slab_spec)
        out_specs = [out_specs, slab_spec]
        out_shape = [out_shape, jax.ShapeDtypeStruct(side_cast.shape, BF16)]
        blocks += [(slab_block, F32), (slab_block, BF16)]
        args = (a, b, side_cast)
    return pl.pallas_call(
        functools.partial(_matmul_kernel, b_is_nk=b_is_nk),
        grid=grid, in_specs=in_specs, out_specs=out_specs, out_shape=out_shape,
        compiler_params=_params(("parallel", "parallel"), blocks, temps=[((tm, tn), F32), ((k, tn), BF16)]),
        name=name,
    )(*args)


def _pool_mixer_kernel(halo_ref, u_ref, w_ref, scale_ref, o_ref, ext_ref, lvl_a_ref, lvl_b_ref, *, tm, gw):
    i = pl.program_id(0)
    groups = len(POOL_WINDOWS)
    base = POOL_PAD + POOL_HALO
    n = POOL_HALO + tm
    width = ext_ref.shape[1]
    for ref in (ext_ref, lvl_a_ref, lvl_b_ref):
        ref[0:POOL_PAD, :] = jnp.zeros((POOL_PAD, width), F32)
    ext_ref[POOL_PAD:base, :] = jnp.where(i > 0, halo_ref[...], 0.0)
    ext_ref[base:base + tm, :] = u_ref[...]
    row = lax.broadcasted_iota(jnp.int32, (tm, 1), 0) + i * tm
    src = ext_ref
    for k, w in enumerate(POOL_WINDOWS):
        span = w // 2
        cols = slice(k * gw, (k + 1) * gw)
        u = u_ref[:, cols]
        win_sum = src[base:base + tm, cols] + src[base - span:base - span + tm, cols]
        inv_cnt = 1.0 / jnp.minimum(row + 1, w).astype(F32)
        pooled = win_sum * inv_cnt - u
        mixed = jnp.dot(pooled.astype(BF16), w_ref[k], preferred_element_type=F32)
        o_ref[:, cols] = (mixed * scale_ref[:, cols]).astype(o_ref.dtype)
        if k + 1 < groups:
            rest = slice((k + 1) * gw, width)
            dst = lvl_a_ref if k % 2 == 0 else lvl_b_ref
            dst[POOL_PAD:POOL_PAD + n, rest] = (src[POOL_PAD:POOL_PAD + n, rest]
                                                + src[POOL_PAD - span:POOL_PAD - span + n, rest])
            src = dst


def _pool_mixer(z, w_pool, pool_scale, *, pool_width, tm=512):
    s = z.shape[0]
    groups, gw, _ = w_pool.shape
    assert all(w == 2 ** (k + 1) for k, w in enumerate(POOL_WINDOWS)) and POOL_WINDOWS[-1] // 2 <= POOL_PAD
    halo_blocks = tm // POOL_HALO
    buf = (POOL_PAD + POOL_HALO + tm, pool_width)
    return pl.pallas_call(
        functools.partial(_pool_mixer_kernel, tm=tm, gw=gw),
        grid=(s // tm,),
        in_specs=[
            pl.BlockSpec((POOL_HALO, pool_width), lambda i: (jnp.maximum(i * halo_blocks - 1, 0), 0)),
            pl.BlockSpec((tm, pool_width), lambda i: (i, 0)),
            pl.BlockSpec((groups, gw, gw), lambda i: (0, 0, 0)),
            pl.BlockSpec((1, pool_width), lambda i: (0, 0)),
        ],
        out_specs=pl.BlockSpec((tm, pool_width), lambda i: (i, 0)),
        out_shape=jax.ShapeDtypeStruct((s, pool_width), BF16),
        scratch_shapes=[pltpu.VMEM(buf, F32)] * 3,
        compiler_params=_params(("parallel",),
                                [((tm, pool_width), F32), ((groups, gw, gw), BF16), ((tm, pool_width), BF16)],
                                scratch=[(buf, F32)] * 3,
                                temps=[((tm, pool_width), F32)]),
        name="pool_mixer",
    )(z, z, w_pool, pool_scale)


def _q_proj_kernel(ql_ref, g_ref, w_ref, pos_ref, freq_ref, o_ref, qn_ref, cos_ref, sin_ref, *, scale):
    @pl.when(pl.program_id(1) == 0)
    def _():
        qn_ref[...] = (_rms(ql_ref[...], g_ref[...]) * scale).astype(qn_ref.dtype)
        ang = freq_ref[...] * pos_ref[...]
        cos_ref[...] = jnp.cos(ang)
        sin_ref[...] = jnp.sin(ang)

    qt = lax.dot_general(w_ref[...], qn_ref[...], (((1,), (1,)), ((), ())), preferred_element_type=F32)
    half = QK_ROPE_DIM // 2
    r0, r1, r2 = QK_NOPE_DIM, QK_NOPE_DIM + half, QK_NOPE_DIM + QK_ROPE_DIM
    c, sn = cos_ref[...], sin_ref[...]
    rows = o_ref.shape[2]
    for hh in range(o_ref.shape[0]):
        q = qt[hh * rows:(hh + 1) * rows]
        x1, x2 = q[r0:r1], q[r1:r2]
        o_ref[hh, 0, 0:r0, :] = q[0:r0].astype(o_ref.dtype)
        o_ref[hh, 0, r0:r1, :] = (x1 * c - x2 * sn).astype(o_ref.dtype)
        o_ref[hh, 0, r1:r2, :] = (x1 * sn + x2 * c).astype(o_ref.dtype)
        o_ref[hh, 0, r2:, :] = q[r2:].astype(o_ref.dtype)


def _q_proj(z, q_norm, wq_t, pos_row, freq_col, *, col_block, rank, scale, tm, heads_per_step=8):
    s = z.shape[0]
    rows = V7X_MXU_DIM
    n_heads = wq_t.shape[0] // rows
    hps = heads_per_step
    half = QK_ROPE_DIM // 2
    return pl.pallas_call(
        functools.partial(_q_proj_kernel, scale=scale),
        grid=(s // tm, n_heads // hps),
        in_specs=[
            pl.BlockSpec((tm, rank), lambda i, h: (i, col_block)),
            pl.BlockSpec((1, rank), lambda i, h: (0, 0)),
            pl.BlockSpec((hps * rows, rank), lambda i, h: (h, 0)),
            pl.BlockSpec((1, tm), lambda i, h: (0, i)),
            pl.BlockSpec((half, 1), lambda i, h: (0, 0)),
        ],
        out_specs=pl.BlockSpec((hps, 1, rows, tm), lambda i, h: (h, i, 0, 0)),
        out_shape=jax.ShapeDtypeStruct((n_heads, s // tm, rows, tm), BF16),
        scratch_shapes=[pltpu.VMEM((tm, rank), BF16), pltpu.VMEM((half, tm), F32), pltpu.VMEM((half, tm), F32)],
        compiler_params=_params(("parallel", "arbitrary"),
                                [((tm, rank), F32), ((hps * rows, rank), BF16), ((hps * rows, tm), BF16)],
                                scratch=[((tm, rank), BF16), ((2 * half, tm), F32)],
                                temps=[((tm, rank), F32), ((hps * rows, tm), F32)]),
        name="q_proj",
    )(z, q_norm, wq_t, pos_row, freq_col)


def _kv_proj_kernel(kvl_ref, h_ref, wkr_ref, g_ref, wk_ref, wvt_ref, pos_ref, freq_ref, k_ref, vt_ref, kro_ref):
    kvn = _rms(kvl_ref[...], g_ref[...]).astype(BF16)
    k_ref[...] = jnp.dot(kvn, wk_ref[...], preferred_element_type=F32).astype(k_ref.dtype)
    vt = lax.dot_general(wvt_ref[...], kvn, (((1,), (1,)), ((), ())), preferred_element_type=F32)
    vt_ref[:, 0] = vt.reshape(vt_ref.shape[0], V_HEAD_DIM, vt.shape[1]).astype(vt_ref.dtype)
    half = QK_ROPE_DIM // 2
    wkr = _load_bf16(wkr_ref)
    wkr = jnp.concatenate([wkr, jnp.zeros((V7X_LANES - wkr.shape[0], wkr.shape[1]), BF16)], axis=0)
    x = lax.dot_general(h_ref[...], wkr, (((1,), (1,)), ((), ())), preferred_element_type=F32)
    ang = pos_ref[...] * freq_ref[...]
    lane = lax.broadcasted_iota(jnp.int32, x.shape, 1)
    x2_at_lo = pltpu.roll(x, V7X_LANES - half, axis=1)
    x1_at_hi = pltpu.roll(x, half, axis=1)
    partner = jnp.where(lane < half, -x2_at_lo, jnp.where(lane < QK_ROPE_DIM, x1_at_hi, 0.0))
    kro_ref[...] = (x * jnp.cos(ang) + partner * jnp.sin(ang)).astype(kro_ref.dtype)


def _kv_proj(z, h, w_in_t, kv_norm, wk, wv_t, pos_col, freq_row, *, lat_block, rope_row_block, rank, tm):
    s, d = h.shape
    nk = wk.shape[1]
    nv = wv_t.shape[0]
    n_heads = nv // V_HEAD_DIM
    return pl.pallas_call(
        _kv_proj_kernel,
        grid=(s // tm,),
        in_specs=[
            pl.BlockSpec((tm, rank), lambda i: (i, lat_block)),
            pl.BlockSpec((tm, d), lambda i: (i, 0)),
            pl.BlockSpec((QK_ROPE_DIM, d), lambda i: (rope_row_block, 0)),
            pl.BlockSpec((1, rank), lambda i: (0, 0)),
            pl.BlockSpec((rank, nk), lambda i: (0, 0)),
            pl.BlockSpec((nv, rank), lambda i: (0, 0)),
            pl.BlockSpec((tm, 1), lambda i: (i, 0)),
            pl.BlockSpec((1, V7X_LANES), lambda i: (0, 0)),
        ],
        out_specs=[
            pl.BlockSpec((tm, nk), lambda i: (i, 0)),
            pl.BlockSpec((n_heads, 1, V_HEAD_DIM, tm), lambda i: (0, i, 0, 0)),
            pl.BlockSpec((tm, V7X_LANES), lambda i: (i, 0)),
        ],
        out_shape=[
            jax.ShapeDtypeStruct((s, nk), BF16),
            jax.ShapeDtypeStruct((n_heads, s // tm, V_HEAD_DIM, tm), BF16),
            jax.ShapeDtypeStruct((s, V7X_LANES), BF16),
        ],
        compiler_params=_params(("parallel",),
                                [((tm, rank), F32), ((tm, d), BF16), ((QK_ROPE_DIM, d), w_in_t.dtype), ((rank, nk), BF16),
                                 ((nv, rank), BF16), ((tm, nk), BF16), ((nv, tm), BF16), ((tm, V7X_LANES), BF16),
                                 ((tm, V7X_LANES), F32)],
                                temps=[((tm, nk), F32), ((nv, tm), F32), ((V7X_LANES, d), BF16)]),
        name="kv_proj",
    )(z, h, w_in_t, kv_norm, wk, wv_t, pos_col, freq_row)


def _flash_kernel(kn_ref, kr_ref, qt_ref, vt_ref, o_ref, s0_ref, s1_ref, s2_ref, bias_ref, smax0_ref, smax1_ref,
                  smax2_ref, m_ref, l_ref, acc_ref, *, t, unroll):
    nq = qt_ref.shape[1]
    ring = ((s0_ref, smax0_ref), (s1_ref, smax1_ref))
    first = (s2_ref, smax2_ref)
    kc = lax.broadcasted_iota(jnp.int32, (t, t), 0) // CHUNK
    qc = lax.broadcasted_iota(jnp.int32, (t, t), 1) // CHUNK
    bias_ref[...] = jnp.where(kc <= qc, 0.0, -jnp.inf).astype(F32)

    def scores(qi, kb, buf):
        s_ref, smax_ref = buf
        ks = pl.ds(pl.multiple_of(kb * t, t), t)
        kcat = jnp.concatenate([kn_ref[ks, :], kr_ref[ks, :]], axis=1)
        s = jnp.dot(kcat, qt_ref[0, qi], preferred_element_type=F32)
        s_ref[...] = s
        smax_ref[...] = jnp.max(s, axis=0, keepdims=True)

    def softmax_pv(kb, buf, masked):
        s_ref, smax_ref = buf
        s = s_ref[...]
        if masked:
            s = s + bias_ref[...]
            block_max = jnp.max(s, axis=0, keepdims=True)
        else:
            block_max = smax_ref[...]
        m_prev = m_ref[...]
        m_new = jnp.maximum(m_prev, block_max)
        alpha = jnp.exp2(m_prev - m_new)
        p = jnp.exp2(s - m_new)
        l_ref[...] = alpha * l_ref[...] + jnp.sum(p, axis=0, keepdims=True)
        pv = jnp.dot(vt_ref[0, kb], p.astype(BF16), preferred_element_type=F32)
        acc_ref[...] = alpha * acc_ref[...] + pv
        m_ref[...] = m_new

    def query_block(qi, carry):
        m_ref[...] = jnp.full(m_ref.shape, -jnp.inf, F32)
        l_ref[...] = jnp.zeros(l_ref.shape, F32)
        acc_ref[...] = jnp.zeros(acc_ref.shape, F32)
        nxt = jnp.minimum(qi + 1, nq - 1)

        @pl.when(qi == 0)
        def _():
            softmax_pv(0, first, masked=True)
            scores(nxt, 0, first)

        @pl.when(qi > 0)
        def _():
            scores(qi, 1, ring[1])
            softmax_pv(0, first, masked=False)
            n_groups = (qi - 1) // unroll

            def group(g, c):
                b0 = 1 + g * unroll
                for u in range(unroll):
                    scores(qi, b0 + u + 1, ring[u % 2])
                    softmax_pv(b0 + u, ring[(u + 1) % 2], masked=False)
                return c

            lax.fori_loop(0, n_groups, group, 0)
            base = 1 + n_groups * unroll
            rest = qi - base
            for k in range(0, unroll - 2, 2):
                @pl.when(rest >= k + 2)
                def _(k=k):
                    for u in range(2):
                        scores(qi, base + k + u + 1, ring[u % 2])
                        softmax_pv(base + k + u, ring[(u + 1) % 2], masked=False)

            @pl.when(rest % 2 == 1)
            def _():
                scores(qi, base + rest, ring[0])
                softmax_pv(base + rest - 1, ring[1], masked=False)

            for parity in (0, 1):
                @pl.when(rest % 2 == parity)
                def _(parity=parity):
                    scores(nxt, 0, first)
                    softmax_pv(qi, ring[(1 + parity) % 2], masked=True)

        rows = pl.ds(pl.multiple_of(qi * t, t), t)
        o_ref[rows, :] = (acc_ref[...] * (1.0 / l_ref[...])).T.astype(o_ref.dtype)
        return carry

    scores(0, 0, first)
    lax.fori_loop(0, nq, query_block, 0)


def _flash_attention(k_nope, k_rope, q_t, v_t, *, n_heads, t, unroll=4):
    s = k_nope.shape[0]
    nq = s // t
    assert t % CHUNK == 0 and unroll % 2 == 0 and nq >= 2
    assert v_t.shape == (n_heads, nq, V_HEAD_DIM, t) and q_t.shape == (n_heads, nq, V7X_MXU_DIM, t)
    return pl.pallas_call(
        functools.partial(_flash_kernel, t=t, unroll=unroll),
        grid=(n_heads,),
        in_specs=[
            pl.BlockSpec((s, QK_NOPE_DIM), lambda h: (0, h)),
            pl.BlockSpec((s, V7X_LANES), lambda h: (0, 0)),
            pl.BlockSpec((1, nq, V7X_MXU_DIM, t), lambda h: (h, 0, 0, 0)),
            pl.BlockSpec((1, nq, V_HEAD_DIM, t), lambda h: (h, 0, 0, 0)),
        ],
        out_specs=pl.BlockSpec((s, V_HEAD_DIM), lambda h: (0, h)),
        out_shape=jax.ShapeDtypeStruct((s, n_heads * V_HEAD_DIM), BF16),
        scratch_shapes=[pltpu.VMEM((t, t), F32)] * 4 + [pltpu.VMEM((1, t), F32)] * 5 + [
            pltpu.VMEM((V_HEAD_DIM, t), F32)],
        compiler_params=_params(("arbitrary",),
                                [((s, QK_NOPE_DIM), BF16), ((s, V7X_LANES), BF16), ((V7X_MXU_DIM, s), BF16),
                                 ((V_HEAD_DIM, s), BF16), ((s, V_HEAD_DIM), BF16)],
                                scratch=[((4 * t + V_HEAD_DIM + 16, t), F32)],
                                temps=[((t, t), F32), ((t, t), F32), ((t, t), BF16), ((t, V7X_MXU_DIM), BF16)]),
        name="flash_attention",
    )(k_nope, k_rope, q_t, v_t)


def _gated_merge_kernel(pm_ref, at_ref, h_ref, wup_ref, wum_ref, wga_ref, wgb_ref, wo_ref, o_ref, wo_o_ref):
    h = h_ref[...]
    ya = jnp.dot(pm_ref[...], wup_ref[...], preferred_element_type=F32)
    yb = jnp.dot(at_ref[...], wum_ref[...], preferred_element_type=F32)
    ga = _sigmoid(jnp.dot(h, wga_ref[...], preferred_element_type=F32))
    gb = _sigmoid(jnp.dot(h, wgb_ref[...], preferred_element_type=F32))
    o_ref[...] = (ga * ya + gb * yb).astype(o_ref.dtype)
    wo_o_ref[...] = wo_ref[...].astype(wo_o_ref.dtype)


def _gated_merge(pm, attn, h, w_up_pool, w_up_mla, w_gate2d, w_out, *, tm=1024, tn=256):
    s, d = h.shape
    kp, km = pm.shape[1], attn.shape[1]
    nb = d // tn
    slab_block, slab_spec = _slab_spec(w_out, (s // tm) * nb, nb)
    return pl.pallas_call(
        _gated_merge_kernel,
        grid=(s // tm, nb),
        in_specs=[
            pl.BlockSpec((tm, kp), lambda i, j: (i, 0)),
            pl.BlockSpec((tm, km), lambda i, j: (i, 0)),
            pl.BlockSpec((tm, d), lambda i, j: (i, 0)),
            pl.BlockSpec((kp, tn), lambda i, j: (0, j)),
            pl.BlockSpec((km, tn), lambda i, j: (0, j)),
            pl.BlockSpec((d, tn), lambda i, j: (0, j)),
            pl.BlockSpec((d, tn), lambda i, j: (0, j + nb)),
            slab_spec,
        ],
        out_specs=[pl.BlockSpec((tm, tn), lambda i, j: (i, j)), slab_spec],
        out_shape=[jax.ShapeDtypeStruct((s, d), BF16), jax.ShapeDtypeStruct(w_out.shape, BF16)],
        compiler_params=_params(("parallel", "parallel"),
                                [((tm, kp), BF16), ((tm, km), BF16), ((tm, d), BF16), ((kp, tn), BF16),
                                 ((km, tn), BF16), ((d, tn), BF16), ((d, tn), BF16), ((tm, tn), BF16),
                                 (slab_block, F32), (slab_block, BF16)],
                                temps=[((tm, tn), F32)] * 4),
        name="gated_merge",
    )(pm, attn, h, w_up_pool, w_up_mla, w_gate2d, w_gate2d, w_out)


def _residual_norm_kernel(t_ref, x_ref, gpost_ref, gnext_ref, xo_ref, ho_ref):
    xo = x_ref[...] + _rms(t_ref[...].astype(F32), gpost_ref[...])
    xo_ref[...] = xo
    ho_ref[...] = _rms(xo, gnext_ref[...]).astype(ho_ref.dtype)


def _residual_final_kernel(t_ref, x_ref, gpost_ref, xo_ref):
    xo_ref[...] = x_ref[...] + _rms(t_ref[...].astype(F32), gpost_ref[...])


def _residual_norm(t, x, g_post, g_next=None, *, tm=256):
    s, d = x.shape
    row = pl.BlockSpec((tm, d), lambda i: (i, 0))
    vec = pl.BlockSpec((1, d), lambda i: (0, 0))
    blocks = [((tm, d), t.dtype), ((tm, d), F32), ((tm, d), F32)]
    if g_next is None:
        return pl.pallas_call(
            _residual_final_kernel, grid=(s // tm,), in_specs=[row, row, vec], out_specs=row,
            out_shape=jax.ShapeDtypeStruct((s, d), F32),
            compiler_params=_params(("parallel",), blocks, temps=[((tm, d), F32)] * 2),
            name="residual_final",
        )(t, x, g_post)
    return pl.pallas_call(
        _residual_norm_kernel, grid=(s // tm,), in_specs=[row, row, vec, vec], out_specs=[row, row],
        out_shape=[jax.ShapeDtypeStruct((s, d), F32), jax.ShapeDtypeStruct((s, d), BF16)],
        compiler_params=_params(("parallel",), blocks + [((tm, d), BF16)], temps=[((tm, d), F32)] * 2),
        name="residual_norm",
    )(t, x, g_post, g_next)


def _swiglu_up_kernel(h_ref, wg_ref, wu_ref, wd_ref, o_ref, wd_o_ref):
    h = h_ref[...]
    gate = jnp.dot(h, _load_bf16(wg_ref), preferred_element_type=F32)
    up = jnp.dot(h, _load_bf16(wu_ref), preferred_element_type=F32)
    o_ref[...] = (gate * _sigmoid(gate) * up).astype(o_ref.dtype)
    wd_o_ref[...] = wd_ref[...].astype(wd_o_ref.dtype)


def _swiglu_up(h, w_gate, w_up, w_down, *, tm=1024, tn=256):
    s, d = h.shape
    f = w_gate.shape[1]
    nj = f // tn
    steps = (s // tm) * nj
    slab = w_down.shape[0] // steps
    assert slab * steps == w_down.shape[0] and slab % 16 == 0
    dn = w_down.shape[1]
    return pl.pallas_call(
        _swiglu_up_kernel,
        grid=(s // tm, nj),
        in_specs=[
            pl.BlockSpec((tm, d), lambda i, j: (i, 0)),
            pl.BlockSpec((d, tn), lambda i, j: (0, j)),
            pl.BlockSpec((d, tn), lambda i, j: (0, j)),
            pl.BlockSpec((slab, dn), lambda i, j: (i * nj + j, 0)),
        ],
        out_specs=[pl.BlockSpec((tm, tn), lambda i, j: (i, j)),
                   pl.BlockSpec((slab, dn), lambda i, j: (i * nj + j, 0))],
        out_shape=[jax.ShapeDtypeStruct((s, f), BF16), jax.ShapeDtypeStruct(w_down.shape, BF16)],
        compiler_params=_params(("parallel", "parallel"),
                                [((tm, d), BF16), ((d, tn), w_gate.dtype), ((d, tn), w_up.dtype), ((tm, tn), BF16),
                                 ((slab, dn), w_down.dtype), ((slab, dn), BF16)],
                                temps=[((tm, tn), F32)] * 3 + [((d, tn), BF16)] * 2),
        name="swiglu_up",
    )(h, w_gate, w_up, w_down)


def _ple_gate_kernel(h_ref, p_ref, wg_ref, wp_ref, o_ref):
    gate = _sigmoid(jnp.dot(h_ref[...], _load_bf16(wg_ref), preferred_element_type=F32))
    pe = jnp.dot(_load_bf16(p_ref), _load_bf16(wp_ref), preferred_element_type=F32)
    o_ref[...] = (pe * gate).astype(o_ref.dtype)


def _ple_gate(h, p, w_gate, w_proj, *, tm=1024, tn=512):
    s, d = h.shape
    r = p.shape[1]
    return pl.pallas_call(
        _ple_gate_kernel,
        grid=(s // tm, d // tn),
        in_specs=[
            pl.BlockSpec((tm, d), lambda i, j: (i, 0)),
            pl.BlockSpec((tm, r), lambda i, j: (i, 0)),
            pl.BlockSpec((d, tn), lambda i, j: (0, j)),
            pl.BlockSpec((r, tn), lambda i, j: (0, j)),
        ],
        out_specs=pl.BlockSpec((tm, tn), lambda i, j: (i, j)),
        out_shape=jax.ShapeDtypeStruct((s, d), BF16),
        compiler_params=_params(("parallel", "parallel"),
                                [((tm, d), BF16), ((tm, r), p.dtype), ((d, tn), w_gate.dtype), ((r, tn), w_proj.dtype),
                                 ((tm, tn), BF16)],
                                temps=[((tm, tn), F32)] * 2 + [((d, tn), BF16)]),
        name="ple_gate",
    )(h, p, w_gate, w_proj)


def kernel(x, p, positions, norm_mix_pre, norm_mix_post, w_in, q_norm, kv_norm, w_q_b, w_kv_b, w_pool, pool_scale,
           w_up_pool, w_up_mla, w_branch_gate, w_out, norm_ffn_pre, norm_ffn_post, w_ffn_gate, w_ffn_up,
           w_ffn_down, norm_ple_pre, w_ple_gate, w_ple_proj, norm_ple_post):
    batch, seq, d_model = x.shape
    depth = w_in.shape[0]
    in_width = w_in.shape[2]
    q_rank, n_heads, qk_dim = w_q_b.shape[1:]
    kv_rank = w_kv_b.shape[1]
    pool_width = w_up_pool.shape[1]
    assert batch == 1 and qk_dim == QK_NOPE_DIM + QK_ROPE_DIM
    assert w_kv_b.shape[3] == QK_NOPE_DIM + V_HEAD_DIM
    o_q, o_kv, o_kr = pool_width, pool_width + q_rank, pool_width + q_rank + kv_rank
    assert o_kr + QK_ROPE_DIM == in_width
    assert o_q % q_rank == 0 and o_kv % kv_rank == 0 and o_kr % QK_ROPE_DIM == 0
    z_tn = 512
    assert o_kr % z_tn == 0
    attn_t = 512
    half = QK_ROPE_DIM // 2
    scale = float(qk_dim) ** -0.5 * LOG2_E

    inv_freq = ROPE_THETA ** (-jnp.arange(0, QK_ROPE_DIM, 2, dtype=F32) / QK_ROPE_DIM)
    freq_col = inv_freq.reshape(half, 1)
    freq_row = jnp.zeros((1, V7X_LANES), F32).at[0, :half].set(inv_freq).at[0, half:QK_ROPE_DIM].set(inv_freq)
    pos_f = positions.astype(F32)
    pos_row, pos_col = pos_f.reshape(1, seq), pos_f.reshape(seq, 1)

    xs = x.reshape(seq, d_model)
    for i in range(depth):
        w_in_t = jnp.transpose(w_in[i])
        wq = w_q_b[i].astype(BF16)
        wq_t = jnp.pad(jnp.transpose(wq, (1, 2, 0)), ((0, 0), (0, V7X_MXU_DIM - qk_dim), (0, 0)))
        wq_t = wq_t.reshape(n_heads * V7X_MXU_DIM, q_rank)
        wkv = w_kv_b[i].astype(BF16)
        wk = wkv[:, :, :QK_NOPE_DIM].reshape(kv_rank, n_heads * QK_NOPE_DIM)
        wv_t = jnp.transpose(wkv[:, :, QK_NOPE_DIM:], (1, 2, 0)).reshape(n_heads * V_HEAD_DIM, kv_rank)
        w_up_p, w_up_m = _cast_pair(w_up_pool[i], w_up_mla[i])

        def row(v):
            return v[i].reshape(1, -1)

        h = _rmsnorm_cast(xs, row(norm_mix_pre))
        z = _matmul(h, w_in_t, tm=1024, tn=z_tn, out_dtype=F32, name="in_proj", b_is_nk=True, n=o_kr)
        pm = _pool_mixer(z, w_pool[i].astype(BF16), row(pool_scale), pool_width=pool_width)
        q_t = _q_proj(z, row(q_norm), wq_t, pos_row, freq_col, col_block=o_q // q_rank, rank=q_rank, scale=scale,
                      tm=attn_t)
        k_nope, v_t, k_rope = _kv_proj(z, h, w_in_t, row(kv_norm), wk, wv_t, pos_col, freq_row,
                                       lat_block=o_kv // kv_rank, rope_row_block=o_kr // QK_ROPE_DIM, rank=kv_rank,
                                       tm=attn_t)
        attn = _flash_attention(k_nope, k_rope, q_t, v_t, n_heads=n_heads, t=attn_t)
        w_gate2d = w_branch_gate[i].astype(BF16).reshape(d_model, 2 * d_model)
        merged, w_out_b = _gated_merge(pm, attn, h, w_up_p, w_up_m, w_gate2d, w_out[i])
        mix = _matmul(merged, w_out_b, tm=1024, tn=1024, out_dtype=BF16, name="out_proj")
        xs, h2 = _residual_norm(mix, xs, row(norm_mix_post), row(norm_ffn_pre))
        act, w_down = _swiglu_up(h2, w_ffn_gate[i], w_ffn_up[i], w_ffn_down[i])
        ffn, w_pg = _matmul(act, w_down, tm=512, tn=512, out_dtype=BF16, name="ffn_down", side_cast=w_ple_gate[i])
        xs, h3 = _residual_norm(ffn, xs, row(norm_ffn_post), row(norm_ple_pre))
        t = _ple_gate(h3, p[i].reshape(seq, -1), w_pg, w_ple_proj[i], tn=1024)
        xs = _residual_norm(t, xs, row(norm_ple_post))
    return xs.reshape(batch, seq, d_model)
```

```python
import functools

import jax
import jax.numpy as jnp
from jax import lax
from jax.experimental import pallas as pl
from jax.experimental.pallas import tpu as pltpu

CHUNK = 64
EPS = 1e-6
POOL_WINDOWS = (2, 4, 8, 16)
QK_NOPE_DIM = 128
QK_ROPE_DIM = 64
V_HEAD_DIM = 128
ROPE_THETA = 10000.0
LOG2_E = 1.4426950408889634

V7X_LANES = 128
V7X_MXU_DIM = 256
V7X_VMEM_BYTES = 64 * 1024 * 1024
V7X_VMEM_USABLE_BYTES = 60000 * 1024
V7X_VMEM_DEFAULT_SCOPED_BYTES = 32 * 1024 * 1024

POOL_HALO = 16
POOL_PAD = 8
F32 = jnp.float32
BF16 = jnp.bfloat16


def _nbytes(shape, dtype):
    n = 1
    for s in shape:
        n *= s
    return n * jnp.dtype(dtype).itemsize


def _params(semantics, blocks, scratch=(), temps=()):
    need = 2 * sum(_nbytes(s, d) for s, d in blocks)
    need += sum(_nbytes(s, d) for s, d in scratch) + sum(_nbytes(s, d) for s, d in temps)
    need = max(int(need * 1.25) + (2 << 20), V7X_VMEM_DEFAULT_SCOPED_BYTES)
    return pltpu.CompilerParams(dimension_semantics=semantics,
                                vmem_limit_bytes=min(need, V7X_VMEM_USABLE_BYTES))


def _load_bf16(ref):
    v = ref[...]
    return v if v.dtype == BF16 else v.astype(BF16)


def _sigmoid(x):
    return 0.5 * jnp.tanh(0.5 * x) + 0.5


def _rms(xf, g):
    return xf * lax.rsqrt(jnp.mean(xf * xf, axis=-1, keepdims=True) + EPS) * g


def _rmsnorm_cast_kernel(x_ref, g_ref, o_ref):
    o_ref[...] = _rms(x_ref[...], g_ref[...]).astype(o_ref.dtype)


def _rmsnorm_cast(x, g, *, tm=512):
    s, d = x.shape
    return pl.pallas_call(
        _rmsnorm_cast_kernel,
        grid=(s // tm,),
        in_specs=[pl.BlockSpec((tm, d), lambda i: (i, 0)), pl.BlockSpec((1, d), lambda i: (0, 0))],
        out_specs=pl.BlockSpec((tm, d), lambda i: (i, 0)),
        out_shape=jax.ShapeDtypeStruct((s, d), BF16),
        compiler_params=_params(("parallel",), [((tm, d), F32), ((tm, d), BF16)], temps=[((tm, d), F32)]),
        name="rmsnorm_cast",
    )(x, g)


def _cast_pair_kernel(a_ref, b_ref, ao_ref, bo_ref):
    ao_ref[...] = a_ref[...].astype(ao_ref.dtype)
    bo_ref[...] = b_ref[...].astype(bo_ref.dtype)


def _cast_pair(a, b, *, rows=256):
    k, n = a.shape
    assert b.shape == a.shape
    spec = pl.BlockSpec((rows, n), lambda i: (i, 0))
    return pl.pallas_call(
        _cast_pair_kernel,
        grid=(k // rows,),
        in_specs=[spec, spec],
        out_specs=[spec, spec],
        out_shape=[jax.ShapeDtypeStruct((k, n), BF16)] * 2,
        compiler_params=_params(("parallel",), [((rows, n), F32), ((rows, n), BF16)] * 2),
        name="cast_up_weights",
    )(a, b)


def _slab_spec(w, steps, nj):
    slab = w.shape[0] // steps
    assert slab * steps == w.shape[0] and slab % 16 == 0
    return (slab, w.shape[1]), pl.BlockSpec((slab, w.shape[1]), lambda i, j: (i * nj + j, 0))


def _matmul_kernel(a_ref, b_ref, *rest, b_is_nk):
    o_ref = rest[-2] if len(rest) == 3 else rest[0]
    contract_b = 1 if b_is_nk else 0
    o_ref[...] = lax.dot_general(a_ref[...], _load_bf16(b_ref), (((1,), (contract_b,)), ((), ())),
                                 preferred_element_type=F32).astype(o_ref.dtype)
    if len(rest) == 3:
        rest[2][...] = rest[0][...].astype(rest[2].dtype)


def _matmul(a, b, *, tm, tn, out_dtype, name, b_is_nk=False, n=None, side_cast=None):
    m, k = a.shape
    if n is None:
        n = b.shape[0] if b_is_nk else b.shape[1]
    if b_is_nk:
        b_block, b_spec = (tn, k), pl.BlockSpec((tn, k), lambda i, j: (j, 0))
    else:
        b_block, b_spec = (k, tn), pl.BlockSpec((k, tn), lambda i, j: (0, j))
    grid = (m // tm, n // tn)
    in_specs = [pl.BlockSpec((tm, k), lambda i, j: (i, 0)), b_spec]
    out_specs = pl.BlockSpec((tm, tn), lambda i, j: (i, j))
    out_shape = jax.ShapeDtypeStruct((m, n), out_dtype)
    blocks = [((tm, k), a.dtype), (b_block, b.dtype), ((tm, tn), out_dtype)]
    args = (a, b)
    if side_cast is not None:
        slab_block, slab_spec = _slab_spec(side_cast, grid[0] * grid[1], grid[1])
        in_specs.append(slab_spec)
        out_specs = [out_specs, slab_spec]
        out_shape = [out_shape, jax.ShapeDtypeStruct(side_cast.shape, BF16)]
        blocks += [(slab_block, F32), (slab_block, BF16)]
        args = (a, b, side_cast)
    return pl.pallas_call(
        functools.partial(_matmul_kernel, b_is_nk=b_is_nk),
        grid=grid, in_specs=in_specs, out_specs=out_specs, out_shape=out_shape,
        compiler_params=_params(("parallel", "parallel"), blocks, temps=[((tm, tn), F32), ((k, tn), BF16)]),
        name=name,
    )(*args)


def _pool_mixer_kernel(halo_ref, u_ref, w_ref, scale_ref, o_ref, ext_ref, lvl_a_ref, lvl_b_ref, *, tm, gw):
    i = pl.program_id(0)
    groups = len(POOL_WINDOWS)
    base = POOL_PAD + POOL_HALO
    n = POOL_HALO + tm
    width = ext_ref.shape[1]
    for ref in (ext_ref, lvl_a_ref, lvl_b_ref):
        ref[0:POOL_PAD, :] = jnp.zeros((POOL_PAD, width), F32)
    ext_ref[POOL_PAD:base, :] = jnp.where(i > 0, halo_ref[...], 0.0)
    ext_ref[base:base + tm, :] = u_ref[...]
    row = lax.broadcasted_iota(jnp.int32, (tm, 1), 0) + i * tm
    src = ext_ref
    for k, w in enumerate(POOL_WINDOWS):
        span = w // 2
        cols = slice(k * gw, (k + 1) * gw)
        u = u_ref[:, cols]
        win_sum = src[base:base + tm, cols] + src[base - span:base - span + tm, cols]
        inv_cnt = 1.0 / jnp.minimum(row + 1, w).astype(F32)
        pooled = win_sum * inv_cnt - u
        mixed = jnp.dot(pooled.astype(BF16), w_ref[k], preferred_element_type=F32)
        o_ref[:, cols] = (mixed * scale_ref[:, cols]).astype(o_ref.dtype)
        if k + 1 < groups:
            rest = slice((k + 1) * gw, width)
            dst = lvl_a_ref if k % 2 == 0 else lvl_b_ref
            dst[POOL_PAD:POOL_PAD + n, rest] = (src[POOL_PAD:POOL_PAD + n, rest]
                                                + src[POOL_PAD - span:POOL_PAD - span + n, rest])
            src = dst


def _pool_mixer(z, w_pool, pool_scale, *, pool_width, tm=512):
    s = z.shape[0]
    groups, gw, _ = w_pool.shape
    assert all(w == 2 ** (k + 1) for k, w in enumerate(POOL_WINDOWS)) and POOL_WINDOWS[-1] // 2 <= POOL_PAD
    halo_blocks = tm // POOL_HALO
    buf = (POOL_PAD + POOL_HALO + tm, pool_width)
    return pl.pallas_call(
        functools.partial(_pool_mixer_kernel, tm=tm, gw=gw),
        grid=(s // tm,),
        in_specs=[
            pl.BlockSpec((POOL_HALO, pool_width), lambda i: (jnp.maximum(i * halo_blocks - 1, 0), 0)),
            pl.BlockSpec((tm, pool_width), lambda i: (i, 0)),
            pl.BlockSpec((groups, gw, gw), lambda i: (0, 0, 0)),
            pl.BlockSpec((1, pool_width), lambda i: (0, 0)),
        ],
        out_specs=pl.BlockSpec((tm, pool_width), lambda i: (i, 0)),
        out_shape=jax.ShapeDtypeStruct((s, pool_width), BF16),
        scratch_shapes=[pltpu.VMEM(buf, F32)] * 3,
        compiler_params=_params(("parallel",),
                                [((tm, pool_width), F32), ((groups, gw, gw), BF16), ((tm, pool_width), BF16)],
                                scratch=[(buf, F32)] * 3,
                                temps=[((tm, pool_width), F32)]),
        name="pool_mixer",
    )(z, z, w_pool, pool_scale)


def _q_proj_kernel(ql_ref, g_ref, w_ref, pos_ref, freq_ref, o_ref, qn_ref, cos_ref, sin_ref, *, scale):
    @pl.when(pl.program_id(1) == 0)
    def _():
        qn_ref[...] = (_rms(ql_ref[...], g_ref[...]) * scale).astype(qn_ref.dtype)
        ang = freq_ref[...] * pos_ref[...]
        cos_ref[...] = jnp.cos(ang)
        sin_ref[...] = jnp.sin(ang)

    qt = lax.dot_general(w_ref[...], qn_ref[...], (((1,), (1,)), ((), ())), preferred_element_type=F32)
    half = QK_ROPE_DIM // 2
    r0, r1, r2 = QK_NOPE_DIM, QK_NOPE_DIM + half, QK_NOPE_DIM + QK_ROPE_DIM
    c, sn = cos_ref[...], sin_ref[...]
    rows = o_ref.shape[2]
    for hh in range(o_ref.shape[0]):
        q = qt[hh * rows:(hh + 1) * rows]
        x1, x2 = q[r0:r1], q[r1:r2]
        o_ref[hh, 0, 0:r0, :] = q[0:r0].astype(o_ref.dtype)
        o_ref[hh, 0, r0:r1, :] = (x1 * c - x2 * sn).astype(o_ref.dtype)
        o_ref[hh, 0, r1:r2, :] = (x1 * sn + x2 * c).astype(o_ref.dtype)
        o_ref[hh, 0, r2:, :] = q[r2:].astype(o_ref.dtype)


def _q_proj(z, q_norm, wq_t, pos_row, freq_col, *, col_block, rank, scale, tm, heads_per_step=8):
    s = z.shape[0]
    rows = V7X_MXU_DIM
    n_heads = wq_t.shape[0] // rows
    hps = heads_per_step
    half = QK_ROPE_DIM // 2
    return pl.pallas_call(
        functools.partial(_q_proj_kernel, scale=scale),
        grid=(s // tm, n_heads // hps),
        in_specs=[
            pl.BlockSpec((tm, rank), lambda i, h: (i, col_block)),
            pl.BlockSpec((1, rank), lambda i, h: (0, 0)),
            pl.BlockSpec((hps * rows, rank), lambda i, h: (h, 0)),
            pl.BlockSpec((1, tm), lambda i, h: (0, i)),
            pl.BlockSpec((half, 1), lambda i, h: (0, 0)),
        ],
        out_specs=pl.BlockSpec((hps, 1, rows, tm), lambda i, h: (h, i, 0, 0)),
        out_shape=jax.ShapeDtypeStruct((n_heads, s // tm, rows, tm), BF16),
        scratch_shapes=[pltpu.VMEM((tm, rank), BF16), pltpu.VMEM((half, tm), F32), pltpu.VMEM((half, tm), F32)],
        compiler_params=_params(("parallel", "arbitrary"),
                                [((tm, rank), F32), ((hps * rows, rank), BF16), ((hps * rows, tm), BF16)],
                                scratch=[((tm, rank), BF16), ((2 * half, tm), F32)],
                                temps=[((tm, rank), F32), ((hps * rows, tm), F32)]),
        name="q_proj",
    )(z, q_norm, wq_t, pos_row, freq_col)


def _kv_proj_kernel(kvl_ref, h_ref, wkr_ref, g_ref, wk_ref, wvt_ref, pos_ref, freq_ref, k_ref, vt_ref, kro_ref):
    kvn = _rms(kvl_ref[...], g_ref[...]).astype(BF16)
    k_ref[...] = jnp.dot(kvn, wk_ref[...], preferred_element_type=F32).astype(k_ref.dtype)
    vt = lax.dot_general(wvt_ref[...], kvn, (((1,), (1,)), ((), ())), preferred_element_type=F32)
    vt_ref[:, 0] = vt.reshape(vt_ref.shape[0], V_HEAD_DIM, vt.shape[1]).astype(vt_ref.dtype)
    half = QK_ROPE_DIM // 2
    wkr = _load_bf16(wkr_ref)
    wkr = jnp.concatenate([wkr, jnp.zeros((V7X_LANES - wkr.shape[0], wkr.shape[1]), BF16)], axis=0)
    x = lax.dot_general(h_ref[...], wkr, (((1,), (1,)), ((), ())), preferred_element_type=F32)
    ang = pos_ref[...] * freq_ref[...]
    lane = lax.broadcasted_iota(jnp.int32, x.shape, 1)
    x2_at_lo = pltpu.roll(x, V7X_LANES - half, axis=1)
    x1_at_hi = pltpu.roll(x, half, axis=1)
    partner = jnp.where(lane < half, -x2_at_lo, jnp.where(lane < QK_ROPE_DIM, x1_at_hi, 0.0))
    kro_ref[...] = (x * jnp.cos(ang) + partner * jnp.sin(ang)).astype(kro_ref.dtype)


def _kv_proj(z, h, w_in_t, kv_norm, wk, wv_t, pos_col, freq_row, *, lat_block, rope_row_block, rank, tm):
    s, d = h.shape
    nk = wk.shape[1]
    nv = wv_t.shape[0]
    n_heads = nv // V_HEAD_DIM
    return pl.pallas_call(
        _kv_proj_kernel,
        grid=(s // tm,),
        in_specs=[
            pl.BlockSpec((tm, rank), lambda i: (i, lat_block)),
            pl.BlockSpec((tm, d), lambda i: (i, 0)),
            pl.BlockSpec((QK_ROPE_DIM, d), lambda i: (rope_row_block, 0)),
            pl.BlockSpec((1, rank), lambda i: (0, 0)),
            pl.BlockSpec((rank, nk), lambda i: (0, 0)),
            pl.BlockSpec((nv, rank), lambda i: (0, 0)),
            pl.BlockSpec((tm, 1), lambda i: (i, 0)),
            pl.BlockSpec((1, V7X_LANES), lambda i: (0, 0)),
        ],
        out_specs=[
            pl.BlockSpec((tm, nk), lambda i: (i, 0)),
            pl.BlockSpec((n_heads, 1, V_HEAD_DIM, tm), lambda i: (0, i, 0, 0)),
            pl.BlockSpec((tm, V7X_LANES), lambda i: (i, 0)),
        ],
        out_shape=[
            jax.ShapeDtypeStruct((s, nk), BF16),
            jax.ShapeDtypeStruct((n_heads, s // tm, V_HEAD_DIM, tm), BF16),
            jax.ShapeDtypeStruct((s, V7X_LANES), BF16),
        ],
        compiler_params=_params(("parallel",),
                                [((tm, rank), F32), ((tm, d), BF16), ((QK_ROPE_DIM, d), w_in_t.dtype), ((rank, nk), BF16),
                                 ((nv, rank), BF16), ((tm, nk), BF16), ((nv, tm), BF16), ((tm, V7X_LANES), BF16),
                                 ((tm, V7X_LANES), F32)],
                                temps=[((tm, nk), F32), ((nv, tm), F32), ((V7X_LANES, d), BF16)]),
        name="kv_proj",
    )(z, h, w_in_t, kv_norm, wk, wv_t, pos_col, freq_row)


def _flash_kernel(kn_ref, kr_ref, qt_ref, vt_ref, o_ref, s_ref, smax_ref, bias_ref, m_ref, l_ref, acc_ref,
                  *, t, unroll):
    heads, nq = qt_ref.shape[0], qt_ref.shape[1]
    first = 2
    kc = lax.broadcasted_iota(jnp.int32, (t, t), 0) // CHUNK
    qc = lax.broadcasted_iota(jnp.int32, (t, t), 1) // CHUNK
    bias_ref[...] = jnp.where(kc <= qc, 0.0, -jnp.inf).astype(F32)

    def scores(qi, kb, slot):
        ks = pl.ds(pl.multiple_of(kb * t, t), t)
        for hh in range(heads):
            kn = kn_ref[ks, hh * QK_NOPE_DIM:(hh + 1) * QK_NOPE_DIM]
            kcat = jnp.concatenate([kn, kr_ref[ks, :]], axis=1)
            s = jnp.dot(kcat, qt_ref[hh, qi], preferred_element_type=F32)
            s_ref[hh, slot] = s
            smax_ref[hh, slot] = jnp.max(s, axis=0, keepdims=True)

    def softmax_pv(kb, slot, masked):
        for hh in range(heads):
            s = s_ref[hh, slot]
            if masked:
                s = s + bias_ref[...]
                block_max = jnp.max(s, axis=0, keepdims=True)
            else:
                block_max = smax_ref[hh, slot]
            m_prev = m_ref[hh]
            m_new = jnp.maximum(m_prev, block_max)
            alpha = jnp.exp2(m_prev - m_new)
            p = jnp.exp2(s - m_new)
            l_ref[hh] = alpha * l_ref[hh] + jnp.sum(p, axis=0, keepdims=True)
            pv = jnp.dot(vt_ref[hh, kb], p.astype(BF16), preferred_element_type=F32)
            acc_ref[hh] = alpha * acc_ref[hh] + pv
            m_ref[hh] = m_new

    def query_block(qi, carry):
        m_ref[...] = jnp.full(m_ref.shape, -jnp.inf, F32)
        l_ref[...] = jnp.zeros(l_ref.shape, F32)
        acc_ref[...] = jnp.zeros(acc_ref.shape, F32)
        nxt = jnp.minimum(qi + 1, nq - 1)

        @pl.when(qi == 0)
        def _():
            softmax_pv(0, first, masked=True)
            scores(nxt, 0, first)

        @pl.when(qi > 0)
        def _():
            scores(qi, 1, 1)
            softmax_pv(0, first, masked=False)
            n_groups = (qi - 1) // unroll

            def group(g, c):
                b0 = 1 + g * unroll
                for u in range(unroll):
                    scores(qi, b0 + u + 1, u % 2)
                    softmax_pv(b0 + u, (u + 1) % 2, masked=False)
                return c

            lax.fori_loop(0, n_groups, group, 0)
            base = 1 + n_groups * unroll
            rest = qi - base
            for k in range(0, unroll - 2, 2):
                @pl.when(rest >= k + 2)
                def _(k=k):
                    for u in range(2):
                        scores(qi, base + k + u + 1, u % 2)
                        softmax_pv(base + k + u, (u + 1) % 2, masked=False)

            @pl.when(rest % 2 == 1)
            def _():
                scores(qi, base + rest, 0)
                softmax_pv(base + rest - 1, 1, masked=False)

            for parity in (0, 1):
                @pl.when(rest % 2 == parity)
                def _(parity=parity):
                    scores(nxt, 0, first)
                    softmax_pv(qi, (1 + parity) % 2, masked=True)

        rows = pl.ds(pl.multiple_of(qi * t, t), t)
        for hh in range(heads):
            out = (acc_ref[hh] * (1.0 / l_ref[hh])).T
            o_ref[rows, hh * V_HEAD_DIM:(hh + 1) * V_HEAD_DIM] = out.astype(o_ref.dtype)
        return carry

    scores(0, 0, first)
    lax.fori_loop(0, nq, query_block, 0)


def _flash_attention(k_nope, k_rope, q_t, v_t, *, n_heads, t, unroll=4, heads_per_step=2):
    s = k_nope.shape[0]
    nq = s // t
    hps = heads_per_step
    assert t % CHUNK == 0 and unroll % 2 == 0 and nq >= 2 and n_heads % hps == 0
    assert v_t.shape == (n_heads, nq, V_HEAD_DIM, t) and q_t.shape == (n_heads, nq, V7X_MXU_DIM, t)
    scratch = [((hps, 3, t, t), F32), ((hps, 3, 1, t), F32), ((t, t), F32), ((hps, 1, t), F32), ((hps, 1, t), F32),
               ((hps, V_HEAD_DIM, t), F32)]
    return pl.pallas_call(
        functools.partial(_flash_kernel, t=t, unroll=unroll),
        grid=(n_heads // hps,),
        in_specs=[
            pl.BlockSpec((s, hps * QK_NOPE_DIM), lambda h: (0, h)),
            pl.BlockSpec((s, V7X_LANES), lambda h: (0, 0)),
            pl.BlockSpec((hps, nq, V7X_MXU_DIM, t), lambda h: (h, 0, 0, 0)),
            pl.BlockSpec((hps, nq, V_HEAD_DIM, t), lambda h: (h, 0, 0, 0)),
        ],
        out_specs=pl.BlockSpec((s, hps * V_HEAD_DIM), lambda h: (0, h)),
        out_shape=jax.ShapeDtypeStruct((s, n_heads * V_HEAD_DIM), BF16),
        scratch_shapes=[pltpu.VMEM(shape, dtype) for shape, dtype in scratch],
        compiler_params=_params(("arbitrary",),
                                [((s, hps * QK_NOPE_DIM), BF16), ((s, V7X_LANES), BF16),
                                 ((hps * V7X_MXU_DIM, s), BF16), ((hps * V_HEAD_DIM, s), BF16),
                                 ((s, hps * V_HEAD_DIM), BF16)],
                                scratch=scratch,
                                temps=[((t, t), F32), ((t, t), BF16), ((t, V7X_MXU_DIM), BF16)]),
        name="flash_attention",
    )(k_nope, k_rope, q_t, v_t)


def _gated_merge_kernel(pm_ref, at_ref, h_ref, wup_ref, wum_ref, wga_ref, wgb_ref, wo_ref, o_ref, wo_o_ref):
    h = h_ref[...]
    ya = jnp.dot(pm_ref[...], wup_ref[...], preferred_element_type=F32)
    yb = jnp.dot(at_ref[...], wum_ref[...], preferred_element_type=F32)
    ga = _sigmoid(jnp.dot(h, wga_ref[...], preferred_element_type=F32))
    gb = _sigmoid(jnp.dot(h, wgb_ref[...], preferred_element_type=F32))
    o_ref[...] = (ga * ya + gb * yb).astype(o_ref.dtype)
    wo_o_ref[...] = wo_ref[...].astype(wo_o_ref.dtype)


def _gated_merge(pm, attn, h, w_up_pool, w_up_mla, w_gate2d, w_out, *, tm=1024, tn=256):
    s, d = h.shape
    kp, km = pm.shape[1], attn.shape[1]
    nb = d // tn
    slab_block, slab_spec = _slab_spec(w_out, (s // tm) * nb, nb)
    return pl.pallas_call(
        _gated_merge_kernel,
        grid=(s // tm, nb),
        in_specs=[
            pl.BlockSpec((tm, kp), lambda i, j: (i, 0)),
            pl.BlockSpec((tm, km), lambda i, j: (i, 0)),
            pl.BlockSpec((tm, d), lambda i, j: (i, 0)),
            pl.BlockSpec((kp, tn), lambda i, j: (0, j)),
            pl.BlockSpec((km, tn), lambda i, j: (0, j)),
            pl.BlockSpec((d, tn), lambda i, j: (0, j)),
            pl.BlockSpec((d, tn), lambda i, j: (0, j + nb)),
            slab_spec,
        ],
        out_specs=[pl.BlockSpec((tm, tn), lambda i, j: (i, j)), slab_spec],
        out_shape=[jax.ShapeDtypeStruct((s, d), BF16), jax.ShapeDtypeStruct(w_out.shape, BF16)],
        compiler_params=_params(("parallel", "parallel"),
                                [((tm, kp), BF16), ((tm, km), BF16), ((tm, d), BF16), ((kp, tn), BF16),
                                 ((km, tn), BF16), ((d, tn), BF16), ((d, tn), BF16), ((tm, tn), BF16),
                                 (slab_block, F32), (slab_block, BF16)],
                                temps=[((tm, tn), F32)] * 4),
        name="gated_merge",
    )(pm, attn, h, w_up_pool, w_up_mla, w_gate2d, w_gate2d, w_out)


def _residual_norm_kernel(t_ref, x_ref, gpost_ref, gnext_ref, xo_ref, ho_ref):
    xo = x_ref[...] + _rms(t_ref[...].astype(F32), gpost_ref[...])
    xo_ref[...] = xo
    ho_ref[...] = _rms(xo, gnext_ref[...]).astype(ho_ref.dtype)


def _residual_final_kernel(t_ref, x_ref, gpost_ref, xo_ref):
    xo_ref[...] = x_ref[...] + _rms(t_ref[...].astype(F32), gpost_ref[...])


def _residual_norm(t, x, g_post, g_next=None, *, tm=256):
    s, d = x.shape
    row = pl.BlockSpec((tm, d), lambda i: (i, 0))
    vec = pl.BlockSpec((1, d), lambda i: (0, 0))
    blocks = [((tm, d), t.dtype), ((tm, d), F32), ((tm, d), F32)]
    if g_next is None:
        return pl.pallas_call(
            _residual_final_kernel, grid=(s // tm,), in_specs=[row, row, vec], out_specs=row,
            out_shape=jax.ShapeDtypeStruct((s, d), F32),
            compiler_params=_params(("parallel",), blocks, temps=[((tm, d), F32)] * 2),
            name="residual_final",
        )(t, x, g_post)
    return pl.pallas_call(
        _residual_norm_kernel, grid=(s // tm,), in_specs=[row, row, vec, vec], out_specs=[row, row],
        out_shape=[jax.ShapeDtypeStruct((s, d), F32), jax.ShapeDtypeStruct((s, d), BF16)],
        compiler_params=_params(("parallel",), blocks + [((tm, d), BF16)], temps=[((tm, d), F32)] * 2),
        name="residual_norm",
    )(t, x, g_post, g_next)


def _swiglu_up_kernel(h_ref, wg_ref, wu_ref, wd_ref, o_ref, wd_o_ref):
    h = h_ref[...]
    gate = jnp.dot(h, _load_bf16(wg_ref), preferred_element_type=F32)
    up = jnp.dot(h, _load_bf16(wu_ref), preferred_element_type=F32)
    o_ref[...] = (gate * _sigmoid(gate) * up).astype(o_ref.dtype)
    wd_o_ref[...] = wd_ref[...].astype(wd_o_ref.dtype)


def _swiglu_up(h, w_gate, w_up, w_down, *, tm=1024, tn=256):
    s, d = h.shape
    f = w_gate.shape[1]
    nj = f // tn
    steps = (s // tm) * nj
    slab = w_down.shape[0] // steps
    assert slab * steps == w_down.shape[0] and slab % 16 == 0
    dn = w_down.shape[1]
    return pl.pallas_call(
        _swiglu_up_kernel,
        grid=(s // tm, nj),
        in_specs=[
            pl.BlockSpec((tm, d), lambda i, j: (i, 0)),
            pl.BlockSpec((d, tn), lambda i, j: (0, j)),
            pl.BlockSpec((d, tn), lambda i, j: (0, j)),
            pl.BlockSpec((slab, dn), lambda i, j: (i * nj + j, 0)),
        ],
        out_specs=[pl.BlockSpec((tm, tn), lambda i, j: (i, j)),
                   pl.BlockSpec((slab, dn), lambda i, j: (i * nj + j, 0))],
        out_shape=[jax.ShapeDtypeStruct((s, f), BF16), jax.ShapeDtypeStruct(w_down.shape, BF16)],
        compiler_params=_params(("parallel", "parallel"),
                                [((tm, d), BF16), ((d, tn), w_gate.dtype), ((d, tn), w_up.dtype), ((tm, tn), BF16),
                                 ((slab, dn), w_down.dtype), ((slab, dn), BF16)],
                                temps=[((tm, tn), F32)] * 3 + [((d, tn), BF16)] * 2),
        name="swiglu_up",
    )(h, w_gate, w_up, w_down)


def _ple_gate_kernel(h_ref, p_ref, wg_ref, wp_ref, o_ref):
    gate = _sigmoid(jnp.dot(h_ref[...], _load_bf16(wg_ref), preferred_element_type=F32))
    pe = jnp.dot(_load_bf16(p_ref), _load_bf16(wp_ref), preferred_element_type=F32)
    o_ref[...] = (pe * gate).astype(o_ref.dtype)


def _ple_gate(h, p, w_gate, w_proj, *, tm=1024, tn=512):
    s, d = h.shape
    r = p.shape[1]
    return pl.pallas_call(
        _ple_gate_kernel,
        grid=(s // tm, d // tn),
        in_specs=[
            pl.BlockSpec((tm, d), lambda i, j: (i, 0)),
            pl.BlockSpec((tm, r), lambda i, j: (i, 0)),
            pl.BlockSpec((d, tn), lambda i, j: (0, j)),
            pl.BlockSpec((r, tn), lambda i, j: (0, j)),
        ],
        out_specs=pl.BlockSpec((tm, tn), lambda i, j: (i, j)),
        out_shape=jax.ShapeDtypeStruct((s, d), BF16),
        compiler_params=_params(("parallel", "parallel"),
                                [((tm, d), BF16), ((tm, r), p.dtype), ((d, tn), w_gate.dtype), ((r, tn), w_proj.dtype),
                                 ((tm, tn), BF16)],
                                temps=[((tm, tn), F32)] * 2 + [((d, tn), BF16)]),
        name="ple_gate",
    )(h, p, w_gate, w_proj)


def kernel(x, p, positions, norm_mix_pre, norm_mix_post, w_in, q_norm, kv_norm, w_q_b, w_kv_b, w_pool, pool_scale,
           w_up_pool, w_up_mla, w_branch_gate, w_out, norm_ffn_pre, norm_ffn_post, w_ffn_gate, w_ffn_up,
           w_ffn_down, norm_ple_pre, w_ple_gate, w_ple_proj, norm_ple_post):
    batch, seq, d_model = x.shape
    depth = w_in.shape[0]
    in_width = w_in.shape[2]
    q_rank, n_heads, qk_dim = w_q_b.shape[1:]
    kv_rank = w_kv_b.shape[1]
    pool_width = w_up_pool.shape[1]
    assert batch == 1 and qk_dim == QK_NOPE_DIM + QK_ROPE_DIM
    assert w_kv_b.shape[3] == QK_NOPE_DIM + V_HEAD_DIM
    o_q, o_kv, o_kr = pool_width, pool_width + q_rank, pool_width + q_rank + kv_rank
    assert o_kr + QK_ROPE_DIM == in_width
    assert o_q % q_rank == 0 and o_kv % kv_rank == 0 and o_kr % QK_ROPE_DIM == 0
    z_tn = 512
    assert o_kr % z_tn == 0
    attn_t = 512
    half = QK_ROPE_DIM // 2
    scale = float(qk_dim) ** -0.5 * LOG2_E

    inv_freq = ROPE_THETA ** (-jnp.arange(0, QK_ROPE_DIM, 2, dtype=F32) / QK_ROPE_DIM)
    freq_col = inv_freq.reshape(half, 1)
    freq_row = jnp.zeros((1, V7X_LANES), F32).at[0, :half].set(inv_freq).at[0, half:QK_ROPE_DIM].set(inv_freq)
    pos_f = positions.astype(F32)
    pos_row, pos_col = pos_f.reshape(1, seq), pos_f.reshape(seq, 1)

    xs = x.reshape(seq, d_model)
    for i in range(depth):
        w_in_t = jnp.transpose(w_in[i])
        wq = w_q_b[i].astype(BF16)
        wq_t = jnp.pad(jnp.transpose(wq, (1, 2, 0)), ((0, 0), (0, V7X_MXU_DIM - qk_dim), (0, 0)))
        wq_t = wq_t.reshape(n_heads * V7X_MXU_DIM, q_rank)
        wkv = w_kv_b[i].astype(BF16)
        wk = wkv[:, :, :QK_NOPE_DIM].reshape(kv_rank, n_heads * QK_NOPE_DIM)
        wv_t = jnp.transpose(wkv[:, :, QK_NOPE_DIM:], (1, 2, 0)).reshape(n_heads * V_HEAD_DIM, kv_rank)
        w_up_p, w_up_m = _cast_pair(w_up_pool[i], w_up_mla[i])

        def row(v):
            return v[i].reshape(1, -1)

        h = _rmsnorm_cast(xs, row(norm_mix_pre))
        z = _matmul(h, w_in_t, tm=1024, tn=z_tn, out_dtype=F32, name="in_proj", b_is_nk=True, n=o_kr)
        pm = _pool_mixer(z, w_pool[i].astype(BF16), row(pool_scale), pool_width=pool_width)
        q_t = _q_proj(z, row(q_norm), wq_t, pos_row, freq_col, col_block=o_q // q_rank, rank=q_rank, scale=scale,
                      tm=attn_t)
        k_nope, v_t, k_rope = _kv_proj(z, h, w_in_t, row(kv_norm), wk, wv_t, pos_col, freq_row,
                                       lat_block=o_kv // kv_rank, rope_row_block=o_kr // QK_ROPE_DIM, rank=kv_rank,
                                       tm=attn_t)
        attn = _flash_attention(k_nope, k_rope, q_t, v_t, n_heads=n_heads, t=attn_t)
        w_gate2d = w_branch_gate[i].astype(BF16).reshape(d_model, 2 * d_model)
        merged, w_out_b = _gated_merge(pm, attn, h, w_up_p, w_up_m, w_gate2d, w_out[i])
        mix = _matmul(merged, w_out_b, tm=1024, tn=1024, out_dtype=BF16, name="out_proj")
        xs, h2 = _residual_norm(mix, xs, row(norm_mix_post), row(norm_ffn_pre))
        act, w_down = _swiglu_up(h2, w_ffn_gate[i], w_ffn_up[i], w_ffn_down[i])
        ffn, w_pg = _matmul(act, w_down, tm=512, tn=512, out_dtype=BF16, name="ffn_down", side_cast=w_ple_gate[i])
        xs, h3 = _residual_norm(ffn, xs, row(norm_ffn_post), row(norm_ple_pre))
        t = _ple_gate(h3, p[i].reshape(seq, -1), w_pg, w_ple_proj[i], tn=1024)
        xs = _residual_norm(t, xs, row(norm_ple_post))
    return xs.reshape(batch, seq, d_model)
```

```python
import functools

import jax
import jax.numpy as jnp
from jax import lax
from jax.experimental import pallas as pl
from jax.experimental.pallas import tpu as pltpu

CHUNK = 64
EPS = 1e-6
POOL_WINDOWS = (2, 4, 8, 16)
QK_NOPE_DIM = 128
QK_ROPE_DIM = 64
V_HEAD_DIM = 128
ROPE_THETA = 10000.0
LOG2_E = 1.4426950408889634

V7X_LANES = 128
V7X_MXU_DIM = 256
V7X_VMEM_BYTES = 64 * 1024 * 1024
V7X_VMEM_USABLE_BYTES = 60000 * 1024
V7X_VMEM_DEFAULT_SCOPED_BYTES = 32 * 1024 * 1024

POOL_HALO = 16
POOL_PAD = 8
F32 = jnp.float32
BF16 = jnp.bfloat16


def _nbytes(shape, dtype):
    n = 1
    for s in shape:
        n *= s
    return n * jnp.dtype(dtype).itemsize


def _params(semantics, blocks, scratch=(), temps=()):
    need = 2 * sum(_nbytes(s, d) for s, d in blocks)
    need += sum(_nbytes(s, d) for s, d in scratch) + sum(_nbytes(s, d) for s, d in temps)
    need = max(int(need * 1.25) + (2 << 20), V7X_VMEM_DEFAULT_SCOPED_BYTES)
    return pltpu.CompilerParams(dimension_semantics=semantics,
                                vmem_limit_bytes=min(need, V7X_VMEM_USABLE_BYTES))


def _load_bf16(ref):
    v = ref[...]
    return v if v.dtype == BF16 else v.astype(BF16)


def _sigmoid(x):
    return 0.5 * jnp.tanh(0.5 * x) + 0.5


def _rms(xf, g):
    return xf * lax.rsqrt(jnp.mean(xf * xf, axis=-1, keepdims=True) + EPS) * g


def _rmsnorm_cast_kernel(x_ref, g_ref, o_ref):
    o_ref[...] = _rms(x_ref[...], g_ref[...]).astype(o_ref.dtype)


def _rmsnorm_cast(x, g, *, tm=512):
    s, d = x.shape
    return pl.pallas_call(
        _rmsnorm_cast_kernel,
        grid=(s // tm,),
        in_specs=[pl.BlockSpec((tm, d), lambda i: (i, 0)), pl.BlockSpec((1, d), lambda i: (0, 0))],
        out_specs=pl.BlockSpec((tm, d), lambda i: (i, 0)),
        out_shape=jax.ShapeDtypeStruct((s, d), BF16),
        compiler_params=_params(("parallel",), [((tm, d), F32), ((tm, d), BF16)], temps=[((tm, d), F32)]),
        name="rmsnorm_cast",
    )(x, g)


def _cast_pair_kernel(a_ref, b_ref, ao_ref, bo_ref):
    ao_ref[...] = a_ref[...].astype(ao_ref.dtype)
    bo_ref[...] = b_ref[...].astype(bo_ref.dtype)


def _cast_pair(a, b, *, rows=256):
    k, n = a.shape
    assert b.shape == a.shape
    spec = pl.BlockSpec((rows, n), lambda i: (i, 0))
    return pl.pallas_call(
        _cast_pair_kernel,
        grid=(k // rows,),
        in_specs=[spec, spec],
        out_specs=[spec, spec],
        out_shape=[jax.ShapeDtypeStruct((k, n), BF16)] * 2,
        compiler_params=_params(("parallel",), [((rows, n), F32), ((rows, n), BF16)] * 2),
        name="cast_up_weights",
    )(a, b)


def _slab_spec(w, steps, nj):
    slab = w.shape[0] // steps
    assert slab * steps == w.shape[0] and slab % 16 == 0
    return (slab, w.shape[1]), pl.BlockSpec((slab, w.shape[1]), lambda i, j: (i * nj + j, 0))


def _matmul_kernel(a_ref, b_ref, *rest, b_is_nk):
    o_ref = rest[-2] if len(rest) == 3 else rest[0]
    contract_b = 1 if b_is_nk else 0
    o_ref[...] = lax.dot_general(a_ref[...], _load_bf16(b_ref), (((1,), (contract_b,)), ((), ())),
                                 preferred_element_type=F32).astype(o_ref.dtype)
    if len(rest) == 3:
        rest[2][...] = rest[0][...].astype(rest[2].dtype)


def _matmul(a, b, *, tm, tn, out_dtype, name, b_is_nk=False, n=None, side_cast=None):
    m, k = a.shape
    if n is None:
        n = b.shape[0] if b_is_nk else b.shape[1]
    if b_is_nk:
        b_block, b_spec = (tn, k), pl.BlockSpec((tn, k), lambda i, j: (j, 0))
    else:
        b_block, b_spec = (k, tn), pl.BlockSpec((k, tn), lambda i, j: (0, j))
    grid = (m // tm, n // tn)
    in_specs = [pl.BlockSpec((tm, k), lambda i, j: (i, 0)), b_spec]
    out_specs = pl.BlockSpec((tm, tn), lambda i, j: (i, j))
    out_shape = jax.ShapeDtypeStruct((m, n), out_dtype)
    blocks = [((tm, k), a.dtype), (b_block, b.dtype), ((tm, tn), out_dtype)]
    args = (a, b)
    if side_cast is not None:
        slab_block, slab_spec = _slab_spec(side_cast, grid[0] * grid[1], grid[1])
        in_specs.append(slab_spec)
        out_specs = [out_specs, slab_spec]
        out_shape = [out_shape, jax.ShapeDtypeStruct(side_cast.shape, BF16)]
        blocks += [(slab_block, F32), (slab_block, BF16)]
        args = (a, b, side_cast)
    return pl.pallas_call(
        functools.partial(_matmul_kernel, b_is_nk=b_is_nk),
        grid=grid, in_specs=in_specs, out_specs=out_specs, out_shape=out_shape,
        compiler_params=_params(("parallel", "parallel"), blocks, temps=[((tm, tn), F32), ((k, tn), BF16)]),
        name=name,
    )(*args)


def _pool_mixer_kernel(halo_ref, u_ref, w_ref, scale_ref, o_ref, ext_ref, lvl_a_ref, lvl_b_ref, *, tm, gw):
    i = pl.program_id(0)
    groups = len(POOL_WINDOWS)
    base = POOL_PAD + POOL_HALO
    n = POOL_HALO + tm
    width = ext_ref.shape[1]
    for ref in (ext_ref, lvl_a_ref, lvl_b_ref):
        ref[0:POOL_PAD, :] = jnp.zeros((POOL_PAD, width), F32)
    ext_ref[POOL_PAD:base, :] = jnp.where(i > 0, halo_ref[...], 0.0)
    ext_ref[base:base + tm, :] = u_ref[...]
    row = lax.broadcasted_iota(jnp.int32, (tm, 1), 0) + i * tm
    src = ext_ref
    for k, w in enumerate(POOL_WINDOWS):
        span = w // 2
        cols = slice(k * gw, (k + 1) * gw)
        u = u_ref[:, cols]
        win_sum = src[base:base + tm, cols] + src[base - span:base - span + tm, cols]
        inv_cnt = 1.0 / jnp.minimum(row + 1, w).astype(F32)
        pooled = win_sum * inv_cnt - u
        mixed = jnp.dot(pooled.astype(BF16), w_ref[k], preferred_element_type=F32)
        o_ref[:, cols] = (mixed * scale_ref[:, cols]).astype(o_ref.dtype)
        if k + 1 < groups:
            rest = slice((k + 1) * gw, width)
            dst = lvl_a_ref if k % 2 == 0 else lvl_b_ref
            dst[POOL_PAD:POOL_PAD + n, rest] = (src[POOL_PAD:POOL_PAD + n, rest]
                                                + src[POOL_PAD - span:POOL_PAD - span + n, rest])
            src = dst


def _pool_mixer(z, w_pool, pool_scale, *, pool_width, tm=512):
    s = z.shape[0]
    groups, gw, _ = w_pool.shape
    assert all(w == 2 ** (k + 1) for k, w in enumerate(POOL_WINDOWS)) and POOL_WINDOWS[-1] // 2 <= POOL_PAD
    halo_blocks = tm // POOL_HALO
    buf = (POOL_PAD + POOL_HALO + tm, pool_width)
    return pl.pallas_call(
        functools.partial(_pool_mixer_kernel, tm=tm, gw=gw),
        grid=(s // tm,),
        in_specs=[
            pl.BlockSpec((POOL_HALO, pool_width), lambda i: (jnp.maximum(i * halo_blocks - 1, 0), 0)),
            pl.BlockSpec((tm, pool_width), lambda i: (i, 0)),
            pl.BlockSpec((groups, gw, gw), lambda i: (0, 0, 0)),
            pl.BlockSpec((1, pool_width), lambda i: (0, 0)),
        ],
        out_specs=pl.BlockSpec((tm, pool_width), lambda i: (i, 0)),
        out_shape=jax.ShapeDtypeStruct((s, pool_width), BF16),
        scratch_shapes=[pltpu.VMEM(buf, F32)] * 3,
        compiler_params=_params(("parallel",),
                                [((tm, pool_width), F32), ((groups, gw, gw), BF16), ((tm, pool_width), BF16)],
                                scratch=[(buf, F32)] * 3,
                                temps=[((tm, pool_width), F32)]),
        name="pool_mixer",
    )(z, z, w_pool, pool_scale)


def _q_proj_kernel(ql_ref, g_ref, w_ref, pos_ref, freq_ref, o_ref, qn_ref, cos_ref, sin_ref, *, scale):
    @pl.when(pl.program_id(1) == 0)
    def _():
        qn_ref[...] = (_rms(ql_ref[...], g_ref[...]) * scale).astype(qn_ref.dtype)
        ang = freq_ref[...] * pos_ref[...]
        cos_ref[...] = jnp.cos(ang)
        sin_ref[...] = jnp.sin(ang)

    qt = lax.dot_general(w_ref[...], qn_ref[...], (((1,), (1,)), ((), ())), preferred_element_type=F32)
    half = QK_ROPE_DIM // 2
    r0, r1, r2 = QK_NOPE_DIM, QK_NOPE_DIM + half, QK_NOPE_DIM + QK_ROPE_DIM
    c, sn = cos_ref[...], sin_ref[...]
    rows = o_ref.shape[2]
    for hh in range(o_ref.shape[0]):
        q = qt[hh * rows:(hh + 1) * rows]
        x1, x2 = q[r0:r1], q[r1:r2]
        o_ref[hh, 0, 0:r0, :] = q[0:r0].astype(o_ref.dtype)
        o_ref[hh, 0, r0:r1, :] = (x1 * c - x2 * sn).astype(o_ref.dtype)
        o_ref[hh, 0, r1:r2, :] = (x1 * sn + x2 * c).astype(o_ref.dtype)
        o_ref[hh, 0, r2:, :] = q[r2:].astype(o_ref.dtype)


def _q_proj(z, q_norm, wq_t, pos_row, freq_col, *, col_block, rank, scale, tm, heads_per_step=8):
    s = z.shape[0]
    rows = V7X_MXU_DIM
    n_heads = wq_t.shape[0] // rows
    hps = heads_per_step
    half = QK_ROPE_DIM // 2
    return pl.pallas_call(
        functools.partial(_q_proj_kernel, scale=scale),
        grid=(s // tm, n_heads // hps),
        in_specs=[
            pl.BlockSpec((tm, rank), lambda i, h: (i, col_block)),
            pl.BlockSpec((1, rank), lambda i, h: (0, 0)),
            pl.BlockSpec((hps * rows, rank), lambda i, h: (h, 0)),
            pl.BlockSpec((1, tm), lambda i, h: (0, i)),
            pl.BlockSpec((half, 1), lambda i, h: (0, 0)),
        ],
        out_specs=pl.BlockSpec((hps, 1, rows, tm), lambda i, h: (h, i, 0, 0)),
        out_shape=jax.ShapeDtypeStruct((n_heads, s // tm, rows, tm), BF16),
        scratch_shapes=[pltpu.VMEM((tm, rank), BF16), pltpu.VMEM((half, tm), F32), pltpu.VMEM((half, tm), F32)],
        compiler_params=_params(("parallel", "arbitrary"),
                                [((tm, rank), F32), ((hps * rows, rank), BF16), ((hps * rows, tm), BF16)],
                                scratch=[((tm, rank), BF16), ((2 * half, tm), F32)],
                                temps=[((tm, rank), F32), ((hps * rows, tm), F32)]),
        name="q_proj",
    )(z, q_norm, wq_t, pos_row, freq_col)


def _kv_proj_kernel(kvl_ref, h_ref, wkr_ref, g_ref, wk_ref, wvt_ref, pos_ref, freq_ref, k_ref, vt_ref, kro_ref):
    kvn = _rms(kvl_ref[...], g_ref[...]).astype(BF16)
    k_ref[...] = jnp.dot(kvn, wk_ref[...], preferred_element_type=F32).astype(k_ref.dtype)
    vt = lax.dot_general(wvt_ref[...], kvn, (((1,), (1,)), ((), ())), preferred_element_type=F32)
    vt_ref[:, 0] = vt.reshape(vt_ref.shape[0], V_HEAD_DIM, vt.shape[1]).astype(vt_ref.dtype)
    half = QK_ROPE_DIM // 2
    wkr = _load_bf16(wkr_ref)
    wkr = jnp.concatenate([wkr, jnp.zeros((V7X_LANES - wkr.shape[0], wkr.shape[1]), BF16)], axis=0)
    x = lax.dot_general(h_ref[...], wkr, (((1,), (1,)), ((), ())), preferred_element_type=F32)
    ang = pos_ref[...] * freq_ref[...]
    lane = lax.broadcasted_iota(jnp.int32, x.shape, 1)
    x2_at_lo = pltpu.roll(x, V7X_LANES - half, axis=1)
    x1_at_hi = pltpu.roll(x, half, axis=1)
    partner = jnp.where(lane < half, -x2_at_lo, jnp.where(lane < QK_ROPE_DIM, x1_at_hi, 0.0))
    kro_ref[...] = (x * jnp.cos(ang) + partner * jnp.sin(ang)).astype(kro_ref.dtype)


def _kv_proj(z, h, w_in_t, kv_norm, wk, wv_t, pos_col, freq_row, *, lat_block, rope_row_block, rank, tm):
    s, d = h.shape
    nk = wk.shape[1]
    nv = wv_t.shape[0]
    n_heads = nv // V_HEAD_DIM
    return pl.pallas_call(
        _kv_proj_kernel,
        grid=(s // tm,),
        in_specs=[
            pl.BlockSpec((tm, rank), lambda i: (i, lat_block)),
            pl.BlockSpec((tm, d), lambda i: (i, 0)),
            pl.BlockSpec((QK_ROPE_DIM, d), lambda i: (rope_row_block, 0)),
            pl.BlockSpec((1, rank), lambda i: (0, 0)),
            pl.BlockSpec((rank, nk), lambda i: (0, 0)),
            pl.BlockSpec((nv, rank), lambda i: (0, 0)),
            pl.BlockSpec((tm, 1), lambda i: (i, 0)),
            pl.BlockSpec((1, V7X_LANES), lambda i: (0, 0)),
        ],
        out_specs=[
            pl.BlockSpec((tm, nk), lambda i: (i, 0)),
            pl.BlockSpec((n_heads, 1, V_HEAD_DIM, tm), lambda i: (0, i, 0, 0)),
            pl.BlockSpec((tm, V7X_LANES), lambda i: (i, 0)),
        ],
        out_shape=[
            jax.ShapeDtypeStruct((s, nk), BF16),
            jax.ShapeDtypeStruct((n_heads, s // tm, V_HEAD_DIM, tm), BF16),
            jax.ShapeDtypeStruct((s, V7X_LANES), BF16),
        ],
        compiler_params=_params(("parallel",),
                                [((tm, rank), F32), ((tm, d), BF16), ((QK_ROPE_DIM, d), w_in_t.dtype), ((rank, nk), BF16),
                                 ((nv, rank), BF16), ((tm, nk), BF16), ((nv, tm), BF16), ((tm, V7X_LANES), BF16),
                                 ((tm, V7X_LANES), F32)],
                                temps=[((tm, nk), F32), ((nv, tm), F32), ((V7X_LANES, d), BF16)]),
        name="kv_proj",
    )(z, h, w_in_t, kv_norm, wk, wv_t, pos_col, freq_row)


def _flash_kernel(kn_ref, kr_ref, qt_ref, vt_ref, o_ref, s_ref, smax_ref, bias_ref, m_ref, l_ref, acc_ref,
                  *, t, unroll):
    heads, nq = qt_ref.shape[0], qt_ref.shape[1]
    first = 2
    kc = lax.broadcasted_iota(jnp.int32, (t, t), 0) // CHUNK
    qc = lax.broadcasted_iota(jnp.int32, (t, t), 1) // CHUNK
    bias_ref[...] = jnp.where(kc <= qc, 0.0, -jnp.inf).astype(F32)

    def scores(qi, kb, slot):
        ks = pl.ds(pl.multiple_of(kb * t, t), t)
        for hh in range(heads):
            kn = kn_ref[ks, hh * QK_NOPE_DIM:(hh + 1) * QK_NOPE_DIM]
            kcat = jnp.concatenate([kn, kr_ref[ks, :]], axis=1)
            s = jnp.dot(kcat, qt_ref[hh, qi], preferred_element_type=F32)
            s_ref[hh, slot] = s
            smax_ref[hh, slot] = jnp.max(s, axis=0, keepdims=True)

    def softmax_pv(kb, slot, masked):
        for hh in range(heads):
            s = s_ref[hh, slot]
            if masked:
                s = s + bias_ref[...]
                block_max = jnp.max(s, axis=0, keepdims=True)
            else:
                block_max = smax_ref[hh, slot]
            m_prev = m_ref[hh]
            m_new = jnp.maximum(m_prev, block_max)
            alpha = jnp.exp2(m_prev - m_new)
            p = jnp.exp2(s - m_new)
            l_ref[hh] = alpha * l_ref[hh] + jnp.sum(p, axis=0, keepdims=True)
            pv = jnp.dot(vt_ref[hh, kb], p.astype(BF16), preferred_element_type=F32)
            acc_ref[hh] = alpha * acc_ref[hh] + pv
            m_ref[hh] = m_new

    def query_block(qi, carry):
        m_ref[...] = jnp.full(m_ref.shape, -jnp.inf, F32)
        l_ref[...] = jnp.zeros(l_ref.shape, F32)
        acc_ref[...] = jnp.zeros(acc_ref.shape, F32)
        nxt = jnp.minimum(qi + 1, nq - 1)

        @pl.when(qi == 0)
        def _():
            softmax_pv(0, first, masked=True)
            scores(nxt, 0, first)

        @pl.when(qi > 0)
        def _():
            scores(qi, 1, 1)
            softmax_pv(0, first, masked=False)
            n_groups = (qi - 1) // unroll

            def group(g, c):
                b0 = 1 + g * unroll
                for u in range(unroll):
                    scores(qi, b0 + u + 1, u % 2)
                    softmax_pv(b0 + u, (u + 1) % 2, masked=False)
                return c

            lax.fori_loop(0, n_groups, group, 0)
            base = 1 + n_groups * unroll
            rest = qi - base
            for r in range(unroll):
                @pl.when(rest == r)
                def _(r=r):
                    for u in range(r):
                        scores(qi, base + u + 1, u % 2)
                        softmax_pv(base + u, (u + 1) % 2, masked=False)
                    scores(nxt, 0, first)
                    softmax_pv(qi, (1 + r) % 2, masked=True)

        rows = pl.ds(pl.multiple_of(qi * t, t), t)
        for hh in range(heads):
            out = (acc_ref[hh] * (1.0 / l_ref[hh])).T
            o_ref[rows, hh * V_HEAD_DIM:(hh + 1) * V_HEAD_DIM] = out.astype(o_ref.dtype)
        return carry

    scores(0, 0, first)
    lax.fori_loop(0, nq, query_block, 0)


def _flash_attention(k_nope, k_rope, q_t, v_t, *, n_heads, t, unroll=4, heads_per_step=2):
    s = k_nope.shape[0]
    nq = s // t
    hps = heads_per_step
    assert t % CHUNK == 0 and unroll % 2 == 0 and nq >= 2 and n_heads % hps == 0
    assert v_t.shape == (n_heads, nq, V_HEAD_DIM, t) and q_t.shape == (n_heads, nq, V7X_MXU_DIM, t)
    scratch = [((hps, 3, t, t), F32), ((hps, 3, 1, t), F32), ((t, t), F32), ((hps, 1, t), F32), ((hps, 1, t), F32),
               ((hps, V_HEAD_DIM, t), F32)]
    return pl.pallas_call(
        functools.partial(_flash_kernel, t=t, unroll=unroll),
        grid=(n_heads // hps,),
        in_specs=[
            pl.BlockSpec((s, hps * QK_NOPE_DIM), lambda h: (0, h)),
            pl.BlockSpec((s, V7X_LANES), lambda h: (0, 0)),
            pl.BlockSpec((hps, nq, V7X_MXU_DIM, t), lambda h: (h, 0, 0, 0)),
            pl.BlockSpec((hps, nq, V_HEAD_DIM, t), lambda h: (h, 0, 0, 0)),
        ],
        out_specs=pl.BlockSpec((s, hps * V_HEAD_DIM), lambda h: (0, h)),
        out_shape=jax.ShapeDtypeStruct((s, n_heads * V_HEAD_DIM), BF16),
        scratch_shapes=[pltpu.VMEM(shape, dtype) for shape, dtype in scratch],
        compiler_params=_params(("arbitrary",),
                                [((s, hps * QK_NOPE_DIM), BF16), ((s, V7X_LANES), BF16),
                                 ((hps * V7X_MXU_DIM, s), BF16), ((hps * V_HEAD_DIM, s), BF16),
                                 ((s, hps * V_HEAD_DIM), BF16)],
                                scratch=scratch,
                                temps=[((t, t), F32), ((t, t), BF16), ((t, V7X_MXU_DIM), BF16)]),
        name="flash_attention",
    )(k_nope, k_rope, q_t, v_t)


def _gated_merge_kernel(pm_ref, at_ref, h_ref, wup_ref, wum_ref, wga_ref, wgb_ref, wo_ref, o_ref, wo_o_ref):
    h = h_ref[...]
    ya = jnp.dot(pm_ref[...], wup_ref[...], preferred_element_type=F32)
    yb = jnp.dot(at_ref[...], wum_ref[...], preferred_element_type=F32)
    ga = _sigmoid(jnp.dot(h, wga_ref[...], preferred_element_type=F32))
    gb = _sigmoid(jnp.dot(h, wgb_ref[...], preferred_element_type=F32))
    o_ref[...] = (ga * ya + gb * yb).astype(o_ref.dtype)
    wo_o_ref[...] = wo_ref[...].astype(wo_o_ref.dtype)


def _gated_merge(pm, attn, h, w_up_pool, w_up_mla, w_gate2d, w_out, *, tm=1024, tn=256):
    s, d = h.shape
    kp, km = pm.shape[1], attn.shape[1]
    nb = d // tn
    slab_block, slab_spec = _slab_spec(w_out, (s // tm) * nb, nb)
    return pl.pallas_call(
        _gated_merge_kernel,
        grid=(s // tm, nb),
        in_specs=[
            pl.BlockSpec((tm, kp), lambda i, j: (i, 0)),
            pl.BlockSpec((tm, km), lambda i, j: (i, 0)),
            pl.BlockSpec((tm, d), lambda i, j: (i, 0)),
            pl.BlockSpec((kp, tn), lambda i, j: (0, j)),
            pl.BlockSpec((km, tn), lambda i, j: (0, j)),
            pl.BlockSpec((d, tn), lambda i, j: (0, j)),
            pl.BlockSpec((d, tn), lambda i, j: (0, j + nb)),
            slab_spec,
        ],
        out_specs=[pl.BlockSpec((tm, tn), lambda i, j: (i, j)), slab_spec],
        out_shape=[jax.ShapeDtypeStruct((s, d), BF16), jax.ShapeDtypeStruct(w_out.shape, BF16)],
        compiler_params=_params(("parallel", "parallel"),
                                [((tm, kp), BF16), ((tm, km), BF16), ((tm, d), BF16), ((kp, tn), BF16),
                                 ((km, tn), BF16), ((d, tn), BF16), ((d, tn), BF16), ((tm, tn), BF16),
                                 (slab_block, F32), (slab_block, BF16)],
                                temps=[((tm, tn), F32)] * 4),
        name="gated_merge",
    )(pm, attn, h, w_up_pool, w_up_mla, w_gate2d, w_gate2d, w_out)


def _residual_norm_kernel(t_ref, x_ref, gpost_ref, gnext_ref, xo_ref, ho_ref):
    xo = x_ref[...] + _rms(t_ref[...].astype(F32), gpost_ref[...])
    xo_ref[...] = xo
    ho_ref[...] = _rms(xo, gnext_ref[...]).astype(ho_ref.dtype)


def _residual_final_kernel(t_ref, x_ref, gpost_ref, xo_ref):
    xo_ref[...] = x_ref[...] + _rms(t_ref[...].astype(F32), gpost_ref[...])


def _residual_norm(t, x, g_post, g_next=None, *, tm=256):
    s, d = x.shape
    row = pl.BlockSpec((tm, d), lambda i: (i, 0))
    vec = pl.BlockSpec((1, d), lambda i: (0, 0))
    blocks = [((tm, d), t.dtype), ((tm, d), F32), ((tm, d), F32)]
    if g_next is None:
        return pl.pallas_call(
            _residual_final_kernel, grid=(s // tm,), in_specs=[row, row, vec], out_specs=row,
            out_shape=jax.ShapeDtypeStruct((s, d), F32),
            compiler_params=_params(("parallel",), blocks, temps=[((tm, d), F32)] * 2),
            name="residual_final",
        )(t, x, g_post)
    return pl.pallas_call(
        _residual_norm_kernel, grid=(s // tm,), in_specs=[row, row, vec, vec], out_specs=[row, row],
        out_shape=[jax.ShapeDtypeStruct((s, d), F32), jax.ShapeDtypeStruct((s, d), BF16)],
        compiler_params=_params(("parallel",), blocks + [((tm, d), BF16)], temps=[((tm, d), F32)] * 2),
        name="residual_norm",
    )(t, x, g_post, g_next)


def _swiglu_up_kernel(h_ref, wg_ref, wu_ref, wd_ref, o_ref, wd_o_ref):
    h = h_ref[...]
    gate = jnp.dot(h, _load_bf16(wg_ref), preferred_element_type=F32)
    up = jnp.dot(h, _load_bf16(wu_ref), preferred_element_type=F32)
    o_ref[...] = (gate * _sigmoid(gate) * up).astype(o_ref.dtype)
    wd_o_ref[...] = wd_ref[...].astype(wd_o_ref.dtype)


def _swiglu_up(h, w_gate, w_up, w_down, *, tm=1024, tn=256):
    s, d = h.shape
    f = w_gate.shape[1]
    nj = f // tn
    steps = (s // tm) * nj
    slab = w_down.shape[0] // steps
    assert slab * steps == w_down.shape[0] and slab % 16 == 0
    dn = w_down.shape[1]
    return pl.pallas_call(
        _swiglu_up_kernel,
        grid=(s // tm, nj),
        in_specs=[
            pl.BlockSpec((tm, d), lambda i, j: (i, 0)),
            pl.BlockSpec((d, tn), lambda i, j: (0, j)),
            pl.BlockSpec((d, tn), lambda i, j: (0, j)),
            pl.BlockSpec((slab, dn), lambda i, j: (i * nj + j, 0)),
        ],
        out_specs=[pl.BlockSpec((tm, tn), lambda i, j: (i, j)),
                   pl.BlockSpec((slab, dn), lambda i, j: (i * nj + j, 0))],
        out_shape=[jax.ShapeDtypeStruct((s, f), BF16), jax.ShapeDtypeStruct(w_down.shape, BF16)],
        compiler_params=_params(("parallel", "parallel"),
                                [((tm, d), BF16), ((d, tn), w_gate.dtype), ((d, tn), w_up.dtype), ((tm, tn), BF16),
                                 ((slab, dn), w_down.dtype), ((slab, dn), BF16)],
                                temps=[((tm, tn), F32)] * 3 + [((d, tn), BF16)] * 2),
        name="swiglu_up",
    )(h, w_gate, w_up, w_down)


def _ple_gate_kernel(h_ref, p_ref, wg_ref, wp_ref, o_ref):
    gate = _sigmoid(jnp.dot(h_ref[...], _load_bf16(wg_ref), preferred_element_type=F32))
    pe = jnp.dot(_load_bf16(p_ref), _load_bf16(wp_ref), preferred_element_type=F32)
    o_ref[...] = (pe * gate).astype(o_ref.dtype)


def _ple_gate(h, p, w_gate, w_proj, *, tm=1024, tn=512):
    s, d = h.shape
    r = p.shape[1]
    return pl.pallas_call(
        _ple_gate_kernel,
        grid=(s // tm, d // tn),
        in_specs=[
            pl.BlockSpec((tm, d), lambda i, j: (i, 0)),
            pl.BlockSpec((tm, r), lambda i, j: (i, 0)),
            pl.BlockSpec((d, tn), lambda i, j: (0, j)),
            pl.BlockSpec((r, tn), lambda i, j: (0, j)),
        ],
        out_specs=pl.BlockSpec((tm, tn), lambda i, j: (i, j)),
        out_shape=jax.ShapeDtypeStruct((s, d), BF16),
        compiler_params=_params(("parallel", "parallel"),
                                [((tm, d), BF16), ((tm, r), p.dtype), ((d, tn), w_gate.dtype), ((r, tn), w_proj.dtype),
                                 ((tm, tn), BF16)],
                                temps=[((tm, tn), F32)] * 2 + [((d, tn), BF16)]),
        name="ple_gate",
    )(h, p, w_gate, w_proj)


def kernel(x, p, positions, norm_mix_pre, norm_mix_post, w_in, q_norm, kv_norm, w_q_b, w_kv_b, w_pool, pool_scale,
           w_up_pool, w_up_mla, w_branch_gate, w_out, norm_ffn_pre, norm_ffn_post, w_ffn_gate, w_ffn_up,
           w_ffn_down, norm_ple_pre, w_ple_gate, w_ple_proj, norm_ple_post):
    batch, seq, d_model = x.shape
    depth = w_in.shape[0]
    in_width = w_in.shape[2]
    q_rank, n_heads, qk_dim = w_q_b.shape[1:]
    kv_rank = w_kv_b.shape[1]
    pool_width = w_up_pool.shape[1]
    assert batch == 1 and qk_dim == QK_NOPE_DIM + QK_ROPE_DIM
    assert w_kv_b.shape[3] == QK_NOPE_DIM + V_HEAD_DIM
    o_q, o_kv, o_kr = pool_width, pool_width + q_rank, pool_width + q_rank + kv_rank
    assert o_kr + QK_ROPE_DIM == in_width
    assert o_q % q_rank == 0 and o_kv % kv_rank == 0 and o_kr % QK_ROPE_DIM == 0
    z_tn = 512
    assert o_kr % z_tn == 0
    attn_t = 512
    half = QK_ROPE_DIM // 2
    scale = float(qk_dim) ** -0.5 * LOG2_E

    inv_freq = ROPE_THETA ** (-jnp.arange(0, QK_ROPE_DIM, 2, dtype=F32) / QK_ROPE_DIM)
    freq_col = inv_freq.reshape(half, 1)
    freq_row = jnp.zeros((1, V7X_LANES), F32).at[0, :half].set(inv_freq).at[0, half:QK_ROPE_DIM].set(inv_freq)
    pos_f = positions.astype(F32)
    pos_row, pos_col = pos_f.reshape(1, seq), pos_f.reshape(seq, 1)

    xs = x.reshape(seq, d_model)
    for i in range(depth):
        w_in_t = jnp.transpose(w_in[i])
        wq = w_q_b[i].astype(BF16)
        wq_t = jnp.pad(jnp.transpose(wq, (1, 2, 0)), ((0, 0), (0, V7X_MXU_DIM - qk_dim), (0, 0)))
        wq_t = wq_t.reshape(n_heads * V7X_MXU_DIM, q_rank)
        wkv = w_kv_b[i].astype(BF16)
        wk = wkv[:, :, :QK_NOPE_DIM].reshape(kv_rank, n_heads * QK_NOPE_DIM)
        wv_t = jnp.transpose(wkv[:, :, QK_NOPE_DIM:], (1, 2, 0)).reshape(n_heads * V_HEAD_DIM, kv_rank)
        w_up_p, w_up_m = _cast_pair(w_up_pool[i], w_up_mla[i])

        def row(v):
            return v[i].reshape(1, -1)

        h = _rmsnorm_cast(xs, row(norm_mix_pre))
        z = _matmul(h, w_in_t, tm=1024, tn=z_tn, out_dtype=F32, name="in_proj", b_is_nk=True, n=o_kr)
        pm = _pool_mixer(z, w_pool[i].astype(BF16), row(pool_scale), pool_width=pool_width)
        q_t = _q_proj(z, row(q_norm), wq_t, pos_row, freq_col, col_block=o_q // q_rank, rank=q_rank, scale=scale,
                      tm=attn_t)
        k_nope, v_t, k_rope = _kv_proj(z, h, w_in_t, row(kv_norm), wk, wv_t, pos_col, freq_row,
                                       lat_block=o_kv // kv_rank, rope_row_block=o_kr // QK_ROPE_DIM, rank=kv_rank,
                                       tm=attn_t)
        attn = _flash_attention(k_nope, k_rope, q_t, v_t, n_heads=n_heads, t=attn_t)
        w_gate2d = w_branch_gate[i].astype(BF16).reshape(d_model, 2 * d_model)
        merged, w_out_b = _gated_merge(pm, attn, h, w_up_p, w_up_m, w_gate2d, w_out[i])
        mix = _matmul(merged, w_out_b, tm=1024, tn=1024, out_dtype=BF16, name="out_proj")
        xs, h2 = _residual_norm(mix, xs, row(norm_mix_post), row(norm_ffn_pre))
        act, w_down = _swiglu_up(h2, w_ffn_gate[i], w_ffn_up[i], w_ffn_down[i])
        ffn, w_pg = _matmul(act, w_down, tm=512, tn=512, out_dtype=BF16, name="ffn_down", side_cast=w_ple_gate[i])
        xs, h3 = _residual_norm(ffn, xs, row(norm_ffn_post), row(norm_ple_pre))
        t = _ple_gate(h3, p[i].reshape(seq, -1), w_pg, w_ple_proj[i], tn=1024)
        xs = _residual_norm(t, xs, row(norm_ple_post))
    return xs.reshape(batch, seq, d_model)
```

```python
import functools

import jax
import jax.numpy as jnp
from jax import lax
from jax.experimental import pallas as pl
from jax.experimental.pallas import tpu as pltpu

CHUNK = 64
EPS = 1e-6
POOL_WINDOWS = (2, 4, 8, 16)
QK_NOPE_DIM = 128
QK_ROPE_DIM = 64
V_HEAD_DIM = 128
ROPE_THETA = 10000.0
LOG2_E = 1.4426950408889634

V7X_LANES = 128
V7X_MXU_DIM = 256
V7X_VMEM_BYTES = 64 * 1024 * 1024
V7X_VMEM_USABLE_BYTES = 60000 * 1024
V7X_VMEM_DEFAULT_SCOPED_BYTES = 32 * 1024 * 1024

POOL_HALO = 16
POOL_PAD = 8
F32 = jnp.float32
BF16 = jnp.bfloat16


def _nbytes(shape, dtype):
    n = 1
    for s in shape:
        n *= s
    return n * jnp.dtype(dtype).itemsize


def _params(semantics, blocks, scratch=(), temps=()):
    need = 2 * sum(_nbytes(s, d) for s, d in blocks)
    need += sum(_nbytes(s, d) for s, d in scratch) + sum(_nbytes(s, d) for s, d in temps)
    need = max(int(need * 1.25) + (2 << 20), V7X_VMEM_DEFAULT_SCOPED_BYTES)
    return pltpu.CompilerParams(dimension_semantics=semantics,
                                vmem_limit_bytes=min(need, V7X_VMEM_USABLE_BYTES))


def _load_bf16(ref):
    v = ref[...]
    return v if v.dtype == BF16 else v.astype(BF16)


def _sigmoid(x):
    return 0.5 * jnp.tanh(0.5 * x) + 0.5


def _rms(xf, g):
    return xf * lax.rsqrt(jnp.mean(xf * xf, axis=-1, keepdims=True) + EPS) * g


def _rmsnorm_cast_kernel(x_ref, g_ref, o_ref):
    o_ref[...] = _rms(x_ref[...], g_ref[...]).astype(o_ref.dtype)


def _rmsnorm_cast(x, g, *, tm=512):
    s, d = x.shape
    return pl.pallas_call(
        _rmsnorm_cast_kernel,
        grid=(s // tm,),
        in_specs=[pl.BlockSpec((tm, d), lambda i: (i, 0)), pl.BlockSpec((1, d), lambda i: (0, 0))],
        out_specs=pl.BlockSpec((tm, d), lambda i: (i, 0)),
        out_shape=jax.ShapeDtypeStruct((s, d), BF16),
        compiler_params=_params(("parallel",), [((tm, d), F32), ((tm, d), BF16)], temps=[((tm, d), F32)]),
        name="rmsnorm_cast",
    )(x, g)


def _cast_pair_kernel(a_ref, b_ref, ao_ref, bo_ref):
    ao_ref[...] = a_ref[...].astype(ao_ref.dtype)
    bo_ref[...] = b_ref[...].astype(bo_ref.dtype)


def _cast_pair(a, b, *, rows=256):
    k, n = a.shape
    assert b.shape == a.shape
    spec = pl.BlockSpec((rows, n), lambda i: (i, 0))
    return pl.pallas_call(
        _cast_pair_kernel,
        grid=(k // rows,),
        in_specs=[spec, spec],
        out_specs=[spec, spec],
        out_shape=[jax.ShapeDtypeStruct((k, n), BF16)] * 2,
        compiler_params=_params(("parallel",), [((rows, n), F32), ((rows, n), BF16)] * 2),
        name="cast_up_weights",
    )(a, b)


def _slab_spec(w, steps, nj):
    slab = w.shape[0] // steps
    assert slab * steps == w.shape[0] and slab % 16 == 0
    return (slab, w.shape[1]), pl.BlockSpec((slab, w.shape[1]), lambda i, j: (i * nj + j, 0))


def _matmul_kernel(a_ref, b_ref, *rest, b_is_nk):
    o_ref = rest[-2] if len(rest) == 3 else rest[0]
    contract_b = 1 if b_is_nk else 0
    o_ref[...] = lax.dot_general(a_ref[...], _load_bf16(b_ref), (((1,), (contract_b,)), ((), ())),
                                 preferred_element_type=F32).astype(o_ref.dtype)
    if len(rest) == 3:
        rest[2][...] = rest[0][...].astype(rest[2].dtype)


def _matmul(a, b, *, tm, tn, out_dtype, name, b_is_nk=False, n=None, side_cast=None):
    m, k = a.shape
    if n is None:
        n = b.shape[0] if b_is_nk else b.shape[1]
    if b_is_nk:
        b_block, b_spec = (tn, k), pl.BlockSpec((tn, k), lambda i, j: (j, 0))
    else:
        b_block, b_spec = (k, tn), pl.BlockSpec((k, tn), lambda i, j: (0, j))
    grid = (m // tm, n // tn)
    in_specs = [pl.BlockSpec((tm, k), lambda i, j: (i, 0)), b_spec]
    out_specs = pl.BlockSpec((tm, tn), lambda i, j: (i, j))
    out_shape = jax.ShapeDtypeStruct((m, n), out_dtype)
    blocks = [((tm, k), a.dtype), (b_block, b.dtype), ((tm, tn), out_dtype)]
    args = (a, b)
    if side_cast is not None:
        slab_block, slab_spec = _slab_spec(side_cast, grid[0] * grid[1], grid[1])
        in_specs.append(slab_spec)
        out_specs = [out_specs, slab_spec]
        out_shape = [out_shape, jax.ShapeDtypeStruct(side_cast.shape, BF16)]
        blocks += [(slab_block, F32), (slab_block, BF16)]
        args = (a, b, side_cast)
    return pl.pallas_call(
        functools.partial(_matmul_kernel, b_is_nk=b_is_nk),
        grid=grid, in_specs=in_specs, out_specs=out_specs, out_shape=out_shape,
        compiler_params=_params(("parallel", "parallel"), blocks, temps=[((tm, tn), F32), ((k, tn), BF16)]),
        name=name,
    )(*args)


def _pool_mixer_kernel(halo_ref, u_ref, w_ref, scale_ref, o_ref, ext_ref, lvl_a_ref, lvl_b_ref, *, tm, gw):
    i = pl.program_id(0)
    groups = len(POOL_WINDOWS)
    base = POOL_PAD + POOL_HALO
    n = POOL_HALO + tm
    width = ext_ref.shape[1]
    for ref in (ext_ref, lvl_a_ref, lvl_b_ref):
        ref[0:POOL_PAD, :] = jnp.zeros((POOL_PAD, width), F32)
    ext_ref[POOL_PAD:base, :] = jnp.where(i > 0, halo_ref[...], 0.0)
    ext_ref[base:base + tm, :] = u_ref[...]
    row = lax.broadcasted_iota(jnp.int32, (tm, 1), 0) + i * tm
    src = ext_ref
    for k, w in enumerate(POOL_WINDOWS):
        span = w // 2
        cols = slice(k * gw, (k + 1) * gw)
        u = u_ref[:, cols]
        win_sum = src[base:base + tm, cols] + src[base - span:base - span + tm, cols]
        inv_cnt = 1.0 / jnp.minimum(row + 1, w).astype(F32)
        pooled = win_sum * inv_cnt - u
        mixed = jnp.dot(pooled.astype(BF16), w_ref[k], preferred_element_type=F32)
        o_ref[:, cols] = (mixed * scale_ref[:, cols]).astype(o_ref.dtype)
        if k + 1 < groups:
            rest = slice((k + 1) * gw, width)
            dst = lvl_a_ref if k % 2 == 0 else lvl_b_ref
            dst[POOL_PAD:POOL_PAD + n, rest] = (src[POOL_PAD:POOL_PAD + n, rest]
                                                + src[POOL_PAD - span:POOL_PAD - span + n, rest])
            src = dst


def _pool_mixer(z, w_pool, pool_scale, *, pool_width, tm=512):
    s = z.shape[0]
    groups, gw, _ = w_pool.shape
    assert all(w == 2 ** (k + 1) for k, w in enumerate(POOL_WINDOWS)) and POOL_WINDOWS[-1] // 2 <= POOL_PAD
    halo_blocks = tm // POOL_HALO
    buf = (POOL_PAD + POOL_HALO + tm, pool_width)
    return pl.pallas_call(
        functools.partial(_pool_mixer_kernel, tm=tm, gw=gw),
        grid=(s // tm,),
        in_specs=[
            pl.BlockSpec((POOL_HALO, pool_width), lambda i: (jnp.maximum(i * halo_blocks - 1, 0), 0)),
            pl.BlockSpec((tm, pool_width), lambda i: (i, 0)),
            pl.BlockSpec((groups, gw, gw), lambda i: (0, 0, 0)),
            pl.BlockSpec((1, pool_width), lambda i: (0, 0)),
        ],
        out_specs=pl.BlockSpec((tm, pool_width), lambda i: (i, 0)),
        out_shape=jax.ShapeDtypeStruct((s, pool_width), BF16),
        scratch_shapes=[pltpu.VMEM(buf, F32)] * 3,
        compiler_params=_params(("parallel",),
                                [((tm, pool_width), F32), ((groups, gw, gw), BF16), ((tm, pool_width), BF16)],
                                scratch=[(buf, F32)] * 3,
                                temps=[((tm, pool_width), F32)]),
        name="pool_mixer",
    )(z, z, w_pool, pool_scale)


def _q_proj_kernel(ql_ref, g_ref, w_ref, pos_ref, freq_ref, o_ref, qn_ref, cos_ref, sin_ref, *, scale):
    @pl.when(pl.program_id(1) == 0)
    def _():
        qn_ref[...] = (_rms(ql_ref[...], g_ref[...]) * scale).astype(qn_ref.dtype)
        ang = freq_ref[...] * pos_ref[...]
        cos_ref[...] = jnp.cos(ang)
        sin_ref[...] = jnp.sin(ang)

    qt = lax.dot_general(w_ref[...], qn_ref[...], (((1,), (1,)), ((), ())), preferred_element_type=F32)
    half = QK_ROPE_DIM // 2
    r0, r1, r2 = QK_NOPE_DIM, QK_NOPE_DIM + half, QK_NOPE_DIM + QK_ROPE_DIM
    c, sn = cos_ref[...], sin_ref[...]
    rows = o_ref.shape[2]
    for hh in range(o_ref.shape[0]):
        q = qt[hh * rows:(hh + 1) * rows]
        x1, x2 = q[r0:r1], q[r1:r2]
        o_ref[hh, 0, 0:r0, :] = q[0:r0].astype(o_ref.dtype)
        o_ref[hh, 0, r0:r1, :] = (x1 * c - x2 * sn).astype(o_ref.dtype)
        o_ref[hh, 0, r1:r2, :] = (x1 * sn + x2 * c).astype(o_ref.dtype)
        o_ref[hh, 0, r2:, :] = q[r2:].astype(o_ref.dtype)


def _q_proj(z, q_norm, wq_t, pos_row, freq_col, *, col_block, rank, scale, tm, heads_per_step=8):
    s = z.shape[0]
    rows = V7X_MXU_DIM
    n_heads = wq_t.shape[0] // rows
    hps = heads_per_step
    half = QK_ROPE_DIM // 2
    return pl.pallas_call(
        functools.partial(_q_proj_kernel, scale=scale),
        grid=(s // tm, n_heads // hps),
        in_specs=[
            pl.BlockSpec((tm, rank), lambda i, h: (i, col_block)),
            pl.BlockSpec((1, rank), lambda i, h: (0, 0)),
            pl.BlockSpec((hps * rows, rank), lambda i, h: (h, 0)),
            pl.BlockSpec((1, tm), lambda i, h: (0, i)),
            pl.BlockSpec((half, 1), lambda i, h: (0, 0)),
        ],
        out_specs=pl.BlockSpec((hps, 1, rows, tm), lambda i, h: (h, i, 0, 0)),
        out_shape=jax.ShapeDtypeStruct((n_heads, s // tm, rows, tm), BF16),
        scratch_shapes=[pltpu.VMEM((tm, rank), BF16), pltpu.VMEM((half, tm), F32), pltpu.VMEM((half, tm), F32)],
        compiler_params=_params(("parallel", "arbitrary"),
                                [((tm, rank), F32), ((hps * rows, rank), BF16), ((hps * rows, tm), BF16)],
                                scratch=[((tm, rank), BF16), ((2 * half, tm), F32)],
                                temps=[((tm, rank), F32), ((hps * rows, tm), F32)]),
        name="q_proj",
    )(z, q_norm, wq_t, pos_row, freq_col)


def _kv_proj_kernel(kvl_ref, h_ref, wkr_ref, g_ref, wk_ref, wvt_ref, pos_ref, freq_ref, k_ref, vt_ref, kro_ref):
    kvn = _rms(kvl_ref[...], g_ref[...]).astype(BF16)
    k_ref[...] = jnp.dot(kvn, wk_ref[...], preferred_element_type=F32).astype(k_ref.dtype)
    vt = lax.dot_general(wvt_ref[...], kvn, (((1,), (1,)), ((), ())), preferred_element_type=F32)
    vt_ref[:, 0] = vt.reshape(vt_ref.shape[0], V_HEAD_DIM, vt.shape[1]).astype(vt_ref.dtype)
    half = QK_ROPE_DIM // 2
    wkr = _load_bf16(wkr_ref)
    wkr = jnp.concatenate([wkr, jnp.zeros((V7X_LANES - wkr.shape[0], wkr.shape[1]), BF16)], axis=0)
    x = lax.dot_general(h_ref[...], wkr, (((1,), (1,)), ((), ())), preferred_element_type=F32)
    ang = pos_ref[...] * freq_ref[...]
    lane = lax.broadcasted_iota(jnp.int32, x.shape, 1)
    x2_at_lo = pltpu.roll(x, V7X_LANES - half, axis=1)
    x1_at_hi = pltpu.roll(x, half, axis=1)
    partner = jnp.where(lane < half, -x2_at_lo, jnp.where(lane < QK_ROPE_DIM, x1_at_hi, 0.0))
    kro_ref[...] = (x * jnp.cos(ang) + partner * jnp.sin(ang)).astype(kro_ref.dtype)


def _kv_proj(z, h, w_in_t, kv_norm, wk, wv_t, pos_col, freq_row, *, lat_block, rope_row_block, rank, tm):
    s, d = h.shape
    nk = wk.shape[1]
    nv = wv_t.shape[0]
    n_heads = nv // V_HEAD_DIM
    return pl.pallas_call(
        _kv_proj_kernel,
        grid=(s // tm,),
        in_specs=[
            pl.BlockSpec((tm, rank), lambda i: (i, lat_block)),
            pl.BlockSpec((tm, d), lambda i: (i, 0)),
            pl.BlockSpec((QK_ROPE_DIM, d), lambda i: (rope_row_block, 0)),
            pl.BlockSpec((1, rank), lambda i: (0, 0)),
            pl.BlockSpec((rank, nk), lambda i: (0, 0)),
            pl.BlockSpec((nv, rank), lambda i: (0, 0)),
            pl.BlockSpec((tm, 1), lambda i: (i, 0)),
            pl.BlockSpec((1, V7X_LANES), lambda i: (0, 0)),
        ],
        out_specs=[
            pl.BlockSpec((tm, nk), lambda i: (i, 0)),
            pl.BlockSpec((n_heads, 1, V_HEAD_DIM, tm), lambda i: (0, i, 0, 0)),
            pl.BlockSpec((tm, V7X_LANES), lambda i: (i, 0)),
        ],
        out_shape=[
            jax.ShapeDtypeStruct((s, nk), BF16),
            jax.ShapeDtypeStruct((n_heads, s // tm, V_HEAD_DIM, tm), BF16),
            jax.ShapeDtypeStruct((s, V7X_LANES), BF16),
        ],
        compiler_params=_params(("parallel",),
                                [((tm, rank), F32), ((tm, d), BF16), ((QK_ROPE_DIM, d), w_in_t.dtype), ((rank, nk), BF16),
                                 ((nv, rank), BF16), ((tm, nk), BF16), ((nv, tm), BF16), ((tm, V7X_LANES), BF16),
                                 ((tm, V7X_LANES), F32)],
                                temps=[((tm, nk), F32), ((nv, tm), F32), ((V7X_LANES, d), BF16)]),
        name="kv_proj",
    )(z, h, w_in_t, kv_norm, wk, wv_t, pos_col, freq_row)


def _flash_kernel(kn_ref, kr_ref, q_ref, qnext_ref, vt_ref, o_ref, s_ref, smax_ref, bias_ref, m_ref, l_ref, acc_ref,
                  *, t, unroll):
    heads = q_ref.shape[0]
    qi = pl.program_id(1)
    first = 2

    def scores(kb, slot, q=q_ref):
        ks = pl.ds(pl.multiple_of(kb * t, t), t)
        for hh in range(heads):
            kn = kn_ref[ks, hh * QK_NOPE_DIM:(hh + 1) * QK_NOPE_DIM]
            kcat = jnp.concatenate([kn, kr_ref[ks, :]], axis=1)
            s = jnp.dot(kcat, q[hh, 0], preferred_element_type=F32)
            s_ref[hh, slot] = s
            smax_ref[hh, slot] = jnp.max(s, axis=0, keepdims=True)

    def softmax_pv(kb, slot, masked):
        for hh in range(heads):
            s = s_ref[hh, slot]
            if masked:
                s = s + bias_ref[...]
                block_max = jnp.max(s, axis=0, keepdims=True)
            else:
                block_max = smax_ref[hh, slot]
            m_prev = m_ref[hh]
            m_new = jnp.maximum(m_prev, block_max)
            alpha = jnp.exp2(m_prev - m_new)
            p = jnp.exp2(s - m_new)
            l_ref[hh] = alpha * l_ref[hh] + jnp.sum(p, axis=0, keepdims=True)
            pv = jnp.dot(vt_ref[hh, kb], p.astype(BF16), preferred_element_type=F32)
            acc_ref[hh] = alpha * acc_ref[hh] + pv
            m_ref[hh] = m_new

    m_ref[...] = jnp.full(m_ref.shape, -jnp.inf, F32)
    l_ref[...] = jnp.zeros(l_ref.shape, F32)
    acc_ref[...] = jnp.zeros(acc_ref.shape, F32)

    @pl.when(qi == 0)
    def _():
        kc = lax.broadcasted_iota(jnp.int32, (t, t), 0) // CHUNK
        qc = lax.broadcasted_iota(jnp.int32, (t, t), 1) // CHUNK
        bias_ref[...] = jnp.where(kc <= qc, 0.0, -jnp.inf).astype(F32)
        scores(0, first)
        softmax_pv(0, first, masked=True)
        scores(0, first, qnext_ref)

    @pl.when(qi > 0)
    def _():
        scores(1, 1)
        softmax_pv(0, first, masked=False)
        n_groups = (qi - 1) // unroll

        def group(g, c):
            b0 = 1 + g * unroll
            for u in range(unroll):
                scores(b0 + u + 1, u % 2)
                softmax_pv(b0 + u, (u + 1) % 2, masked=False)
            return c

        lax.fori_loop(0, n_groups, group, 0)
        base = 1 + n_groups * unroll
        rest = qi - base
        for r in range(unroll):
            @pl.when(rest == r)
            def _(r=r):
                for u in range(r):
                    scores(base + u + 1, u % 2)
                    softmax_pv(base + u, (u + 1) % 2, masked=False)
                scores(0, first, qnext_ref)
                softmax_pv(qi, (1 + r) % 2, masked=True)

    for hh in range(heads):
        out = (acc_ref[hh] * (1.0 / l_ref[hh])).T
        o_ref[:, hh * V_HEAD_DIM:(hh + 1) * V_HEAD_DIM] = out.astype(o_ref.dtype)


def _flash_attention(k_nope, k_rope, q_t, v_t, *, n_heads, t, unroll=4, heads_per_step=4):
    s = k_nope.shape[0]
    nq = s // t
    hps = heads_per_step
    assert t % CHUNK == 0 and unroll % 2 == 0 and nq >= 2 and n_heads % hps == 0
    assert v_t.shape == (n_heads, nq, V_HEAD_DIM, t) and q_t.shape == (n_heads, nq, V7X_MXU_DIM, t)
    scratch = [((hps, 3, t, t), F32), ((hps, 3, 1, t), F32), ((t, t), F32), ((hps, 1, t), F32), ((hps, 1, t), F32),
               ((hps, V_HEAD_DIM, t), F32)]
    return pl.pallas_call(
        functools.partial(_flash_kernel, t=t, unroll=unroll),
        grid=(n_heads // hps, nq),
        in_specs=[
            pl.BlockSpec((s, hps * QK_NOPE_DIM), lambda h, i: (0, h)),
            pl.BlockSpec((s, V7X_LANES), lambda h, i: (0, 0)),
            pl.BlockSpec((hps, 1, V7X_MXU_DIM, t), lambda h, i: (h, i, 0, 0)),
            pl.BlockSpec((hps, 1, V7X_MXU_DIM, t), lambda h, i: (h, jnp.minimum(i + 1, nq - 1), 0, 0)),
            pl.BlockSpec((hps, nq, V_HEAD_DIM, t), lambda h, i: (h, 0, 0, 0)),
        ],
        out_specs=pl.BlockSpec((t, hps * V_HEAD_DIM), lambda h, i: (i, h)),
        out_shape=jax.ShapeDtypeStruct((s, n_heads * V_HEAD_DIM), BF16),
        scratch_shapes=[pltpu.VMEM(shape, dtype) for shape, dtype in scratch],
        compiler_params=_params(("arbitrary", "arbitrary"),
                                [((s, hps * QK_NOPE_DIM), BF16), ((s, V7X_LANES), BF16),
                                 ((2 * hps * V7X_MXU_DIM, t), BF16), ((hps * V_HEAD_DIM, s), BF16),
                                 ((t, hps * V_HEAD_DIM), BF16)],
                                scratch=scratch,
                                temps=[((t, t), F32), ((t, t), BF16), ((t, V7X_MXU_DIM), BF16)]),
        name="flash_attention",
    )(k_nope, k_rope, q_t, q_t, v_t)


def _gated_merge_kernel(pm_ref, at_ref, h_ref, wup_ref, wum_ref, wga_ref, wgb_ref, wo_ref, o_ref, wo_o_ref):
    h = h_ref[...]
    ya = jnp.dot(pm_ref[...], wup_ref[...], preferred_element_type=F32)
    yb = jnp.dot(at_ref[...], wum_ref[...], preferred_element_type=F32)
    ga = _sigmoid(jnp.dot(h, wga_ref[...], preferred_element_type=F32))
    gb = _sigmoid(jnp.dot(h, wgb_ref[...], preferred_element_type=F32))
    o_ref[...] = (ga * ya + gb * yb).astype(o_ref.dtype)
    wo_o_ref[...] = wo_ref[...].astype(wo_o_ref.dtype)


def _gated_merge(pm, attn, h, w_up_pool, w_up_mla, w_gate2d, w_out, *, tm=1024, tn=256):
    s, d = h.shape
    kp, km = pm.shape[1], attn.shape[1]
    nb = d // tn
    slab_block, slab_spec = _slab_spec(w_out, (s // tm) * nb, nb)
    return pl.pallas_call(
        _gated_merge_kernel,
        grid=(s // tm, nb),
        in_specs=[
            pl.BlockSpec((tm, kp), lambda i, j: (i, 0)),
            pl.BlockSpec((tm, km), lambda i, j: (i, 0)),
            pl.BlockSpec((tm, d), lambda i, j: (i, 0)),
            pl.BlockSpec((kp, tn), lambda i, j: (0, j)),
            pl.BlockSpec((km, tn), lambda i, j: (0, j)),
            pl.BlockSpec((d, tn), lambda i, j: (0, j)),
            pl.BlockSpec((d, tn), lambda i, j: (0, j + nb)),
            slab_spec,
        ],
        out_specs=[pl.BlockSpec((tm, tn), lambda i, j: (i, j)), slab_spec],
        out_shape=[jax.ShapeDtypeStruct((s, d), BF16), jax.ShapeDtypeStruct(w_out.shape, BF16)],
        compiler_params=_params(("parallel", "parallel"),
                                [((tm, kp), BF16), ((tm, km), BF16), ((tm, d), BF16), ((kp, tn), BF16),
                                 ((km, tn), BF16), ((d, tn), BF16), ((d, tn), BF16), ((tm, tn), BF16),
                                 (slab_block, F32), (slab_block, BF16)],
                                temps=[((tm, tn), F32)] * 4),
        name="gated_merge",
    )(pm, attn, h, w_up_pool, w_up_mla, w_gate2d, w_gate2d, w_out)


def _residual_norm_kernel(t_ref, x_ref, gpost_ref, gnext_ref, xo_ref, ho_ref):
    xo = x_ref[...] + _rms(t_ref[...].astype(F32), gpost_ref[...])
    xo_ref[...] = xo
    ho_ref[...] = _rms(xo, gnext_ref[...]).astype(ho_ref.dtype)


def _residual_final_kernel(t_ref, x_ref, gpost_ref, xo_ref):
    xo_ref[...] = x_ref[...] + _rms(t_ref[...].astype(F32), gpost_ref[...])


def _residual_norm(t, x, g_post, g_next=None, *, tm=256):
    s, d = x.shape
    row = pl.BlockSpec((tm, d), lambda i: (i, 0))
    vec = pl.BlockSpec((1, d), lambda i: (0, 0))
    blocks = [((tm, d), t.dtype), ((tm, d), F32), ((tm, d), F32)]
    if g_next is None:
        return pl.pallas_call(
            _residual_final_kernel, grid=(s // tm,), in_specs=[row, row, vec], out_specs=row,
            out_shape=jax.ShapeDtypeStruct((s, d), F32),
            compiler_params=_params(("parallel",), blocks, temps=[((tm, d), F32)] * 2),
            name="residual_final",
        )(t, x, g_post)
    return pl.pallas_call(
        _residual_norm_kernel, grid=(s // tm,), in_specs=[row, row, vec, vec], out_specs=[row, row],
        out_shape=[jax.ShapeDtypeStruct((s, d), F32), jax.ShapeDtypeStruct((s, d), BF16)],
        compiler_params=_params(("parallel",), blocks + [((tm, d), BF16)], temps=[((tm, d), F32)] * 2),
        name="residual_norm",
    )(t, x, g_post, g_next)


def _swiglu_up_kernel(h_ref, wg_ref, wu_ref, wd_ref, o_ref, wd_o_ref):
    h = h_ref[...]
    gate = jnp.dot(h, _load_bf16(wg_ref), preferred_element_type=F32)
    up = jnp.dot(h, _load_bf16(wu_ref), preferred_element_type=F32)
    o_ref[...] = (gate * _sigmoid(gate) * up).astype(o_ref.dtype)
    wd_o_ref[...] = wd_ref[...].astype(wd_o_ref.dtype)


def _swiglu_up(h, w_gate, w_up, w_down, *, tm=1024, tn=256):
    s, d = h.shape
    f = w_gate.shape[1]
    nj = f // tn
    steps = (s // tm) * nj
    slab = w_down.shape[0] // steps
    assert slab * steps == w_down.shape[0] and slab % 16 == 0
    dn = w_down.shape[1]
    return pl.pallas_call(
        _swiglu_up_kernel,
        grid=(s // tm, nj),
        in_specs=[
            pl.BlockSpec((tm, d), lambda i, j: (i, 0)),
            pl.BlockSpec((d, tn), lambda i, j: (0, j)),
            pl.BlockSpec((d, tn), lambda i, j: (0, j)),
            pl.BlockSpec((slab, dn), lambda i, j: (i * nj + j, 0)),
        ],
        out_specs=[pl.BlockSpec((tm, tn), lambda i, j: (i, j)),
                   pl.BlockSpec((slab, dn), lambda i, j: (i * nj + j, 0))],
        out_shape=[jax.ShapeDtypeStruct((s, f), BF16), jax.ShapeDtypeStruct(w_down.shape, BF16)],
        compiler_params=_params(("parallel", "parallel"),
                                [((tm, d), BF16), ((d, tn), w_gate.dtype), ((d, tn), w_up.dtype), ((tm, tn), BF16),
                                 ((slab, dn), w_down.dtype), ((slab, dn), BF16)],
                                temps=[((tm, tn), F32)] * 3 + [((d, tn), BF16)] * 2),
        name="swiglu_up",
    )(h, w_gate, w_up, w_down)


def _ple_gate_kernel(h_ref, p_ref, wg_ref, wp_ref, o_ref):
    gate = _sigmoid(jnp.dot(h_ref[...], _load_bf16(wg_ref), preferred_element_type=F32))
    pe = jnp.dot(_load_bf16(p_ref), _load_bf16(wp_ref), preferred_element_type=F32)
    o_ref[...] = (pe * gate).astype(o_ref.dtype)


def _ple_gate(h, p, w_gate, w_proj, *, tm=1024, tn=512):
    s, d = h.shape
    r = p.shape[1]
    return pl.pallas_call(
        _ple_gate_kernel,
        grid=(s // tm, d // tn),
        in_specs=[
            pl.BlockSpec((tm, d), lambda i, j: (i, 0)),
            pl.BlockSpec((tm, r), lambda i, j: (i, 0)),
            pl.BlockSpec((d, tn), lambda i, j: (0, j)),
            pl.BlockSpec((r, tn), lambda i, j: (0, j)),
        ],
        out_specs=pl.BlockSpec((tm, tn), lambda i, j: (i, j)),
        out_shape=jax.ShapeDtypeStruct((s, d), BF16),
        compiler_params=_params(("parallel", "parallel"),
                                [((tm, d), BF16), ((tm, r), p.dtype), ((d, tn), w_gate.dtype), ((r, tn), w_proj.dtype),
                                 ((tm, tn), BF16)],
                                temps=[((tm, tn), F32)] * 2 + [((d, tn), BF16)]),
        name="ple_gate",
    )(h, p, w_gate, w_proj)


def kernel(x, p, positions, norm_mix_pre, norm_mix_post, w_in, q_norm, kv_norm, w_q_b, w_kv_b, w_pool, pool_scale,
           w_up_pool, w_up_mla, w_branch_gate, w_out, norm_ffn_pre, norm_ffn_post, w_ffn_gate, w_ffn_up,
           w_ffn_down, norm_ple_pre, w_ple_gate, w_ple_proj, norm_ple_post):
    batch, seq, d_model = x.shape
    depth = w_in.shape[0]
    in_width = w_in.shape[2]
    q_rank, n_heads, qk_dim = w_q_b.shape[1:]
    kv_rank = w_kv_b.shape[1]
    pool_width = w_up_pool.shape[1]
    assert batch == 1 and qk_dim == QK_NOPE_DIM + QK_ROPE_DIM
    assert w_kv_b.shape[3] == QK_NOPE_DIM + V_HEAD_DIM
    o_q, o_kv, o_kr = pool_width, pool_width + q_rank, pool_width + q_rank + kv_rank
    assert o_kr + QK_ROPE_DIM == in_width
    assert o_q % q_rank == 0 and o_kv % kv_rank == 0 and o_kr % QK_ROPE_DIM == 0
    z_tn = 512
    assert o_kr % z_tn == 0
    attn_t = 512
    half = QK_ROPE_DIM // 2
    scale = float(qk_dim) ** -0.5 * LOG2_E

    inv_freq = ROPE_THETA ** (-jnp.arange(0, QK_ROPE_DIM, 2, dtype=F32) / QK_ROPE_DIM)
    freq_col = inv_freq.reshape(half, 1)
    freq_row = jnp.zeros((1, V7X_LANES), F32).at[0, :half].set(inv_freq).at[0, half:QK_ROPE_DIM].set(inv_freq)
    pos_f = positions.astype(F32)
    pos_row, pos_col = pos_f.reshape(1, seq), pos_f.reshape(seq, 1)

    xs = x.reshape(seq, d_model)
    for i in range(depth):
        w_in_t = jnp.transpose(w_in[i])
        wq = w_q_b[i].astype(BF16)
        wq_t = jnp.pad(jnp.transpose(wq, (1, 2, 0)), ((0, 0), (0, V7X_MXU_DIM - qk_dim), (0, 0)))
        wq_t = wq_t.reshape(n_heads * V7X_MXU_DIM, q_rank)
        wkv = w_kv_b[i].astype(BF16)
        wk = wkv[:, :, :QK_NOPE_DIM].reshape(kv_rank, n_heads * QK_NOPE_DIM)
        wv_t = jnp.transpose(wkv[:, :, QK_NOPE_DIM:], (1, 2, 0)).reshape(n_heads * V_HEAD_DIM, kv_rank)
        w_up_p, w_up_m = _cast_pair(w_up_pool[i], w_up_mla[i])

        def row(v):
            return v[i].reshape(1, -1)

        h = _rmsnorm_cast(xs, row(norm_mix_pre))
        z = _matmul(h, w_in_t, tm=1024, tn=z_tn, out_dtype=F32, name="in_proj", b_is_nk=True, n=o_kr)
        pm = _pool_mixer(z, w_pool[i].astype(BF16), row(pool_scale), pool_width=pool_width)
        q_t = _q_proj(z, row(q_norm), wq_t, pos_row, freq_col, col_block=o_q // q_rank, rank=q_rank, scale=scale,
                      tm=attn_t)
        k_nope, v_t, k_rope = _kv_proj(z, h, w_in_t, row(kv_norm), wk, wv_t, pos_col, freq_row,
                                       lat_block=o_kv // kv_rank, rope_row_block=o_kr // QK_ROPE_DIM, rank=kv_rank,
                                       tm=attn_t)
        attn = _flash_attention(k_nope, k_rope, q_t, v_t, n_heads=n_heads, t=attn_t)
        w_gate2d = w_branch_gate[i].astype(BF16).reshape(d_model, 2 * d_model)
        merged, w_out_b = _gated_merge(pm, attn, h, w_up_p, w_up_m, w_gate2d, w_out[i])
        mix = _matmul(merged, w_out_b, tm=1024, tn=1024, out_dtype=BF16, name="out_proj")
        xs, h2 = _residual_norm(mix, xs, row(norm_mix_post), row(norm_ffn_pre))
        act, w_down = _swiglu_up(h2, w_ffn_gate[i], w_ffn_up[i], w_ffn_down[i])
        ffn, w_pg = _matmul(act, w_down, tm=512, tn=512, out_dtype=BF16, name="ffn_down", side_cast=w_ple_gate[i])
        xs, h3 = _residual_norm(ffn, xs, row(norm_ffn_post), row(norm_ple_pre))
        t = _ple_gate(h3, p[i].reshape(seq, -1), w_pg, w_ple_proj[i], tn=1024)
        xs = _residual_norm(t, xs, row(norm_ple_post))
    return xs.reshape(batch, seq, d_model)
```

```python
import functools

import jax
import jax.numpy as jnp
from jax import lax
from jax.experimental import pallas as pl
from jax.experimental.pallas import tpu as pltpu

CHUNK = 64
EPS = 1e-6
POOL_WINDOWS = (2, 4, 8, 16)
QK_NOPE_DIM = 128
QK_ROPE_DIM = 64
V_HEAD_DIM = 128
ROPE_THETA = 10000.0
LOG2_E = 1.4426950408889634

V7X_LANES = 128
V7X_MXU_DIM = 256
V7X_VMEM_BYTES = 64 * 1024 * 1024
V7X_VMEM_USABLE_BYTES = 60000 * 1024
V7X_VMEM_DEFAULT_SCOPED_BYTES = 32 * 1024 * 1024

POOL_HALO = 16
POOL_PAD = 8
F32 = jnp.float32
BF16 = jnp.bfloat16


def _nbytes(shape, dtype):
    n = 1
    for s in shape:
        n *= s
    return n * jnp.dtype(dtype).itemsize


def _params(semantics, blocks, scratch=(), temps=()):
    need = 2 * sum(_nbytes(s, d) for s, d in blocks)
    need += sum(_nbytes(s, d) for s, d in scratch) + sum(_nbytes(s, d) for s, d in temps)
    need = max(int(need * 1.25) + (2 << 20), V7X_VMEM_DEFAULT_SCOPED_BYTES)
    return pltpu.CompilerParams(dimension_semantics=semantics,
                                vmem_limit_bytes=min(need, V7X_VMEM_USABLE_BYTES))


def _load_bf16(ref):
    v = ref[...]
    return v if v.dtype == BF16 else v.astype(BF16)


def _sigmoid(x):
    return 0.5 * jnp.tanh(0.5 * x) + 0.5


def _rms(xf, g):
    return xf * lax.rsqrt(jnp.mean(xf * xf, axis=-1, keepdims=True) + EPS) * g


NORM_ROW_CHUNK = 32


def _for_row_chunks(n_rows, body):
    def trip(c, carry):
        body(pl.ds(pl.multiple_of(c * NORM_ROW_CHUNK, NORM_ROW_CHUNK), NORM_ROW_CHUNK))
        return carry

    lax.fori_loop(0, n_rows // NORM_ROW_CHUNK, trip, 0)


def _rmsnorm_cast_kernel(x_ref, g_ref, o_ref):
    def rows(r):
        o_ref[r, :] = _rms(x_ref[r, :], g_ref[...]).astype(o_ref.dtype)

    _for_row_chunks(x_ref.shape[0], rows)


def _rmsnorm_cast(x, g, *, tm=512):
    s, d = x.shape
    return pl.pallas_call(
        _rmsnorm_cast_kernel,
        grid=(s // tm,),
        in_specs=[pl.BlockSpec((tm, d), lambda i: (i, 0)), pl.BlockSpec((1, d), lambda i: (0, 0))],
        out_specs=pl.BlockSpec((tm, d), lambda i: (i, 0)),
        out_shape=jax.ShapeDtypeStruct((s, d), BF16),
        compiler_params=_params(("parallel",), [((tm, d), F32), ((tm, d), BF16)], temps=[((tm, d), F32)]),
        name="rmsnorm_cast",
    )(x, g)


def _cast_pair_kernel(a_ref, b_ref, ao_ref, bo_ref):
    ao_ref[...] = a_ref[...].astype(ao_ref.dtype)
    bo_ref[...] = b_ref[...].astype(bo_ref.dtype)


def _cast_pair(a, b, *, rows=256):
    k, n = a.shape
    assert b.shape == a.shape
    spec = pl.BlockSpec((rows, n), lambda i: (i, 0))
    return pl.pallas_call(
        _cast_pair_kernel,
        grid=(k // rows,),
        in_specs=[spec, spec],
        out_specs=[spec, spec],
        out_shape=[jax.ShapeDtypeStruct((k, n), BF16)] * 2,
        compiler_params=_params(("parallel",), [((rows, n), F32), ((rows, n), BF16)] * 2),
        name="cast_up_weights",
    )(a, b)


def _slab_spec(w, steps, nj):
    slab = w.shape[0] // steps
    assert slab * steps == w.shape[0] and slab % 16 == 0
    return (slab, w.shape[1]), pl.BlockSpec((slab, w.shape[1]), lambda i, j: (i * nj + j, 0))


def _matmul_kernel(a_ref, b_ref, *rest, b_is_nk):
    o_ref = rest[-2] if len(rest) == 3 else rest[0]
    contract_b = 1 if b_is_nk else 0
    o_ref[...] = lax.dot_general(a_ref[...], _load_bf16(b_ref), (((1,), (contract_b,)), ((), ())),
                                 preferred_element_type=F32).astype(o_ref.dtype)
    if len(rest) == 3:
        rest[2][...] = rest[0][...].astype(rest[2].dtype)


def _matmul(a, b, *, tm, tn, out_dtype, name, b_is_nk=False, n=None, side_cast=None):
    m, k = a.shape
    if n is None:
        n = b.shape[0] if b_is_nk else b.shape[1]
    if b_is_nk:
        b_block, b_spec = (tn, k), pl.BlockSpec((tn, k), lambda i, j: (j, 0))
    else:
        b_block, b_spec = (k, tn), pl.BlockSpec((k, tn), lambda i, j: (0, j))
    grid = (m // tm, n // tn)
    in_specs = [pl.BlockSpec((tm, k), lambda i, j: (i, 0)), b_spec]
    out_specs = pl.BlockSpec((tm, tn), lambda i, j: (i, j))
    out_shape = jax.ShapeDtypeStruct((m, n), out_dtype)
    blocks = [((tm, k), a.dtype), (b_block, b.dtype), ((tm, tn), out_dtype)]
    args = (a, b)
    if side_cast is not None:
        slab_block, slab_spec = _slab_spec(side_cast, grid[0] * grid[1], grid[1])
        in_specs.append(slab_spec)
        out_specs = [out_specs, slab_spec]
        out_shape = [out_shape, jax.ShapeDtypeStruct(side_cast.shape, BF16)]
        blocks += [(slab_block, F32), (slab_block, BF16)]
        args = (a, b, side_cast)
    return pl.pallas_call(
        functools.partial(_matmul_kernel, b_is_nk=b_is_nk),
        grid=grid, in_specs=in_specs, out_specs=out_specs, out_shape=out_shape,
        compiler_params=_params(("parallel", "parallel"), blocks, temps=[((tm, tn), F32), ((k, tn), BF16)]),
        name=name,
    )(*args)


def _pool_mixer_kernel(halo_ref, u_ref, w_ref, scale_ref, o_ref, ext_ref, lvl_a_ref, lvl_b_ref, *, tm, gw):
    i = pl.program_id(0)
    groups = len(POOL_WINDOWS)
    base = POOL_PAD + POOL_HALO
    n = POOL_HALO + tm
    width = ext_ref.shape[1]
    for ref in (ext_ref, lvl_a_ref, lvl_b_ref):
        ref[0:POOL_PAD, :] = jnp.zeros((POOL_PAD, width), F32)
    ext_ref[POOL_PAD:base, :] = jnp.where(i > 0, halo_ref[...], 0.0)
    ext_ref[base:base + tm, :] = u_ref[...]
    row = lax.broadcasted_iota(jnp.int32, (tm, 1), 0) + i * tm
    src = ext_ref
    for k, w in enumerate(POOL_WINDOWS):
        span = w // 2
        cols = slice(k * gw, (k + 1) * gw)
        u = u_ref[:, cols]
        win_sum = src[base:base + tm, cols] + src[base - span:base - span + tm, cols]
        inv_cnt = 1.0 / jnp.minimum(row + 1, w).astype(F32)
        pooled = win_sum * inv_cnt - u
        mixed = jnp.dot(pooled.astype(BF16), w_ref[k], preferred_element_type=F32)
        o_ref[:, cols] = (mixed * scale_ref[:, cols]).astype(o_ref.dtype)
        if k + 1 < groups:
            rest = slice((k + 1) * gw, width)
            dst = lvl_a_ref if k % 2 == 0 else lvl_b_ref
            dst[POOL_PAD:POOL_PAD + n, rest] = (src[POOL_PAD:POOL_PAD + n, rest]
                                                + src[POOL_PAD - span:POOL_PAD - span + n, rest])
            src = dst


def _pool_mixer(z, w_pool, pool_scale, *, pool_width, tm=512):
    s = z.shape[0]
    groups, gw, _ = w_pool.shape
    assert all(w == 2 ** (k + 1) for k, w in enumerate(POOL_WINDOWS)) and POOL_WINDOWS[-1] // 2 <= POOL_PAD
    halo_blocks = tm // POOL_HALO
    buf = (POOL_PAD + POOL_HALO + tm, pool_width)
    return pl.pallas_call(
        functools.partial(_pool_mixer_kernel, tm=tm, gw=gw),
        grid=(s // tm,),
        in_specs=[
            pl.BlockSpec((POOL_HALO, pool_width), lambda i: (jnp.maximum(i * halo_blocks - 1, 0), 0)),
            pl.BlockSpec((tm, pool_width), lambda i: (i, 0)),
            pl.BlockSpec((groups, gw, gw), lambda i: (0, 0, 0)),
            pl.BlockSpec((1, pool_width), lambda i: (0, 0)),
        ],
        out_specs=pl.BlockSpec((tm, pool_width), lambda i: (i, 0)),
        out_shape=jax.ShapeDtypeStruct((s, pool_width), BF16),
        scratch_shapes=[pltpu.VMEM(buf, F32)] * 3,
        compiler_params=_params(("parallel",),
                                [((tm, pool_width), F32), ((groups, gw, gw), BF16), ((tm, pool_width), BF16)],
                                scratch=[(buf, F32)] * 3,
                                temps=[((tm, pool_width), F32)]),
        name="pool_mixer",
    )(z, z, w_pool, pool_scale)


def _q_proj_kernel(ql_ref, g_ref, w_ref, pos_ref, freq_ref, o_ref, qn_ref, cos_ref, sin_ref, *, scale):
    @pl.when(pl.program_id(1) == 0)
    def _():
        qn_ref[...] = (_rms(ql_ref[...], g_ref[...]) * scale).astype(qn_ref.dtype)
        ang = freq_ref[...] * pos_ref[...]
        cos_ref[...] = jnp.cos(ang)
        sin_ref[...] = jnp.sin(ang)

    qt = lax.dot_general(w_ref[...], qn_ref[...], (((1,), (1,)), ((), ())), preferred_element_type=F32)
    half = QK_ROPE_DIM // 2
    r0, r1, r2 = QK_NOPE_DIM, QK_NOPE_DIM + half, QK_NOPE_DIM + QK_ROPE_DIM
    c, sn = cos_ref[...], sin_ref[...]
    rows = o_ref.shape[2]
    for hh in range(o_ref.shape[0]):
        q = qt[hh * rows:(hh + 1) * rows]
        x1, x2 = q[r0:r1], q[r1:r2]
        o_ref[hh, 0, 0:r0, :] = q[0:r0].astype(o_ref.dtype)
        o_ref[hh, 0, r0:r1, :] = (x1 * c - x2 * sn).astype(o_ref.dtype)
        o_ref[hh, 0, r1:r2, :] = (x1 * sn + x2 * c).astype(o_ref.dtype)
        o_ref[hh, 0, r2:, :] = q[r2:].astype(o_ref.dtype)


def _q_proj(z, q_norm, wq_t, pos_row, freq_col, *, col_block, rank, scale, tm, heads_per_step=8):
    s = z.shape[0]
    rows = V7X_MXU_DIM
    n_heads = wq_t.shape[0] // rows
    hps = heads_per_step
    half = QK_ROPE_DIM // 2
    return pl.pallas_call(
        functools.partial(_q_proj_kernel, scale=scale),
        grid=(s // tm, n_heads // hps),
        in_specs=[
            pl.BlockSpec((tm, rank), lambda i, h: (i, col_block)),
            pl.BlockSpec((1, rank), lambda i, h: (0, 0)),
            pl.BlockSpec((hps * rows, rank), lambda i, h: (h, 0)),
            pl.BlockSpec((1, tm), lambda i, h: (0, i)),
            pl.BlockSpec((half, 1), lambda i, h: (0, 0)),
        ],
        out_specs=pl.BlockSpec((hps, 1, rows, tm), lambda i, h: (h, i, 0, 0)),
        out_shape=jax.ShapeDtypeStruct((n_heads, s // tm, rows, tm), BF16),
        scratch_shapes=[pltpu.VMEM((tm, rank), BF16), pltpu.VMEM((half, tm), F32), pltpu.VMEM((half, tm), F32)],
        compiler_params=_params(("parallel", "arbitrary"),
                                [((tm, rank), F32), ((hps * rows, rank), BF16), ((hps * rows, tm), BF16)],
                                scratch=[((tm, rank), BF16), ((2 * half, tm), F32)],
                                temps=[((tm, rank), F32), ((hps * rows, tm), F32)]),
        name="q_proj",
    )(z, q_norm, wq_t, pos_row, freq_col)


def _kv_proj_kernel(kvl_ref, h_ref, wkr_ref, g_ref, wk_ref, wvt_ref, pos_ref, freq_ref, k_ref, vt_ref, kro_ref):
    kvn = _rms(kvl_ref[...], g_ref[...]).astype(BF16)
    k_ref[...] = jnp.dot(kvn, wk_ref[...], preferred_element_type=F32).astype(k_ref.dtype)
    vt = lax.dot_general(wvt_ref[...], kvn, (((1,), (1,)), ((), ())), preferred_element_type=F32)
    vt_ref[:, 0] = vt.reshape(vt_ref.shape[0], V_HEAD_DIM, vt.shape[1]).astype(vt_ref.dtype)
    half = QK_ROPE_DIM // 2
    wkr = _load_bf16(wkr_ref)
    wkr = jnp.concatenate([wkr, jnp.zeros((V7X_LANES - wkr.shape[0], wkr.shape[1]), BF16)], axis=0)
    x = lax.dot_general(h_ref[...], wkr, (((1,), (1,)), ((), ())), preferred_element_type=F32)
    ang = pos_ref[...] * freq_ref[...]
    lane = lax.broadcasted_iota(jnp.int32, x.shape, 1)
    x2_at_lo = pltpu.roll(x, V7X_LANES - half, axis=1)
    x1_at_hi = pltpu.roll(x, half, axis=1)
    partner = jnp.where(lane < half, -x2_at_lo, jnp.where(lane < QK_ROPE_DIM, x1_at_hi, 0.0))
    kro_ref[...] = (x * jnp.cos(ang) + partner * jnp.sin(ang)).astype(kro_ref.dtype)


def _kv_proj(z, h, w_in_t, kv_norm, wk, wv_t, pos_col, freq_row, *, lat_block, rope_row_block, rank, tm):
    s, d = h.shape
    nk = wk.shape[1]
    nv = wv_t.shape[0]
    n_heads = nv // V_HEAD_DIM
    return pl.pallas_call(
        _kv_proj_kernel,
        grid=(s // tm,),
        in_specs=[
            pl.BlockSpec((tm, rank), lambda i: (i, lat_block)),
            pl.BlockSpec((tm, d), lambda i: (i, 0)),
            pl.BlockSpec((QK_ROPE_DIM, d), lambda i: (rope_row_block, 0)),
            pl.BlockSpec((1, rank), lambda i: (0, 0)),
            pl.BlockSpec((rank, nk), lambda i: (0, 0)),
            pl.BlockSpec((nv, rank), lambda i: (0, 0)),
            pl.BlockSpec((tm, 1), lambda i: (i, 0)),
            pl.BlockSpec((1, V7X_LANES), lambda i: (0, 0)),
        ],
        out_specs=[
            pl.BlockSpec((tm, nk), lambda i: (i, 0)),
            pl.BlockSpec((n_heads, 1, V_HEAD_DIM, tm), lambda i: (0, i, 0, 0)),
            pl.BlockSpec((tm, V7X_LANES), lambda i: (i, 0)),
        ],
        out_shape=[
            jax.ShapeDtypeStruct((s, nk), BF16),
            jax.ShapeDtypeStruct((n_heads, s // tm, V_HEAD_DIM, tm), BF16),
            jax.ShapeDtypeStruct((s, V7X_LANES), BF16),
        ],
        compiler_params=_params(("parallel",),
                                [((tm, rank), F32), ((tm, d), BF16), ((QK_ROPE_DIM, d), w_in_t.dtype), ((rank, nk), BF16),
                                 ((nv, rank), BF16), ((tm, nk), BF16), ((nv, tm), BF16), ((tm, V7X_LANES), BF16),
                                 ((tm, V7X_LANES), F32)],
                                temps=[((tm, nk), F32), ((nv, tm), F32), ((V7X_LANES, d), BF16)]),
        name="kv_proj",
    )(z, h, w_in_t, kv_norm, wk, wv_t, pos_col, freq_row)


def _flash_kernel(kn_ref, kr_ref, q_ref, qnext_ref, vt_ref, o_ref, s_ref, smax_ref, bias_ref, m_ref, l_ref, acc_ref,
                  *, t, unroll):
    heads = q_ref.shape[0]
    qi = pl.program_id(1)
    first = 2

    def scores(kb, slot, q=q_ref):
        ks = pl.ds(pl.multiple_of(kb * t, t), t)
        for hh in range(heads):
            kn = kn_ref[ks, hh * QK_NOPE_DIM:(hh + 1) * QK_NOPE_DIM]
            kcat = jnp.concatenate([kn, kr_ref[ks, :]], axis=1)
            s = jnp.dot(kcat, q[hh, 0], preferred_element_type=F32)
            s_ref[hh, slot] = s
            smax_ref[hh, slot] = jnp.max(s, axis=0, keepdims=True)

    def softmax_pv(kb, slot, masked):
        for hh in range(heads):
            s = s_ref[hh, slot]
            if masked:
                s = s + bias_ref[...]
                block_max = jnp.max(s, axis=0, keepdims=True)
            else:
                block_max = smax_ref[hh, slot]
            m_prev = m_ref[hh]
            m_new = jnp.maximum(m_prev, block_max)
            alpha = jnp.exp2(m_prev - m_new)
            p = jnp.exp2(s - m_new)
            l_ref[hh] = alpha * l_ref[hh] + jnp.sum(p, axis=0, keepdims=True)
            pv = jnp.dot(vt_ref[hh, kb], p.astype(BF16), preferred_element_type=F32)
            acc_ref[hh] = alpha * acc_ref[hh] + pv
            m_ref[hh] = m_new

    m_ref[...] = jnp.full(m_ref.shape, -jnp.inf, F32)
    l_ref[...] = jnp.zeros(l_ref.shape, F32)
    acc_ref[...] = jnp.zeros(acc_ref.shape, F32)

    @pl.when(qi == 0)
    def _():
        kc = lax.broadcasted_iota(jnp.int32, (t, t), 0) // CHUNK
        qc = lax.broadcasted_iota(jnp.int32, (t, t), 1) // CHUNK
        bias_ref[...] = jnp.where(kc <= qc, 0.0, -jnp.inf).astype(F32)
        scores(0, first)
        softmax_pv(0, first, masked=True)
        scores(0, first, qnext_ref)

    @pl.when(qi > 0)
    def _():
        scores(1, 1)
        softmax_pv(0, first, masked=False)
        n_groups = (qi - 1) // unroll

        def group(g, c):
            b0 = 1 + g * unroll
            for u in range(unroll):
                scores(b0 + u + 1, u % 2)
                softmax_pv(b0 + u, (u + 1) % 2, masked=False)
            return c

        lax.fori_loop(0, n_groups, group, 0)
        base = 1 + n_groups * unroll
        rest = qi - base
        for r in range(unroll):
            @pl.when(rest == r)
            def _(r=r):
                for u in range(r):
                    scores(base + u + 1, u % 2)
                    softmax_pv(base + u, (u + 1) % 2, masked=False)
                scores(0, first, qnext_ref)
                softmax_pv(qi, (1 + r) % 2, masked=True)

    for hh in range(heads):
        out = (acc_ref[hh] * (1.0 / l_ref[hh])).T
        o_ref[:, hh * V_HEAD_DIM:(hh + 1) * V_HEAD_DIM] = out.astype(o_ref.dtype)


def _flash_attention(k_nope, k_rope, q_t, v_t, *, n_heads, t, unroll=2, heads_per_step=4):
    s = k_nope.shape[0]
    nq = s // t
    hps = heads_per_step
    assert t % CHUNK == 0 and unroll % 2 == 0 and nq >= 2 and n_heads % hps == 0
    assert v_t.shape == (n_heads, nq, V_HEAD_DIM, t) and q_t.shape == (n_heads, nq, V7X_MXU_DIM, t)
    scratch = [((hps, 3, t, t), F32), ((hps, 3, 1, t), F32), ((t, t), F32), ((hps, 1, t), F32), ((hps, 1, t), F32),
               ((hps, V_HEAD_DIM, t), F32)]
    return pl.pallas_call(
        functools.partial(_flash_kernel, t=t, unroll=unroll),
        grid=(n_heads // hps, nq),
        in_specs=[
            pl.BlockSpec((s, hps * QK_NOPE_DIM), lambda h, i: (0, h)),
            pl.BlockSpec((s, V7X_LANES), lambda h, i: (0, 0)),
            pl.BlockSpec((hps, 1, V7X_MXU_DIM, t), lambda h, i: (h, i, 0, 0)),
            pl.BlockSpec((hps, 1, V7X_MXU_DIM, t), lambda h, i: (h, jnp.minimum(i + 1, nq - 1), 0, 0)),
            pl.BlockSpec((hps, nq, V_HEAD_DIM, t), lambda h, i: (h, 0, 0, 0)),
        ],
        out_specs=pl.BlockSpec((t, hps * V_HEAD_DIM), lambda h, i: (i, h)),
        out_shape=jax.ShapeDtypeStruct((s, n_heads * V_HEAD_DIM), BF16),
        scratch_shapes=[pltpu.VMEM(shape, dtype) for shape, dtype in scratch],
        compiler_params=_params(("arbitrary", "arbitrary"),
                                [((s, hps * QK_NOPE_DIM), BF16), ((s, V7X_LANES), BF16),
                                 ((2 * hps * V7X_MXU_DIM, t), BF16), ((hps * V_HEAD_DIM, s), BF16),
                                 ((t, hps * V_HEAD_DIM), BF16)],
                                scratch=scratch,
                                temps=[((t, t), F32), ((t, t), BF16), ((t, V7X_MXU_DIM), BF16)]),
        name="flash_attention",
    )(k_nope, k_rope, q_t, q_t, v_t)


def _gated_merge_kernel(pm_ref, at_ref, h_ref, wup_ref, wum_ref, wga_ref, wgb_ref, wo_ref, o_ref, wo_o_ref):
    h = h_ref[...]
    ya = jnp.dot(pm_ref[...], wup_ref[...], preferred_element_type=F32)
    yb = jnp.dot(at_ref[...], wum_ref[...], preferred_element_type=F32)
    ga = _sigmoid(jnp.dot(h, wga_ref[...], preferred_element_type=F32))
    gb = _sigmoid(jnp.dot(h, wgb_ref[...], preferred_element_type=F32))
    o_ref[...] = (ga * ya + gb * yb).astype(o_ref.dtype)
    wo_o_ref[...] = wo_ref[...].astype(wo_o_ref.dtype)


def _gated_merge(pm, attn, h, w_up_pool, w_up_mla, w_gate2d, w_out, *, tm=1024, tn=256):
    s, d = h.shape
    kp, km = pm.shape[1], attn.shape[1]
    nb = d // tn
    slab_block, slab_spec = _slab_spec(w_out, (s // tm) * nb, nb)
    return pl.pallas_call(
        _gated_merge_kernel,
        grid=(s // tm, nb),
        in_specs=[
            pl.BlockSpec((tm, kp), lambda i, j: (i, 0)),
            pl.BlockSpec((tm, km), lambda i, j: (i, 0)),
            pl.BlockSpec((tm, d), lambda i, j: (i, 0)),
            pl.BlockSpec((kp, tn), lambda i, j: (0, j)),
            pl.BlockSpec((km, tn), lambda i, j: (0, j)),
            pl.BlockSpec((d, tn), lambda i, j: (0, j)),
            pl.BlockSpec((d, tn), lambda i, j: (0, j + nb)),
            slab_spec,
        ],
        out_specs=[pl.BlockSpec((tm, tn), lambda i, j: (i, j)), slab_spec],
        out_shape=[jax.ShapeDtypeStruct((s, d), BF16), jax.ShapeDtypeStruct(w_out.shape, BF16)],
        compiler_params=_params(("parallel", "parallel"),
                                [((tm, kp), BF16), ((tm, km), BF16), ((tm, d), BF16), ((kp, tn), BF16),
                                 ((km, tn), BF16), ((d, tn), BF16), ((d, tn), BF16), ((tm, tn), BF16),
                                 (slab_block, F32), (slab_block, BF16)],
                                temps=[((tm, tn), F32)] * 4),
        name="gated_merge",
    )(pm, attn, h, w_up_pool, w_up_mla, w_gate2d, w_gate2d, w_out)


def _residual_norm_kernel(t_ref, x_ref, gpost_ref, gnext_ref, xo_ref, ho_ref):
    def rows(r):
        xo = x_ref[r, :] + _rms(t_ref[r, :].astype(F32), gpost_ref[...])
        xo_ref[r, :] = xo
        ho_ref[r, :] = _rms(xo, gnext_ref[...]).astype(ho_ref.dtype)

    _for_row_chunks(t_ref.shape[0], rows)


def _residual_final_kernel(t_ref, x_ref, gpost_ref, xo_ref):
    def rows(r):
        xo_ref[r, :] = x_ref[r, :] + _rms(t_ref[r, :].astype(F32), gpost_ref[...])

    _for_row_chunks(t_ref.shape[0], rows)


def _residual_norm(t, x, g_post, g_next=None, *, tm=256):
    s, d = x.shape
    row = pl.BlockSpec((tm, d), lambda i: (i, 0))
    vec = pl.BlockSpec((1, d), lambda i: (0, 0))
    blocks = [((tm, d), t.dtype), ((tm, d), F32), ((tm, d), F32)]
    if g_next is None:
        return pl.pallas_call(
            _residual_final_kernel, grid=(s // tm,), in_specs=[row, row, vec], out_specs=row,
            out_shape=jax.ShapeDtypeStruct((s, d), F32),
            compiler_params=_params(("parallel",), blocks, temps=[((tm, d), F32)] * 2),
            name="residual_final",
        )(t, x, g_post)
    return pl.pallas_call(
        _residual_norm_kernel, grid=(s // tm,), in_specs=[row, row, vec, vec], out_specs=[row, row],
        out_shape=[jax.ShapeDtypeStruct((s, d), F32), jax.ShapeDtypeStruct((s, d), BF16)],
        compiler_params=_params(("parallel",), blocks + [((tm, d), BF16)], temps=[((tm, d), F32)] * 2),
        name="residual_norm",
    )(t, x, g_post, g_next)


def _swiglu_up_kernel(h_ref, wg_ref, wu_ref, wd_ref, o_ref, wd_o_ref):
    h = h_ref[...]
    gate = jnp.dot(h, _load_bf16(wg_ref), preferred_element_type=F32)
    up = jnp.dot(h, _load_bf16(wu_ref), preferred_element_type=F32)
    o_ref[...] = (gate * _sigmoid(gate) * up).astype(o_ref.dtype)
    wd_o_ref[...] = wd_ref[...].astype(wd_o_ref.dtype)


def _swiglu_up(h, w_gate, w_up, w_down, *, tm=1024, tn=256):
    s, d = h.shape
    f = w_gate.shape[1]
    nj = f // tn
    steps = (s // tm) * nj
    slab = w_down.shape[0] // steps
    assert slab * steps == w_down.shape[0] and slab % 16 == 0
    dn = w_down.shape[1]
    return pl.pallas_call(
        _swiglu_up_kernel,
        grid=(s // tm, nj),
        in_specs=[
            pl.BlockSpec((tm, d), lambda i, j: (i, 0)),
            pl.BlockSpec((d, tn), lambda i, j: (0, j)),
            pl.BlockSpec((d, tn), lambda i, j: (0, j)),
            pl.BlockSpec((slab, dn), lambda i, j: (i * nj + j, 0)),
        ],
        out_specs=[pl.BlockSpec((tm, tn), lambda i, j: (i, j)),
                   pl.BlockSpec((slab, dn), lambda i, j: (i * nj + j, 0))],
        out_shape=[jax.ShapeDtypeStruct((s, f), BF16), jax.ShapeDtypeStruct(w_down.shape, BF16)],
        compiler_params=_params(("parallel", "parallel"),
                                [((tm, d), BF16), ((d, tn), w_gate.dtype), ((d, tn), w_up.dtype), ((tm, tn), BF16),
                                 ((slab, dn), w_down.dtype), ((slab, dn), BF16)],
                                temps=[((tm, tn), F32)] * 3 + [((d, tn), BF16)] * 2),
        name="swiglu_up",
    )(h, w_gate, w_up, w_down)


def _ple_gate_kernel(h_ref, p_ref, wg_ref, wp_ref, o_ref):
    gate = _sigmoid(jnp.dot(h_ref[...], _load_bf16(wg_ref), preferred_element_type=F32))
    pe = jnp.dot(_load_bf16(p_ref), _load_bf16(wp_ref), preferred_element_type=F32)
    o_ref[...] = (pe * gate).astype(o_ref.dtype)


def _ple_gate(h, p, w_gate, w_proj, *, tm=1024, tn=512):
    s, d = h.shape
    r = p.shape[1]
    return pl.pallas_call(
        _ple_gate_kernel,
        grid=(s // tm, d // tn),
        in_specs=[
            pl.BlockSpec((tm, d), lambda i, j: (i, 0)),
            pl.BlockSpec((tm, r), lambda i, j: (i, 0)),
            pl.BlockSpec((d, tn), lambda i, j: (0, j)),
            pl.BlockSpec((r, tn), lambda i, j: (0, j)),
        ],
        out_specs=pl.BlockSpec((tm, tn), lambda i, j: (i, j)),
        out_shape=jax.ShapeDtypeStruct((s, d), BF16),
        compiler_params=_params(("parallel", "parallel"),
                                [((tm, d), BF16), ((tm, r), p.dtype), ((d, tn), w_gate.dtype), ((r, tn), w_proj.dtype),
                                 ((tm, tn), BF16)],
                                temps=[((tm, tn), F32)] * 2 + [((d, tn), BF16)]),
        name="ple_gate",
    )(h, p, w_gate, w_proj)


def kernel(x, p, positions, norm_mix_pre, norm_mix_post, w_in, q_norm, kv_norm, w_q_b, w_kv_b, w_pool, pool_scale,
           w_up_pool, w_up_mla, w_branch_gate, w_out, norm_ffn_pre, norm_ffn_post, w_ffn_gate, w_ffn_up,
           w_ffn_down, norm_ple_pre, w_ple_gate, w_ple_proj, norm_ple_post):
    batch, seq, d_model = x.shape
    depth = w_in.shape[0]
    in_width = w_in.shape[2]
    q_rank, n_heads, qk_dim = w_q_b.shape[1:]
    kv_rank = w_kv_b.shape[1]
    pool_width = w_up_pool.shape[1]
    assert batch == 1 and qk_dim == QK_NOPE_DIM + QK_ROPE_DIM
    assert w_kv_b.shape[3] == QK_NOPE_DIM + V_HEAD_DIM
    o_q, o_kv, o_kr = pool_width, pool_width + q_rank, pool_width + q_rank + kv_rank
    assert o_kr + QK_ROPE_DIM == in_width
    assert o_q % q_rank == 0 and o_kv % kv_rank == 0 and o_kr % QK_ROPE_DIM == 0
    z_tn = 512
    assert o_kr % z_tn == 0
    attn_t = 512
    half = QK_ROPE_DIM // 2
    scale = float(qk_dim) ** -0.5 * LOG2_E

    inv_freq = ROPE_THETA ** (-jnp.arange(0, QK_ROPE_DIM, 2, dtype=F32) / QK_ROPE_DIM)
    freq_col = inv_freq.reshape(half, 1)
    freq_row = jnp.zeros((1, V7X_LANES), F32).at[0, :half].set(inv_freq).at[0, half:QK_ROPE_DIM].set(inv_freq)
    pos_f = positions.astype(F32)
    pos_row, pos_col = pos_f.reshape(1, seq), pos_f.reshape(seq, 1)

    xs = x.reshape(seq, d_model)
    for i in range(depth):
        w_in_t = jnp.transpose(w_in[i])
        wq = w_q_b[i].astype(BF16)
        wq_t = jnp.pad(jnp.transpose(wq, (1, 2, 0)), ((0, 0), (0, V7X_MXU_DIM - qk_dim), (0, 0)))
        wq_t = wq_t.reshape(n_heads * V7X_MXU_DIM, q_rank)
        wkv = w_kv_b[i].astype(BF16)
        wk = wkv[:, :, :QK_NOPE_DIM].reshape(kv_rank, n_heads * QK_NOPE_DIM)
        wv_t = jnp.transpose(wkv[:, :, QK_NOPE_DIM:], (1, 2, 0)).reshape(n_heads * V_HEAD_DIM, kv_rank)
        w_up_p, w_up_m = _cast_pair(w_up_pool[i], w_up_mla[i])

        def row(v):
            return v[i].reshape(1, -1)

        h = _rmsnorm_cast(xs, row(norm_mix_pre))
        z = _matmul(h, w_in_t, tm=1024, tn=z_tn, out_dtype=F32, name="in_proj", b_is_nk=True, n=o_kr)
        pm = _pool_mixer(z, w_pool[i].astype(BF16), row(pool_scale), pool_width=pool_width)
        q_t = _q_proj(z, row(q_norm), wq_t, pos_row, freq_col, col_block=o_q // q_rank, rank=q_rank, scale=scale,
                      tm=attn_t)
        k_nope, v_t, k_rope = _kv_proj(z, h, w_in_t, row(kv_norm), wk, wv_t, pos_col, freq_row,
                                       lat_block=o_kv // kv_rank, rope_row_block=o_kr // QK_ROPE_DIM, rank=kv_rank,
                                       tm=attn_t)
        attn = _flash_attention(k_nope, k_rope, q_t, v_t, n_heads=n_heads, t=attn_t)
        w_gate2d = w_branch_gate[i].astype(BF16).reshape(d_model, 2 * d_model)
        merged, w_out_b = _gated_merge(pm, attn, h, w_up_p, w_up_m, w_gate2d, w_out[i])
        mix = _matmul(merged, w_out_b, tm=1024, tn=1024, out_dtype=BF16, name="out_proj")
        xs, h2 = _residual_norm(mix, xs, row(norm_mix_post), row(norm_ffn_pre))
        act, w_down = _swiglu_up(h2, w_ffn_gate[i], w_ffn_up[i], w_ffn_down[i])
        ffn, w_pg = _matmul(act, w_down, tm=512, tn=512, out_dtype=BF16, name="ffn_down", side_cast=w_ple_gate[i])
        xs, h3 = _residual_norm(ffn, xs, row(norm_ffn_post), row(norm_ple_pre))
        t = _ple_gate(h3, p[i].reshape(seq, -1), w_pg, w_ple_proj[i], tn=1024)
        xs = _residual_norm(t, xs, row(norm_ple_post))
    return xs.reshape(batch, seq, d_model)
```

```python
import functools

import jax
import jax.numpy as jnp
from jax import lax
from jax.experimental import pallas as pl
from jax.experimental.pallas import tpu as pltpu

CHUNK = 64
EPS = 1e-6
POOL_WINDOWS = (2, 4, 8, 16)
QK_NOPE_DIM = 128
QK_ROPE_DIM = 64
V_HEAD_DIM = 128
ROPE_THETA = 10000.0
LOG2_E = 1.4426950408889634

V7X_LANES = 128
V7X_MXU_DIM = 256
V7X_VMEM_BYTES = 64 * 1024 * 1024
V7X_VMEM_USABLE_BYTES = 60000 * 1024
V7X_VMEM_DEFAULT_SCOPED_BYTES = 32 * 1024 * 1024

POOL_HALO = 16
POOL_PAD = 8
F32 = jnp.float32
BF16 = jnp.bfloat16


def _nbytes(shape, dtype):
    n = 1
    for s in shape:
        n *= s
    return n * jnp.dtype(dtype).itemsize


def _params(semantics, blocks, scratch=(), temps=()):
    need = 2 * sum(_nbytes(s, d) for s, d in blocks)
    need += sum(_nbytes(s, d) for s, d in scratch) + sum(_nbytes(s, d) for s, d in temps)
    need = max(int(need * 1.25) + (2 << 20), V7X_VMEM_DEFAULT_SCOPED_BYTES)
    return pltpu.CompilerParams(dimension_semantics=semantics,
                                vmem_limit_bytes=min(need, V7X_VMEM_USABLE_BYTES))


def _load_bf16(ref):
    v = ref[...]
    return v if v.dtype == BF16 else v.astype(BF16)


def _sigmoid(x):
    return 0.5 * jnp.tanh(0.5 * x) + 0.5


def _rms(xf, g):
    return xf * lax.rsqrt(jnp.mean(xf * xf, axis=-1, keepdims=True) + EPS) * g


def _rmsnorm_cast_kernel(x_ref, g_ref, o_ref):
    o_ref[...] = _rms(x_ref[...], g_ref[...]).astype(o_ref.dtype)


def _rmsnorm_cast(x, g, *, tm=512):
    s, d = x.shape
    return pl.pallas_call(
        _rmsnorm_cast_kernel,
        grid=(s // tm,),
        in_specs=[pl.BlockSpec((tm, d), lambda i: (i, 0)), pl.BlockSpec((1, d), lambda i: (0, 0))],
        out_specs=pl.BlockSpec((tm, d), lambda i: (i, 0)),
        out_shape=jax.ShapeDtypeStruct((s, d), BF16),
        compiler_params=_params(("parallel",), [((tm, d), F32), ((tm, d), BF16)], temps=[((tm, d), F32)]),
        name="rmsnorm_cast",
    )(x, g)


def _cast_pair_kernel(a_ref, b_ref, ao_ref, bo_ref):
    ao_ref[...] = a_ref[...].astype(ao_ref.dtype)
    bo_ref[...] = b_ref[...].astype(bo_ref.dtype)


def _cast_pair(a, b, *, rows=256):
    k, n = a.shape
    assert b.shape == a.shape
    spec = pl.BlockSpec((rows, n), lambda i: (i, 0))
    return pl.pallas_call(
        _cast_pair_kernel,
        grid=(k // rows,),
        in_specs=[spec, spec],
        out_specs=[spec, spec],
        out_shape=[jax.ShapeDtypeStruct((k, n), BF16)] * 2,
        compiler_params=_params(("parallel",), [((rows, n), F32), ((rows, n), BF16)] * 2),
        name="cast_up_weights",
    )(a, b)


def _slab_spec(w, steps, nj):
    slab = w.shape[0] // steps
    assert slab * steps == w.shape[0] and slab % 16 == 0
    return (slab, w.shape[1]), pl.BlockSpec((slab, w.shape[1]), lambda i, j: (i * nj + j, 0))


def _matmul_kernel(a_ref, b_ref, *rest, b_is_nk):
    o_ref = rest[-2] if len(rest) == 3 else rest[0]
    contract_b = 1 if b_is_nk else 0
    o_ref[...] = lax.dot_general(a_ref[...], _load_bf16(b_ref), (((1,), (contract_b,)), ((), ())),
                                 preferred_element_type=F32).astype(o_ref.dtype)
    if len(rest) == 3:
        rest[2][...] = rest[0][...].astype(rest[2].dtype)


def _matmul(a, b, *, tm, tn, out_dtype, name, b_is_nk=False, n=None, side_cast=None):
    m, k = a.shape
    if n is None:
        n = b.shape[0] if b_is_nk else b.shape[1]
    if b_is_nk:
        b_block, b_spec = (tn, k), pl.BlockSpec((tn, k), lambda i, j: (j, 0))
    else:
        b_block, b_spec = (k, tn), pl.BlockSpec((k, tn), lambda i, j: (0, j))
    grid = (m // tm, n // tn)
    in_specs = [pl.BlockSpec((tm, k), lambda i, j: (i, 0)), b_spec]
    out_specs = pl.BlockSpec((tm, tn), lambda i, j: (i, j))
    out_shape = jax.ShapeDtypeStruct((m, n), out_dtype)
    blocks = [((tm, k), a.dtype), (b_block, b.dtype), ((tm, tn), out_dtype)]
    args = (a, b)
    if side_cast is not None:
        slab_block, slab_spec = _slab_spec(side_cast, grid[0] * grid[1], grid[1])
        in_specs.append(slab_spec)
        out_specs = [out_specs, slab_spec]
        out_shape = [out_shape, jax.ShapeDtypeStruct(side_cast.shape, BF16)]
        blocks += [(slab_block, F32), (slab_block, BF16)]
        args = (a, b, side_cast)
    return pl.pallas_call(
        functools.partial(_matmul_kernel, b_is_nk=b_is_nk),
        grid=grid, in_specs=in_specs, out_specs=out_specs, out_shape=out_shape,
        compiler_params=_params(("parallel", "parallel"), blocks, temps=[((tm, tn), F32), ((k, tn), BF16)]),
        name=name,
    )(*args)


def _pool_mixer_kernel(halo_ref, u_ref, w_ref, scale_ref, o_ref, ext_ref, lvl_a_ref, lvl_b_ref, *, tm, gw):
    i = pl.program_id(0)
    groups = len(POOL_WINDOWS)
    base = POOL_PAD + POOL_HALO
    n = POOL_HALO + tm
    width = ext_ref.shape[1]
    for ref in (ext_ref, lvl_a_ref, lvl_b_ref):
        ref[0:POOL_PAD, :] = jnp.zeros((POOL_PAD, width), F32)
    ext_ref[POOL_PAD:base, :] = jnp.where(i > 0, halo_ref[...], 0.0)
    ext_ref[base:base + tm, :] = u_ref[...]
    row = lax.broadcasted_iota(jnp.int32, (tm, 1), 0) + i * tm
    src = ext_ref
    for k, w in enumerate(POOL_WINDOWS):
        span = w // 2
        cols = slice(k * gw, (k + 1) * gw)
        u = u_ref[:, cols]
        win_sum = src[base:base + tm, cols] + src[base - span:base - span + tm, cols]
        inv_cnt = 1.0 / jnp.minimum(row + 1, w).astype(F32)
        pooled = win_sum * inv_cnt - u
        mixed = jnp.dot(pooled.astype(BF16), w_ref[k], preferred_element_type=F32)
        o_ref[:, cols] = (mixed * scale_ref[:, cols]).astype(o_ref.dtype)
        if k + 1 < groups:
            rest = slice((k + 1) * gw, width)
            dst = lvl_a_ref if k % 2 == 0 else lvl_b_ref
            dst[POOL_PAD:POOL_PAD + n, rest] = (src[POOL_PAD:POOL_PAD + n, rest]
                                                + src[POOL_PAD - span:POOL_PAD - span + n, rest])
            src = dst


def _pool_mixer(z, w_pool, pool_scale, *, pool_width, tm=512):
    s = z.shape[0]
    groups, gw, _ = w_pool.shape
    assert all(w == 2 ** (k + 1) for k, w in enumerate(POOL_WINDOWS)) and POOL_WINDOWS[-1] // 2 <= POOL_PAD
    halo_blocks = tm // POOL_HALO
    buf = (POOL_PAD + POOL_HALO + tm, pool_width)
    return pl.pallas_call(
        functools.partial(_pool_mixer_kernel, tm=tm, gw=gw),
        grid=(s // tm,),
        in_specs=[
            pl.BlockSpec((POOL_HALO, pool_width), lambda i: (jnp.maximum(i * halo_blocks - 1, 0), 0)),
            pl.BlockSpec((tm, pool_width), lambda i: (i, 0)),
            pl.BlockSpec((groups, gw, gw), lambda i: (0, 0, 0)),
            pl.BlockSpec((1, pool_width), lambda i: (0, 0)),
        ],
        out_specs=pl.BlockSpec((tm, pool_width), lambda i: (i, 0)),
        out_shape=jax.ShapeDtypeStruct((s, pool_width), BF16),
        scratch_shapes=[pltpu.VMEM(buf, F32)] * 3,
        compiler_params=_params(("parallel",),
                                [((tm, pool_width), F32), ((groups, gw, gw), BF16), ((tm, pool_width), BF16)],
                                scratch=[(buf, F32)] * 3,
                                temps=[((tm, pool_width), F32)]),
        name="pool_mixer",
    )(z, z, w_pool, pool_scale)


def _q_proj_kernel(ql_ref, g_ref, w_ref, pos_ref, freq_ref, o_ref, qn_ref, cos_ref, sin_ref, *, scale):
    @pl.when(pl.program_id(1) == 0)
    def _():
        qn_ref[...] = (_rms(ql_ref[...], g_ref[...]) * scale).astype(qn_ref.dtype)
        ang = freq_ref[...] * pos_ref[...]
        cos_ref[...] = jnp.cos(ang)
        sin_ref[...] = jnp.sin(ang)

    qt = lax.dot_general(w_ref[...], qn_ref[...], (((1,), (1,)), ((), ())), preferred_element_type=F32)
    half = QK_ROPE_DIM // 2
    r0, r1, r2 = QK_NOPE_DIM, QK_NOPE_DIM + half, QK_NOPE_DIM + QK_ROPE_DIM
    c, sn = cos_ref[...], sin_ref[...]
    rows = o_ref.shape[2]
    for hh in range(o_ref.shape[0]):
        q = qt[hh * rows:(hh + 1) * rows]
        x1, x2 = q[r0:r1], q[r1:r2]
        o_ref[hh, 0, 0:r0, :] = q[0:r0].astype(o_ref.dtype)
        o_ref[hh, 0, r0:r1, :] = (x1 * c - x2 * sn).astype(o_ref.dtype)
        o_ref[hh, 0, r1:r2, :] = (x1 * sn + x2 * c).astype(o_ref.dtype)
        o_ref[hh, 0, r2:, :] = q[r2:].astype(o_ref.dtype)


def _q_proj(z, q_norm, wq_t, pos_row, freq_col, *, col_block, rank, scale, tm, heads_per_step=8):
    s = z.shape[0]
    rows = V7X_MXU_DIM
    n_heads = wq_t.shape[0] // rows
    hps = heads_per_step
    half = QK_ROPE_DIM // 2
    return pl.pallas_call(
        functools.partial(_q_proj_kernel, scale=scale),
        grid=(s // tm, n_heads // hps),
        in_specs=[
            pl.BlockSpec((tm, rank), lambda i, h: (i, col_block)),
            pl.BlockSpec((1, rank), lambda i, h: (0, 0)),
            pl.BlockSpec((hps * rows, rank), lambda i, h: (h, 0)),
            pl.BlockSpec((1, tm), lambda i, h: (0, i)),
            pl.BlockSpec((half, 1), lambda i, h: (0, 0)),
        ],
        out_specs=pl.BlockSpec((hps, 1, rows, tm), lambda i, h: (h, i, 0, 0)),
        out_shape=jax.ShapeDtypeStruct((n_heads, s // tm, rows, tm), BF16),
        scratch_shapes=[pltpu.VMEM((tm, rank), BF16), pltpu.VMEM((half, tm), F32), pltpu.VMEM((half, tm), F32)],
        compiler_params=_params(("parallel", "arbitrary"),
                                [((tm, rank), F32), ((hps * rows, rank), BF16), ((hps * rows, tm), BF16)],
                                scratch=[((tm, rank), BF16), ((2 * half, tm), F32)],
                                temps=[((tm, rank), F32), ((hps * rows, tm), F32)]),
        name="q_proj",
    )(z, q_norm, wq_t, pos_row, freq_col)


def _kv_proj_kernel(kvl_ref, h_ref, wkr_ref, g_ref, wk_ref, wvt_ref, pos_ref, freq_ref, k_ref, vt_ref, kro_ref):
    kvn = _rms(kvl_ref[...], g_ref[...]).astype(BF16)
    k_ref[...] = jnp.dot(kvn, wk_ref[...], preferred_element_type=F32).astype(k_ref.dtype)
    vt = lax.dot_general(wvt_ref[...], kvn, (((1,), (1,)), ((), ())), preferred_element_type=F32)
    vt_ref[:, 0] = vt.reshape(vt_ref.shape[0], V_HEAD_DIM, vt.shape[1]).astype(vt_ref.dtype)
    half = QK_ROPE_DIM // 2
    wkr = _load_bf16(wkr_ref)
    wkr = jnp.concatenate([wkr, jnp.zeros((V7X_LANES - wkr.shape[0], wkr.shape[1]), BF16)], axis=0)
    x = lax.dot_general(h_ref[...], wkr, (((1,), (1,)), ((), ())), preferred_element_type=F32)
    ang = pos_ref[...] * freq_ref[...]
    lane = lax.broadcasted_iota(jnp.int32, x.shape, 1)
    x2_at_lo = pltpu.roll(x, V7X_LANES - half, axis=1)
    x1_at_hi = pltpu.roll(x, half, axis=1)
    partner = jnp.where(lane < half, -x2_at_lo, jnp.where(lane < QK_ROPE_DIM, x1_at_hi, 0.0))
    kro_ref[...] = (x * jnp.cos(ang) + partner * jnp.sin(ang)).astype(kro_ref.dtype)


def _kv_proj(z, h, w_in_t, kv_norm, wk, wv_t, pos_col, freq_row, *, lat_block, rope_row_block, rank, tm):
    s, d = h.shape
    nk = wk.shape[1]
    nv = wv_t.shape[0]
    n_heads = nv // V_HEAD_DIM
    return pl.pallas_call(
        _kv_proj_kernel,
        grid=(s // tm,),
        in_specs=[
            pl.BlockSpec((tm, rank), lambda i: (i, lat_block)),
            pl.BlockSpec((tm, d), lambda i: (i, 0)),
            pl.BlockSpec((QK_ROPE_DIM, d), lambda i: (rope_row_block, 0)),
            pl.BlockSpec((1, rank), lambda i: (0, 0)),
            pl.BlockSpec((rank, nk), lambda i: (0, 0)),
            pl.BlockSpec((nv, rank), lambda i: (0, 0)),
            pl.BlockSpec((tm, 1), lambda i: (i, 0)),
            pl.BlockSpec((1, V7X_LANES), lambda i: (0, 0)),
        ],
        out_specs=[
            pl.BlockSpec((tm, nk), lambda i: (i, 0)),
            pl.BlockSpec((n_heads, 1, V_HEAD_DIM, tm), lambda i: (0, i, 0, 0)),
            pl.BlockSpec((tm, V7X_LANES), lambda i: (i, 0)),
        ],
        out_shape=[
            jax.ShapeDtypeStruct((s, nk), BF16),
            jax.ShapeDtypeStruct((n_heads, s // tm, V_HEAD_DIM, tm), BF16),
            jax.ShapeDtypeStruct((s, V7X_LANES), BF16),
        ],
        compiler_params=_params(("parallel",),
                                [((tm, rank), F32), ((tm, d), BF16), ((QK_ROPE_DIM, d), w_in_t.dtype), ((rank, nk), BF16),
                                 ((nv, rank), BF16), ((tm, nk), BF16), ((nv, tm), BF16), ((tm, V7X_LANES), BF16),
                                 ((tm, V7X_LANES), F32)],
                                temps=[((tm, nk), F32), ((nv, tm), F32), ((V7X_LANES, d), BF16)]),
        name="kv_proj",
    )(z, h, w_in_t, kv_norm, wk, wv_t, pos_col, freq_row)


def _flash_kernel(kn_ref, kr_ref, q_ref, qnext_ref, vt_ref, o_ref, s_ref, smax_ref, bias_ref, m_ref, l_ref, acc_ref,
                  *, t, unroll):
    heads = q_ref.shape[0]
    qi = pl.program_id(1)
    first = 2

    def scores(kb, slot, q=q_ref):
        ks = pl.ds(pl.multiple_of(kb * t, t), t)
        for hh in range(heads):
            kn = kn_ref[ks, hh * QK_NOPE_DIM:(hh + 1) * QK_NOPE_DIM]
            kcat = jnp.concatenate([kn, kr_ref[ks, :]], axis=1)
            s = jnp.dot(kcat, q[hh, 0], preferred_element_type=F32)
            s_ref[hh, slot] = s
            smax_ref[hh, slot] = jnp.max(s, axis=0, keepdims=True)

    def softmax_pv(kb, slot, masked):
        for hh in range(heads):
            s = s_ref[hh, slot]
            if masked:
                s = s + bias_ref[...]
                block_max = jnp.max(s, axis=0, keepdims=True)
            else:
                block_max = smax_ref[hh, slot]
            m_prev = m_ref[hh]
            m_new = jnp.maximum(m_prev, block_max)
            alpha = jnp.exp2(m_prev - m_new)
            p = jnp.exp2(s - m_new)
            l_ref[hh] = alpha * l_ref[hh] + jnp.sum(p, axis=0, keepdims=True)
            pv = jnp.dot(vt_ref[hh, kb], p.astype(BF16), preferred_element_type=F32)
            acc_ref[hh] = alpha * acc_ref[hh] + pv
            m_ref[hh] = m_new

    m_ref[...] = jnp.full(m_ref.shape, -jnp.inf, F32)
    l_ref[...] = jnp.zeros(l_ref.shape, F32)
    acc_ref[...] = jnp.zeros(acc_ref.shape, F32)

    @pl.when(qi == 0)
    def _():
        kc = lax.broadcasted_iota(jnp.int32, (t, t), 0) // CHUNK
        qc = lax.broadcasted_iota(jnp.int32, (t, t), 1) // CHUNK
        bias_ref[...] = jnp.where(kc <= qc, 0.0, -jnp.inf).astype(F32)
        scores(0, first)
        softmax_pv(0, first, masked=True)
        scores(0, first, qnext_ref)

    @pl.when(qi > 0)
    def _():
        scores(1, 1)
        softmax_pv(0, first, masked=False)
        n_groups = (qi - 1) // unroll

        def group(g, c):
            b0 = 1 + g * unroll
            for u in range(unroll):
                scores(b0 + u + 1, u % 2)
                softmax_pv(b0 + u, (u + 1) % 2, masked=False)
            return c

        lax.fori_loop(0, n_groups, group, 0)
        base = 1 + n_groups * unroll
        rest = qi - base
        for r in range(unroll):
            @pl.when(rest == r)
            def _(r=r):
                for u in range(r):
                    scores(base + u + 1, u % 2)
                    softmax_pv(base + u, (u + 1) % 2, masked=False)
                scores(0, first, qnext_ref)
                softmax_pv(qi, (1 + r) % 2, masked=True)

    for hh in range(heads):
        out = (acc_ref[hh] * (1.0 / l_ref[hh])).T
        o_ref[:, hh * V_HEAD_DIM:(hh + 1) * V_HEAD_DIM] = out.astype(o_ref.dtype)


def _flash_attention(k_nope, k_rope, q_t, v_t, *, n_heads, t, unroll=2, heads_per_step=4):
    s = k_nope.shape[0]
    nq = s // t
    hps = heads_per_step
    assert t % CHUNK == 0 and unroll % 2 == 0 and nq >= 2 and n_heads % hps == 0
    assert v_t.shape == (n_heads, nq, V_HEAD_DIM, t) and q_t.shape == (n_heads, nq, V7X_MXU_DIM, t)
    scratch = [((hps, 3, t, t), F32), ((hps, 3, 1, t), F32), ((t, t), F32), ((hps, 1, t), F32), ((hps, 1, t), F32),
               ((hps, V_HEAD_DIM, t), F32)]
    return pl.pallas_call(
        functools.partial(_flash_kernel, t=t, unroll=unroll),
        grid=(n_heads // hps, nq),
        in_specs=[
            pl.BlockSpec((s, hps * QK_NOPE_DIM), lambda h, i: (0, h)),
            pl.BlockSpec((s, V7X_LANES), lambda h, i: (0, 0)),
            pl.BlockSpec((hps, 1, V7X_MXU_DIM, t), lambda h, i: (h, i, 0, 0)),
            pl.BlockSpec((hps, 1, V7X_MXU_DIM, t), lambda h, i: (h, jnp.minimum(i + 1, nq - 1), 0, 0)),
            pl.BlockSpec((hps, nq, V_HEAD_DIM, t), lambda h, i: (h, 0, 0, 0)),
        ],
        out_specs=pl.BlockSpec((t, hps * V_HEAD_DIM), lambda h, i: (i, h)),
        out_shape=jax.ShapeDtypeStruct((s, n_heads * V_HEAD_DIM), BF16),
        scratch_shapes=[pltpu.VMEM(shape, dtype) for shape, dtype in scratch],
        compiler_params=_params(("arbitrary", "arbitrary"),
                                [((s, hps * QK_NOPE_DIM), BF16), ((s, V7X_LANES), BF16),
                                 ((2 * hps * V7X_MXU_DIM, t), BF16), ((hps * V_HEAD_DIM, s), BF16),
                                 ((t, hps * V_HEAD_DIM), BF16)],
                                scratch=scratch,
                                temps=[((t, t), F32), ((t, t), BF16), ((t, V7X_MXU_DIM), BF16)]),
        name="flash_attention",
    )(k_nope, k_rope, q_t, q_t, v_t)


def _gated_merge_kernel(pm_ref, at_ref, h_ref, wup_ref, wum_ref, wga_ref, wgb_ref, wo_ref, o_ref, wo_o_ref):
    h = h_ref[...]
    ya = jnp.dot(pm_ref[...], wup_ref[...], preferred_element_type=F32)
    yb = jnp.dot(at_ref[...], wum_ref[...], preferred_element_type=F32)
    ga = _sigmoid(jnp.dot(h, wga_ref[...], preferred_element_type=F32))
    gb = _sigmoid(jnp.dot(h, wgb_ref[...], preferred_element_type=F32))
    o_ref[...] = (ga * ya + gb * yb).astype(o_ref.dtype)
    wo_o_ref[...] = wo_ref[...].astype(wo_o_ref.dtype)


def _gated_merge(pm, attn, h, w_up_pool, w_up_mla, w_gate2d, w_out, *, tm=1024, tn=256):
    s, d = h.shape
    kp, km = pm.shape[1], attn.shape[1]
    nb = d // tn
    slab_block, slab_spec = _slab_spec(w_out, (s // tm) * nb, nb)
    return pl.pallas_call(
        _gated_merge_kernel,
        grid=(s // tm, nb),
        in_specs=[
            pl.BlockSpec((tm, kp), lambda i, j: (i, 0)),
            pl.BlockSpec((tm, km), lambda i, j: (i, 0)),
            pl.BlockSpec((tm, d), lambda i, j: (i, 0)),
            pl.BlockSpec((kp, tn), lambda i, j: (0, j)),
            pl.BlockSpec((km, tn), lambda i, j: (0, j)),
            pl.BlockSpec((d, tn), lambda i, j: (0, j)),
            pl.BlockSpec((d, tn), lambda i, j: (0, j + nb)),
            slab_spec,
        ],
        out_specs=[pl.BlockSpec((tm, tn), lambda i, j: (i, j)), slab_spec],
        out_shape=[jax.ShapeDtypeStruct((s, d), BF16), jax.ShapeDtypeStruct(w_out.shape, BF16)],
        compiler_params=_params(("parallel", "parallel"),
                                [((tm, kp), BF16), ((tm, km), BF16), ((tm, d), BF16), ((kp, tn), BF16),
                                 ((km, tn), BF16), ((d, tn), BF16), ((d, tn), BF16), ((tm, tn), BF16),
                                 (slab_block, F32), (slab_block, BF16)],
                                temps=[((tm, tn), F32)] * 4),
        name="gated_merge",
    )(pm, attn, h, w_up_pool, w_up_mla, w_gate2d, w_gate2d, w_out)


def _residual_norm_kernel(t_ref, x_ref, gpost_ref, gnext_ref, xo_ref, ho_ref):
    xo = x_ref[...] + _rms(t_ref[...].astype(F32), gpost_ref[...])
    xo_ref[...] = xo
    ho_ref[...] = _rms(xo, gnext_ref[...]).astype(ho_ref.dtype)


def _residual_final_kernel(t_ref, x_ref, gpost_ref, xo_ref):
    xo_ref[...] = x_ref[...] + _rms(t_ref[...].astype(F32), gpost_ref[...])


def _residual_norm(t, x, g_post, g_next=None, *, tm=256):
    s, d = x.shape
    row = pl.BlockSpec((tm, d), lambda i: (i, 0))
    vec = pl.BlockSpec((1, d), lambda i: (0, 0))
    blocks = [((tm, d), t.dtype), ((tm, d), F32), ((tm, d), F32)]
    if g_next is None:
        return pl.pallas_call(
            _residual_final_kernel, grid=(s // tm,), in_specs=[row, row, vec], out_specs=row,
            out_shape=jax.ShapeDtypeStruct((s, d), F32),
            compiler_params=_params(("parallel",), blocks, temps=[((tm, d), F32)] * 2),
            name="residual_final",
        )(t, x, g_post)
    return pl.pallas_call(
        _residual_norm_kernel, grid=(s // tm,), in_specs=[row, row, vec, vec], out_specs=[row, row],
        out_shape=[jax.ShapeDtypeStruct((s, d), F32), jax.ShapeDtypeStruct((s, d), BF16)],
        compiler_params=_params(("parallel",), blocks + [((tm, d), BF16)], temps=[((tm, d), F32)] * 2),
        name="residual_norm",
    )(t, x, g_post, g_next)


def _swiglu_up_kernel(h_ref, wg_ref, wu_ref, wd_ref, o_ref, wd_o_ref):
    h = h_ref[...]
    gate = jnp.dot(h, _load_bf16(wg_ref), preferred_element_type=F32)
    up = jnp.dot(h, _load_bf16(wu_ref), preferred_element_type=F32)
    o_ref[...] = (gate * _sigmoid(gate) * up).astype(o_ref.dtype)
    wd_o_ref[...] = wd_ref[...].astype(wd_o_ref.dtype)


def _swiglu_up(h, w_gate, w_up, w_down, *, tm=1024, tn=256):
    s, d = h.shape
    f = w_gate.shape[1]
    nj = f // tn
    steps = (s // tm) * nj
    slab = w_down.shape[0] // steps
    assert slab * steps == w_down.shape[0] and slab % 16 == 0
    dn = w_down.shape[1]
    return pl.pallas_call(
        _swiglu_up_kernel,
        grid=(s // tm, nj),
        in_specs=[
            pl.BlockSpec((tm, d), lambda i, j: (i, 0)),
            pl.BlockSpec((d, tn), lambda i, j: (0, j)),
            pl.BlockSpec((d, tn), lambda i, j: (0, j)),
            pl.BlockSpec((slab, dn), lambda i, j: (i * nj + j, 0)),
        ],
        out_specs=[pl.BlockSpec((tm, tn), lambda i, j: (i, j)),
                   pl.BlockSpec((slab, dn), lambda i, j: (i * nj + j, 0))],
        out_shape=[jax.ShapeDtypeStruct((s, f), BF16), jax.ShapeDtypeStruct(w_down.shape, BF16)],
        compiler_params=_params(("parallel", "parallel"),
                                [((tm, d), BF16), ((d, tn), w_gate.dtype), ((d, tn), w_up.dtype), ((tm, tn), BF16),
                                 ((slab, dn), w_down.dtype), ((slab, dn), BF16)],
                                temps=[((tm, tn), F32)] * 3 + [((d, tn), BF16)] * 2),
        name="swiglu_up",
    )(h, w_gate, w_up, w_down)


def _ple_gate_kernel(h_ref, p_ref, wg_ref, wp_ref, o_ref):
    gate = _sigmoid(jnp.dot(h_ref[...], _load_bf16(wg_ref), preferred_element_type=F32))
    pe = jnp.dot(_load_bf16(p_ref), _load_bf16(wp_ref), preferred_element_type=F32)
    o_ref[...] = (pe * gate).astype(o_ref.dtype)


def _ple_gate(h, p, w_gate, w_proj, *, tm=1024, tn=512):
    s, d = h.shape
    r = p.shape[1]
    return pl.pallas_call(
        _ple_gate_kernel,
        grid=(s // tm, d // tn),
        in_specs=[
            pl.BlockSpec((tm, d), lambda i, j: (i, 0)),
            pl.BlockSpec((tm, r), lambda i, j: (i, 0)),
            pl.BlockSpec((d, tn), lambda i, j: (0, j)),
            pl.BlockSpec((r, tn), lambda i, j: (0, j)),
        ],
        out_specs=pl.BlockSpec((tm, tn), lambda i, j: (i, j)),
        out_shape=jax.ShapeDtypeStruct((s, d), BF16),
        compiler_params=_params(("parallel", "parallel"),
                                [((tm, d), BF16), ((tm, r), p.dtype), ((d, tn), w_gate.dtype), ((r, tn), w_proj.dtype),
                                 ((tm, tn), BF16)],
                                temps=[((tm, tn), F32)] * 2 + [((d, tn), BF16)]),
        name="ple_gate",
    )(h, p, w_gate, w_proj)


def kernel(x, p, positions, norm_mix_pre, norm_mix_post, w_in, q_norm, kv_norm, w_q_b, w_kv_b, w_pool, pool_scale,
           w_up_pool, w_up_mla, w_branch_gate, w_out, norm_ffn_pre, norm_ffn_post, w_ffn_gate, w_ffn_up,
           w_ffn_down, norm_ple_pre, w_ple_gate, w_ple_proj, norm_ple_post):
    batch, seq, d_model = x.shape
    depth = w_in.shape[0]
    in_width = w_in.shape[2]
    q_rank, n_heads, qk_dim = w_q_b.shape[1:]
    kv_rank = w_kv_b.shape[1]
    pool_width = w_up_pool.shape[1]
    assert batch == 1 and qk_dim == QK_NOPE_DIM + QK_ROPE_DIM
    assert w_kv_b.shape[3] == QK_NOPE_DIM + V_HEAD_DIM
    o_q, o_kv, o_kr = pool_width, pool_width + q_rank, pool_width + q_rank + kv_rank
    assert o_kr + QK_ROPE_DIM == in_width
    assert o_q % q_rank == 0 and o_kv % kv_rank == 0 and o_kr % QK_ROPE_DIM == 0
    z_tn = 512
    assert o_kr % z_tn == 0
    attn_t = 512
    half = QK_ROPE_DIM // 2
    scale = float(qk_dim) ** -0.5 * LOG2_E

    inv_freq = ROPE_THETA ** (-jnp.arange(0, QK_ROPE_DIM, 2, dtype=F32) / QK_ROPE_DIM)
    freq_col = inv_freq.reshape(half, 1)
    freq_row = jnp.zeros((1, V7X_LANES), F32).at[0, :half].set(inv_freq).at[0, half:QK_ROPE_DIM].set(inv_freq)
    pos_f = positions.astype(F32)
    pos_row, pos_col = pos_f.reshape(1, seq), pos_f.reshape(seq, 1)

    xs = x.reshape(seq, d_model)
    for i in range(depth):
        w_in_t = jnp.transpose(w_in[i])
        wq = w_q_b[i].astype(BF16)
        wq_t = jnp.pad(jnp.transpose(wq, (1, 2, 0)), ((0, 0), (0, V7X_MXU_DIM - qk_dim), (0, 0)))
        wq_t = wq_t.reshape(n_heads * V7X_MXU_DIM, q_rank)
        wkv = w_kv_b[i].astype(BF16)
        wk = wkv[:, :, :QK_NOPE_DIM].reshape(kv_rank, n_heads * QK_NOPE_DIM)
        wv_t = jnp.transpose(wkv[:, :, QK_NOPE_DIM:], (1, 2, 0)).reshape(n_heads * V_HEAD_DIM, kv_rank)
        w_up_p, w_up_m = _cast_pair(w_up_pool[i], w_up_mla[i])

        def row(v):
            return v[i].reshape(1, -1)

        h = _rmsnorm_cast(xs, row(norm_mix_pre))
        z = _matmul(h, w_in_t, tm=1024, tn=z_tn, out_dtype=F32, name="in_proj", b_is_nk=True, n=o_kr)
        pm = _pool_mixer(z, w_pool[i].astype(BF16), row(pool_scale), pool_width=pool_width)
        q_t = _q_proj(z, row(q_norm), wq_t, pos_row, freq_col, col_block=o_q // q_rank, rank=q_rank, scale=scale,
                      tm=attn_t)
        k_nope, v_t, k_rope = _kv_proj(z, h, w_in_t, row(kv_norm), wk, wv_t, pos_col, freq_row,
                                       lat_block=o_kv // kv_rank, rope_row_block=o_kr // QK_ROPE_DIM, rank=kv_rank,
                                       tm=attn_t)
        attn = _flash_attention(k_nope, k_rope, q_t, v_t, n_heads=n_heads, t=attn_t)
        w_gate2d = w_branch_gate[i].astype(BF16).reshape(d_model, 2 * d_model)
        merged, w_out_b = _gated_merge(pm, attn, h, w_up_p, w_up_m, w_gate2d, w_out[i])
        mix = _matmul(merged, w_out_b, tm=1024, tn=1024, out_dtype=BF16, name="out_proj")
        xs, h2 = _residual_norm(mix, xs, row(norm_mix_post), row(norm_ffn_pre))
        act, w_down = _swiglu_up(h2, w_ffn_gate[i], w_ffn_up[i], w_ffn_down[i])
        ffn, w_pg = _matmul(act, w_down, tm=512, tn=512, out_dtype=BF16, name="ffn_down", side_cast=w_ple_gate[i])
        xs, h3 = _residual_norm(ffn, xs, row(norm_ffn_post), row(norm_ple_pre))
        t = _ple_gate(h3, p[i].reshape(seq, -1), w_pg, w_ple_proj[i], tn=1024)
        xs = _residual_norm(t, xs, row(norm_ple_post))
    return xs.reshape(batch, seq, d_model)
```

```python
import functools

import jax
import jax.numpy as jnp
from jax import lax
from jax.experimental import pallas as pl
from jax.experimental.pallas import tpu as pltpu

CHUNK = 64
EPS = 1e-6
POOL_WINDOWS = (2, 4, 8, 16)
QK_NOPE_DIM = 128
QK_ROPE_DIM = 64
V_HEAD_DIM = 128
ROPE_THETA = 10000.0
LOG2_E = 1.4426950408889634

V7X_LANES = 128
V7X_MXU_DIM = 256
V7X_VMEM_BYTES = 64 * 1024 * 1024
V7X_VMEM_USABLE_BYTES = 60000 * 1024
V7X_VMEM_DEFAULT_SCOPED_BYTES = 32 * 1024 * 1024

POOL_HALO = 16
POOL_PAD = 8
F32 = jnp.float32
BF16 = jnp.bfloat16


def _nbytes(shape, dtype):
    n = 1
    for s in shape:
        n *= s
    return n * jnp.dtype(dtype).itemsize


def _params(semantics, blocks, scratch=(), temps=()):
    need = 2 * sum(_nbytes(s, d) for s, d in blocks)
    need += sum(_nbytes(s, d) for s, d in scratch) + sum(_nbytes(s, d) for s, d in temps)
    need = max(int(need * 1.25) + (2 << 20), V7X_VMEM_DEFAULT_SCOPED_BYTES)
    return pltpu.CompilerParams(dimension_semantics=semantics,
                                vmem_limit_bytes=min(need, V7X_VMEM_USABLE_BYTES))


def _load_bf16(ref):
    v = ref[...]
    return v if v.dtype == BF16 else v.astype(BF16)


def _sigmoid(x):
    return 0.5 * jnp.tanh(0.5 * x) + 0.5


def _rms(xf, g):
    return xf * lax.rsqrt(jnp.mean(xf * xf, axis=-1, keepdims=True) + EPS) * g


def _rmsnorm_cast_kernel(x_ref, g_ref, o_ref):
    o_ref[...] = _rms(x_ref[...], g_ref[...]).astype(o_ref.dtype)


def _rmsnorm_cast(x, g, *, tm=512):
    s, d = x.shape
    return pl.pallas_call(
        _rmsnorm_cast_kernel,
        grid=(s // tm,),
        in_specs=[pl.BlockSpec((tm, d), lambda i: (i, 0)), pl.BlockSpec((1, d), lambda i: (0, 0))],
        out_specs=pl.BlockSpec((tm, d), lambda i: (i, 0)),
        out_shape=jax.ShapeDtypeStruct((s, d), BF16),
        compiler_params=_params(("parallel",), [((tm, d), F32), ((tm, d), BF16)], temps=[((tm, d), F32)]),
        name="rmsnorm_cast",
    )(x, g)


def _cast_pair_kernel(a_ref, b_ref, ao_ref, bo_ref):
    ao_ref[...] = a_ref[...].astype(ao_ref.dtype)
    bo_ref[...] = b_ref[...].astype(bo_ref.dtype)


def _cast_pair(a, b, *, rows=256):
    k, n = a.shape
    assert b.shape == a.shape
    spec = pl.BlockSpec((rows, n), lambda i: (i, 0))
    return pl.pallas_call(
        _cast_pair_kernel,
        grid=(k // rows,),
        in_specs=[spec, spec],
        out_specs=[spec, spec],
        out_shape=[jax.ShapeDtypeStruct((k, n), BF16)] * 2,
        compiler_params=_params(("parallel",), [((rows, n), F32), ((rows, n), BF16)] * 2),
        name="cast_up_weights",
    )(a, b)


def _slab_spec(w, steps, nj):
    slab = w.shape[0] // steps
    assert slab * steps == w.shape[0] and slab % 16 == 0
    return (slab, w.shape[1]), pl.BlockSpec((slab, w.shape[1]), lambda i, j: (i * nj + j, 0))


def _matmul_kernel(a_ref, b_ref, *rest, b_is_nk):
    o_ref = rest[-2] if len(rest) == 3 else rest[0]
    contract_b = 1 if b_is_nk else 0
    o_ref[...] = lax.dot_general(a_ref[...], _load_bf16(b_ref), (((1,), (contract_b,)), ((), ())),
                                 preferred_element_type=F32).astype(o_ref.dtype)
    if len(rest) == 3:
        rest[2][...] = rest[0][...].astype(rest[2].dtype)


def _matmul(a, b, *, tm, tn, out_dtype, name, b_is_nk=False, n=None, side_cast=None):
    m, k = a.shape
    if n is None:
        n = b.shape[0] if b_is_nk else b.shape[1]
    if b_is_nk:
        b_block, b_spec = (tn, k), pl.BlockSpec((tn, k), lambda i, j: (j, 0))
    else:
        b_block, b_spec = (k, tn), pl.BlockSpec((k, tn), lambda i, j: (0, j))
    grid = (m // tm, n // tn)
    in_specs = [pl.BlockSpec((tm, k), lambda i, j: (i, 0)), b_spec]
    out_specs = pl.BlockSpec((tm, tn), lambda i, j: (i, j))
    out_shape = jax.ShapeDtypeStruct((m, n), out_dtype)
    blocks = [((tm, k), a.dtype), (b_block, b.dtype), ((tm, tn), out_dtype)]
    args = (a, b)
    if side_cast is not None:
        slab_block, slab_spec = _slab_spec(side_cast, grid[0] * grid[1], grid[1])
        in_specs.append(slab_spec)
        out_specs = [out_specs, slab_spec]
        out_shape = [out_shape, jax.ShapeDtypeStruct(side_cast.shape, BF16)]
        blocks += [(slab_block, F32), (slab_block, BF16)]
        args = (a, b, side_cast)
    return pl.pallas_call(
        functools.partial(_matmul_kernel, b_is_nk=b_is_nk),
        grid=grid, in_specs=in_specs, out_specs=out_specs, out_shape=out_shape,
        compiler_params=_params(("parallel", "parallel"), blocks, temps=[((tm, tn), F32), ((k, tn), BF16)]),
        name=name,
    )(*args)


def _pool_mixer_kernel(halo_ref, u_ref, w_ref, scale_ref, o_ref, ext_ref, lvl_a_ref, lvl_b_ref, *, tm, gw):
    i = pl.program_id(0)
    groups = len(POOL_WINDOWS)
    base = POOL_PAD + POOL_HALO
    n = POOL_HALO + tm
    width = ext_ref.shape[1]
    for ref in (ext_ref, lvl_a_ref, lvl_b_ref):
        ref[0:POOL_PAD, :] = jnp.zeros((POOL_PAD, width), F32)
    ext_ref[POOL_PAD:base, :] = jnp.where(i > 0, halo_ref[...], 0.0)
    ext_ref[base:base + tm, :] = u_ref[...]
    row = lax.broadcasted_iota(jnp.int32, (tm, 1), 0) + i * tm
    src = ext_ref
    for k, w in enumerate(POOL_WINDOWS):
        span = w // 2
        cols = slice(k * gw, (k + 1) * gw)
        u = u_ref[:, cols]
        win_sum = src[base:base + tm, cols] + src[base - span:base - span + tm, cols]
        inv_cnt = 1.0 / jnp.minimum(row + 1, w).astype(F32)
        pooled = win_sum * inv_cnt - u
        mixed = jnp.dot(pooled.astype(BF16), w_ref[k], preferred_element_type=F32)
        o_ref[:, cols] = (mixed * scale_ref[:, cols]).astype(o_ref.dtype)
        if k + 1 < groups:
            rest = slice((k + 1) * gw, width)
            dst = lvl_a_ref if k % 2 == 0 else lvl_b_ref
            dst[POOL_PAD:POOL_PAD + n, rest] = (src[POOL_PAD:POOL_PAD + n, rest]
                                                + src[POOL_PAD - span:POOL_PAD - span + n, rest])
            src = dst


def _pool_mixer(z, w_pool, pool_scale, *, pool_width, tm=512):
    s = z.shape[0]
    groups, gw, _ = w_pool.shape
    assert all(w == 2 ** (k + 1) for k, w in enumerate(POOL_WINDOWS)) and POOL_WINDOWS[-1] // 2 <= POOL_PAD
    halo_blocks = tm // POOL_HALO
    buf = (POOL_PAD + POOL_HALO + tm, pool_width)
    return pl.pallas_call(
        functools.partial(_pool_mixer_kernel, tm=tm, gw=gw),
        grid=(s // tm,),
        in_specs=[
            pl.BlockSpec((POOL_HALO, pool_width), lambda i: (jnp.maximum(i * halo_blocks - 1, 0), 0)),
            pl.BlockSpec((tm, pool_width), lambda i: (i, 0)),
            pl.BlockSpec((groups, gw, gw), lambda i: (0, 0, 0)),
            pl.BlockSpec((1, pool_width), lambda i: (0, 0)),
        ],
        out_specs=pl.BlockSpec((tm, pool_width), lambda i: (i, 0)),
        out_shape=jax.ShapeDtypeStruct((s, pool_width), BF16),
        scratch_shapes=[pltpu.VMEM(buf, F32)] * 3,
        compiler_params=_params(("parallel",),
                                [((tm, pool_width), F32), ((groups, gw, gw), BF16), ((tm, pool_width), BF16)],
                                scratch=[(buf, F32)] * 3,
                                temps=[((tm, pool_width), F32)]),
        name="pool_mixer",
    )(z, z, w_pool, pool_scale)


def _q_proj_kernel(ql_ref, g_ref, w_ref, pos_ref, freq_ref, o_ref, qn_ref, cos_ref, sin_ref, *, scale):
    @pl.when(pl.program_id(1) == 0)
    def _():
        qn_ref[...] = (_rms(ql_ref[...], g_ref[...]) * scale).astype(qn_ref.dtype)
        ang = freq_ref[...] * pos_ref[...]
        cos_ref[...] = jnp.cos(ang)
        sin_ref[...] = jnp.sin(ang)

    qt = lax.dot_general(w_ref[...], qn_ref[...], (((1,), (1,)), ((), ())), preferred_element_type=F32)
    half = QK_ROPE_DIM // 2
    r0, r1, r2 = QK_NOPE_DIM, QK_NOPE_DIM + half, QK_NOPE_DIM + QK_ROPE_DIM
    c, sn = cos_ref[...], sin_ref[...]
    rows = o_ref.shape[2]
    for hh in range(o_ref.shape[0]):
        q = qt[hh * rows:(hh + 1) * rows]
        x1, x2 = q[r0:r1], q[r1:r2]
        o_ref[hh, 0, 0:r0, :] = q[0:r0].astype(o_ref.dtype)
        o_ref[hh, 0, r0:r1, :] = (x1 * c - x2 * sn).astype(o_ref.dtype)
        o_ref[hh, 0, r1:r2, :] = (x1 * sn + x2 * c).astype(o_ref.dtype)
        o_ref[hh, 0, r2:, :] = q[r2:].astype(o_ref.dtype)


def _q_proj(z, q_norm, wq_t, pos_row, freq_col, *, col_block, rank, scale, tm, heads_per_step=8):
    s = z.shape[0]
    rows = V7X_MXU_DIM
    n_heads = wq_t.shape[0] // rows
    hps = heads_per_step
    half = QK_ROPE_DIM // 2
    return pl.pallas_call(
        functools.partial(_q_proj_kernel, scale=scale),
        grid=(s // tm, n_heads // hps),
        in_specs=[
            pl.BlockSpec((tm, rank), lambda i, h: (i, col_block)),
            pl.BlockSpec((1, rank), lambda i, h: (0, 0)),
            pl.BlockSpec((hps * rows, rank), lambda i, h: (h, 0)),
            pl.BlockSpec((1, tm), lambda i, h: (0, i)),
            pl.BlockSpec((half, 1), lambda i, h: (0, 0)),
        ],
        out_specs=pl.BlockSpec((hps, 1, rows, tm), lambda i, h: (h, i, 0, 0)),
        out_shape=jax.ShapeDtypeStruct((n_heads, s // tm, rows, tm), BF16),
        scratch_shapes=[pltpu.VMEM((tm, rank), BF16), pltpu.VMEM((half, tm), F32), pltpu.VMEM((half, tm), F32)],
        compiler_params=_params(("parallel", "arbitrary"),
                                [((tm, rank), F32), ((hps * rows, rank), BF16), ((hps * rows, tm), BF16)],
                                scratch=[((tm, rank), BF16), ((2 * half, tm), F32)],
                                temps=[((tm, rank), F32), ((hps * rows, tm), F32)]),
        name="q_proj",
    )(z, q_norm, wq_t, pos_row, freq_col)


def _kv_proj_kernel(kvl_ref, h_ref, wkr_ref, g_ref, wk_ref, wvt_ref, pos_ref, freq_ref, k_ref, vt_ref, kro_ref):
    kvn = _rms(kvl_ref[...], g_ref[...]).astype(BF16)
    k_ref[...] = jnp.dot(kvn, wk_ref[...], preferred_element_type=F32).astype(k_ref.dtype)
    vt = lax.dot_general(wvt_ref[...], kvn, (((1,), (1,)), ((), ())), preferred_element_type=F32)
    vt_ref[:, 0] = vt.reshape(vt_ref.shape[0], V_HEAD_DIM, vt.shape[1]).astype(vt_ref.dtype)
    half = QK_ROPE_DIM // 2
    wkr = _load_bf16(wkr_ref)
    wkr = jnp.concatenate([wkr, jnp.zeros((V7X_LANES - wkr.shape[0], wkr.shape[1]), BF16)], axis=0)
    x = lax.dot_general(h_ref[...], wkr, (((1,), (1,)), ((), ())), preferred_element_type=F32)
    ang = pos_ref[...] * freq_ref[...]
    lane = lax.broadcasted_iota(jnp.int32, x.shape, 1)
    x2_at_lo = pltpu.roll(x, V7X_LANES - half, axis=1)
    x1_at_hi = pltpu.roll(x, half, axis=1)
    partner = jnp.where(lane < half, -x2_at_lo, jnp.where(lane < QK_ROPE_DIM, x1_at_hi, 0.0))
    kro_ref[...] = (x * jnp.cos(ang) + partner * jnp.sin(ang)).astype(kro_ref.dtype)


def _kv_proj(z, h, w_in_t, kv_norm, wk, wv_t, pos_col, freq_row, *, lat_block, rope_row_block, rank, tm):
    s, d = h.shape
    nk = wk.shape[1]
    nv = wv_t.shape[0]
    n_heads = nv // V_HEAD_DIM
    return pl.pallas_call(
        _kv_proj_kernel,
        grid=(s // tm,),
        in_specs=[
            pl.BlockSpec((tm, rank), lambda i: (i, lat_block)),
            pl.BlockSpec((tm, d), lambda i: (i, 0)),
            pl.BlockSpec((QK_ROPE_DIM, d), lambda i: (rope_row_block, 0)),
            pl.BlockSpec((1, rank), lambda i: (0, 0)),
            pl.BlockSpec((rank, nk), lambda i: (0, 0)),
            pl.BlockSpec((nv, rank), lambda i: (0, 0)),
            pl.BlockSpec((tm, 1), lambda i: (i, 0)),
            pl.BlockSpec((1, V7X_LANES), lambda i: (0, 0)),
        ],
        out_specs=[
            pl.BlockSpec((tm, nk), lambda i: (i, 0)),
            pl.BlockSpec((n_heads, 1, V_HEAD_DIM, tm), lambda i: (0, i, 0, 0)),
            pl.BlockSpec((tm, V7X_LANES), lambda i: (i, 0)),
        ],
        out_shape=[
            jax.ShapeDtypeStruct((s, nk), BF16),
            jax.ShapeDtypeStruct((n_heads, s // tm, V_HEAD_DIM, tm), BF16),
            jax.ShapeDtypeStruct((s, V7X_LANES), BF16),
        ],
        compiler_params=_params(("parallel",),
                                [((tm, rank), F32), ((tm, d), BF16), ((QK_ROPE_DIM, d), w_in_t.dtype), ((rank, nk), BF16),
                                 ((nv, rank), BF16), ((tm, nk), BF16), ((nv, tm), BF16), ((tm, V7X_LANES), BF16),
                                 ((tm, V7X_LANES), F32)],
                                temps=[((tm, nk), F32), ((nv, tm), F32), ((V7X_LANES, d), BF16)]),
        name="kv_proj",
    )(z, h, w_in_t, kv_norm, wk, wv_t, pos_col, freq_row)


def _flash_kernel(kn_ref, kr_ref, q_ref, qnext_ref, vt_ref, o_ref, s_ref, smax_ref, bias_ref, m_ref, l_ref, acc_ref,
                  *, t, unroll):
    heads = q_ref.shape[0]
    qi = pl.program_id(1)
    first = 2

    def scores(kb, slot, q=q_ref):
        ks = pl.ds(pl.multiple_of(kb * t, t), t)
        for hh in range(heads):
            kn = kn_ref[ks, hh * QK_NOPE_DIM:(hh + 1) * QK_NOPE_DIM]
            kcat = jnp.concatenate([kn, kr_ref[ks, :]], axis=1)
            s = jnp.dot(kcat, q[hh, 0], preferred_element_type=F32)
            s_ref[hh, slot] = s
            smax_ref[hh, slot] = jnp.max(s, axis=0, keepdims=True)

    def softmax_pv(kb, slot, masked):
        for hh in range(heads):
            s = s_ref[hh, slot]
            if masked:
                s = s + bias_ref[...]
                block_max = jnp.max(s, axis=0, keepdims=True)
            else:
                block_max = smax_ref[hh, slot]
            m_prev = m_ref[hh]
            m_new = jnp.maximum(m_prev, block_max)
            alpha = jnp.exp2(m_prev - m_new)
            p = jnp.exp2(s - m_new)
            l_ref[hh] = alpha * l_ref[hh] + jnp.sum(p, axis=0, keepdims=True)
            pv = jnp.dot(vt_ref[hh, kb], p.astype(BF16), preferred_element_type=F32)
            acc_ref[hh] = alpha * acc_ref[hh] + pv
            m_ref[hh] = m_new

    m_ref[...] = jnp.full(m_ref.shape, -jnp.inf, F32)
    l_ref[...] = jnp.zeros(l_ref.shape, F32)
    acc_ref[...] = jnp.zeros(acc_ref.shape, F32)

    @pl.when(qi == 0)
    def _():
        kc = lax.broadcasted_iota(jnp.int32, (t, t), 0) // CHUNK
        qc = lax.broadcasted_iota(jnp.int32, (t, t), 1) // CHUNK
        bias_ref[...] = jnp.where(kc <= qc, 0.0, -jnp.inf).astype(F32)
        scores(0, first)
        softmax_pv(0, first, masked=True)
        scores(0, first, qnext_ref)

    @pl.when(qi > 0)
    def _():
        scores(1, 1)
        softmax_pv(0, first, masked=False)
        n_groups = (qi - 1) // unroll

        def group(g, c):
            b0 = 1 + g * unroll
            for u in range(unroll):
                scores(b0 + u + 1, u % 2)
                softmax_pv(b0 + u, (u + 1) % 2, masked=False)
            return c

        lax.fori_loop(0, n_groups, group, 0)
        base = 1 + n_groups * unroll
        rest = qi - base
        for r in range(unroll):
            @pl.when(rest == r)
            def _(r=r):
                for u in range(r):
                    scores(base + u + 1, u % 2)
                    softmax_pv(base + u, (u + 1) % 2, masked=False)
                scores(0, first, qnext_ref)
                softmax_pv(qi, (1 + r) % 2, masked=True)

    for hh in range(heads):
        out = (acc_ref[hh] * (1.0 / l_ref[hh])).T
        o_ref[:, hh * V_HEAD_DIM:(hh + 1) * V_HEAD_DIM] = out.astype(o_ref.dtype)


def _flash_attention(k_nope, k_rope, q_t, v_t, *, n_heads, t, unroll=2, heads_per_step=4):
    s = k_nope.shape[0]
    nq = s // t
    hps = heads_per_step
    assert t % CHUNK == 0 and unroll % 2 == 0 and nq >= 2 and n_heads % hps == 0
    assert v_t.shape == (n_heads, nq, V_HEAD_DIM, t) and q_t.shape == (n_heads, nq, V7X_MXU_DIM, t)
    scratch = [((hps, 3, t, t), F32), ((hps, 3, 1, t), F32), ((t, t), F32), ((hps, 1, t), F32), ((hps, 1, t), F32),
               ((hps, V_HEAD_DIM, t), F32)]
    return pl.pallas_call(
        functools.partial(_flash_kernel, t=t, unroll=unroll),
        grid=(n_heads // hps, nq),
        in_specs=[
            pl.BlockSpec((s, hps * QK_NOPE_DIM), lambda h, i: (0, h)),
            pl.BlockSpec((s, V7X_LANES), lambda h, i: (0, 0)),
            pl.BlockSpec((hps, 1, V7X_MXU_DIM, t), lambda h, i: (h, i, 0, 0)),
            pl.BlockSpec((hps, 1, V7X_MXU_DIM, t), lambda h, i: (h, jnp.minimum(i + 1, nq - 1), 0, 0)),
            pl.BlockSpec((hps, nq, V_HEAD_DIM, t), lambda h, i: (h, 0, 0, 0)),
        ],
        out_specs=pl.BlockSpec((t, hps * V_HEAD_DIM), lambda h, i: (i, h)),
        out_shape=jax.ShapeDtypeStruct((s, n_heads * V_HEAD_DIM), BF16),
        scratch_shapes=[pltpu.VMEM(shape, dtype) for shape, dtype in scratch],
        compiler_params=_params(("arbitrary", "arbitrary"),
                                [((s, hps * QK_NOPE_DIM), BF16), ((s, V7X_LANES), BF16),
                                 ((2 * hps * V7X_MXU_DIM, t), BF16), ((hps * V_HEAD_DIM, s), BF16),
                                 ((t, hps * V_HEAD_DIM), BF16)],
                                scratch=scratch,
                                temps=[((t, t), F32), ((t, t), BF16), ((t, V7X_MXU_DIM), BF16)]),
        name="flash_attention",
    )(k_nope, k_rope, q_t, q_t, v_t)


def _gated_merge_kernel(pm_ref, at_ref, h_ref, wup_ref, wum_ref, wga_ref, wgb_ref, wo_ref, o_ref, wo_o_ref):
    h = h_ref[...]
    ya = jnp.dot(pm_ref[...], wup_ref[...], preferred_element_type=F32)
    yb = jnp.dot(at_ref[...], wum_ref[...], preferred_element_type=F32)
    ga = _sigmoid(jnp.dot(h, wga_ref[...], preferred_element_type=F32))
    gb = _sigmoid(jnp.dot(h, wgb_ref[...], preferred_element_type=F32))
    o_ref[...] = (ga * ya + gb * yb).astype(o_ref.dtype)
    wo_o_ref[...] = wo_ref[...].astype(wo_o_ref.dtype)


def _gated_merge(pm, attn, h, w_up_pool, w_up_mla, w_gate2d, w_out, *, tm=1024, tn=256):
    s, d = h.shape
    kp, km = pm.shape[1], attn.shape[1]
    nb = d // tn
    slab_block, slab_spec = _slab_spec(w_out, (s // tm) * nb, nb)
    return pl.pallas_call(
        _gated_merge_kernel,
        grid=(s // tm, nb),
        in_specs=[
            pl.BlockSpec((tm, kp), lambda i, j: (i, 0)),
            pl.BlockSpec((tm, km), lambda i, j: (i, 0)),
            pl.BlockSpec((tm, d), lambda i, j: (i, 0)),
            pl.BlockSpec((kp, tn), lambda i, j: (0, j)),
            pl.BlockSpec((km, tn), lambda i, j: (0, j)),
            pl.BlockSpec((d, tn), lambda i, j: (0, j)),
            pl.BlockSpec((d, tn), lambda i, j: (0, j + nb)),
            slab_spec,
        ],
        out_specs=[pl.BlockSpec((tm, tn), lambda i, j: (i, j)), slab_spec],
        out_shape=[jax.ShapeDtypeStruct((s, d), BF16), jax.ShapeDtypeStruct(w_out.shape, BF16)],
        compiler_params=_params(("parallel", "parallel"),
                                [((tm, kp), BF16), ((tm, km), BF16), ((tm, d), BF16), ((kp, tn), BF16),
                                 ((km, tn), BF16), ((d, tn), BF16), ((d, tn), BF16), ((tm, tn), BF16),
                                 (slab_block, F32), (slab_block, BF16)],
                                temps=[((tm, tn), F32)] * 4),
        name="gated_merge",
    )(pm, attn, h, w_up_pool, w_up_mla, w_gate2d, w_gate2d, w_out)


def _residual_norm_kernel(t_ref, x_ref, gpost_ref, gnext_ref, xo_ref, ho_ref):
    xo = x_ref[...] + _rms(t_ref[...].astype(F32), gpost_ref[...])
    xo_ref[...] = xo
    ho_ref[...] = _rms(xo, gnext_ref[...]).astype(ho_ref.dtype)


def _residual_final_kernel(t_ref, x_ref, gpost_ref, xo_ref):
    xo_ref[...] = x_ref[...] + _rms(t_ref[...].astype(F32), gpost_ref[...])


def _residual_norm(t, x, g_post, g_next=None, *, tm=256):
    s, d = x.shape
    row = pl.BlockSpec((tm, d), lambda i: (i, 0))
    vec = pl.BlockSpec((1, d), lambda i: (0, 0))
    blocks = [((tm, d), t.dtype), ((tm, d), F32), ((tm, d), F32)]
    if g_next is None:
        return pl.pallas_call(
            _residual_final_kernel, grid=(s // tm,), in_specs=[row, row, vec], out_specs=row,
            out_shape=jax.ShapeDtypeStruct((s, d), F32),
            compiler_params=_params(("parallel",), blocks, temps=[((tm, d), F32)] * 2),
            name="residual_final",
        )(t, x, g_post)
    return pl.pallas_call(
        _residual_norm_kernel, grid=(s // tm,), in_specs=[row, row, vec, vec], out_specs=[row, row],
        out_shape=[jax.ShapeDtypeStruct((s, d), F32), jax.ShapeDtypeStruct((s, d), BF16)],
        compiler_params=_params(("parallel",), blocks + [((tm, d), BF16)], temps=[((tm, d), F32)] * 2),
        name="residual_norm",
    )(t, x, g_post, g_next)


def _swiglu_up_kernel(h_ref, wg_ref, wu_ref, wd_ref, o_ref, wd_o_ref, wg_b_ref, wu_b_ref):
    @pl.when(pl.program_id(1) == 0)
    def _():
        wg_b_ref[...] = _load_bf16(wg_ref)
        wu_b_ref[...] = _load_bf16(wu_ref)

    h = h_ref[...]
    gate = jnp.dot(h, wg_b_ref[...], preferred_element_type=F32)
    up = jnp.dot(h, wu_b_ref[...], preferred_element_type=F32)
    o_ref[...] = (gate * _sigmoid(gate) * up).astype(o_ref.dtype)
    wd_o_ref[...] = wd_ref[...].astype(wd_o_ref.dtype)


def _swiglu_up(h, w_gate, w_up, w_down, *, tm=1024, tn=256):
    s, d = h.shape
    f = w_gate.shape[1]
    ni, nj = s // tm, f // tn
    steps = ni * nj
    slab = w_down.shape[0] // steps
    assert slab * steps == w_down.shape[0] and slab % 16 == 0
    dn = w_down.shape[1]
    return pl.pallas_call(
        _swiglu_up_kernel,
        grid=(nj, ni),
        in_specs=[
            pl.BlockSpec((tm, d), lambda j, i: (i, 0)),
            pl.BlockSpec((d, tn), lambda j, i: (0, j)),
            pl.BlockSpec((d, tn), lambda j, i: (0, j)),
            pl.BlockSpec((slab, dn), lambda j, i: (j * ni + i, 0)),
        ],
        out_specs=[pl.BlockSpec((tm, tn), lambda j, i: (i, j)),
                   pl.BlockSpec((slab, dn), lambda j, i: (j * ni + i, 0))],
        out_shape=[jax.ShapeDtypeStruct((s, f), BF16), jax.ShapeDtypeStruct(w_down.shape, BF16)],
        scratch_shapes=[pltpu.VMEM((d, tn), BF16)] * 2,
        compiler_params=_params(("arbitrary", "arbitrary"),
                                [((tm, d), BF16), ((d, tn), w_gate.dtype), ((d, tn), w_up.dtype), ((tm, tn), BF16),
                                 ((slab, dn), w_down.dtype), ((slab, dn), BF16)],
                                scratch=[((d, tn), BF16)] * 2,
                                temps=[((tm, tn), F32)] * 3),
        name="swiglu_up",
    )(h, w_gate, w_up, w_down)


def _ple_gate_kernel(h_ref, p_ref, wg_ref, wp_ref, o_ref):
    gate = _sigmoid(jnp.dot(h_ref[...], _load_bf16(wg_ref), preferred_element_type=F32))
    pe = jnp.dot(_load_bf16(p_ref), _load_bf16(wp_ref), preferred_element_type=F32)
    o_ref[...] = (pe * gate).astype(o_ref.dtype)


def _ple_gate(h, p, w_gate, w_proj, *, tm=1024, tn=512):
    s, d = h.shape
    r = p.shape[1]
    return pl.pallas_call(
        _ple_gate_kernel,
        grid=(s // tm, d // tn),
        in_specs=[
            pl.BlockSpec((tm, d), lambda i, j: (i, 0)),
            pl.BlockSpec((tm, r), lambda i, j: (i, 0)),
            pl.BlockSpec((d, tn), lambda i, j: (0, j)),
            pl.BlockSpec((r, tn), lambda i, j: (0, j)),
        ],
        out_specs=pl.BlockSpec((tm, tn), lambda i, j: (i, j)),
        out_shape=jax.ShapeDtypeStruct((s, d), BF16),
        compiler_params=_params(("parallel", "parallel"),
                                [((tm, d), BF16), ((tm, r), p.dtype), ((d, tn), w_gate.dtype), ((r, tn), w_proj.dtype),
                                 ((tm, tn), BF16)],
                                temps=[((tm, tn), F32)] * 2 + [((d, tn), BF16)]),
        name="ple_gate",
    )(h, p, w_gate, w_proj)


def kernel(x, p, positions, norm_mix_pre, norm_mix_post, w_in, q_norm, kv_norm, w_q_b, w_kv_b, w_pool, pool_scale,
           w_up_pool, w_up_mla, w_branch_gate, w_out, norm_ffn_pre, norm_ffn_post, w_ffn_gate, w_ffn_up,
           w_ffn_down, norm_ple_pre, w_ple_gate, w_ple_proj, norm_ple_post):
    batch, seq, d_model = x.shape
    depth = w_in.shape[0]
    in_width = w_in.shape[2]
    q_rank, n_heads, qk_dim = w_q_b.shape[1:]
    kv_rank = w_kv_b.shape[1]
    pool_width = w_up_pool.shape[1]
    assert batch == 1 and qk_dim == QK_NOPE_DIM + QK_ROPE_DIM
    assert w_kv_b.shape[3] == QK_NOPE_DIM + V_HEAD_DIM
    o_q, o_kv, o_kr = pool_width, pool_width + q_rank, pool_width + q_rank + kv_rank
    assert o_kr + QK_ROPE_DIM == in_width
    assert o_q % q_rank == 0 and o_kv % kv_rank == 0 and o_kr % QK_ROPE_DIM == 0
    z_tn = 512
    assert o_kr % z_tn == 0
    attn_t = 512
    half = QK_ROPE_DIM // 2
    scale = float(qk_dim) ** -0.5 * LOG2_E

    inv_freq = ROPE_THETA ** (-jnp.arange(0, QK_ROPE_DIM, 2, dtype=F32) / QK_ROPE_DIM)
    freq_col = inv_freq.reshape(half, 1)
    freq_row = jnp.zeros((1, V7X_LANES), F32).at[0, :half].set(inv_freq).at[0, half:QK_ROPE_DIM].set(inv_freq)
    pos_f = positions.astype(F32)
    pos_row, pos_col = pos_f.reshape(1, seq), pos_f.reshape(seq, 1)

    xs = x.reshape(seq, d_model)
    for i in range(depth):
        w_in_t = jnp.transpose(w_in[i])
        wq = w_q_b[i].astype(BF16)
        wq_t = jnp.pad(jnp.transpose(wq, (1, 2, 0)), ((0, 0), (0, V7X_MXU_DIM - qk_dim), (0, 0)))
        wq_t = wq_t.reshape(n_heads * V7X_MXU_DIM, q_rank)
        wkv = w_kv_b[i].astype(BF16)
        wk = wkv[:, :, :QK_NOPE_DIM].reshape(kv_rank, n_heads * QK_NOPE_DIM)
        wv_t = jnp.transpose(wkv[:, :, QK_NOPE_DIM:], (1, 2, 0)).reshape(n_heads * V_HEAD_DIM, kv_rank)
        w_up_p, w_up_m = _cast_pair(w_up_pool[i], w_up_mla[i])

        def row(v):
            return v[i].reshape(1, -1)

        h = _rmsnorm_cast(xs, row(norm_mix_pre))
        z = _matmul(h, w_in_t, tm=1024, tn=z_tn, out_dtype=F32, name="in_proj", b_is_nk=True, n=o_kr)
        pm = _pool_mixer(z, w_pool[i].astype(BF16), row(pool_scale), pool_width=pool_width)
        q_t = _q_proj(z, row(q_norm), wq_t, pos_row, freq_col, col_block=o_q // q_rank, rank=q_rank, scale=scale,
                      tm=attn_t)
        k_nope, v_t, k_rope = _kv_proj(z, h, w_in_t, row(kv_norm), wk, wv_t, pos_col, freq_row,
                                       lat_block=o_kv // kv_rank, rope_row_block=o_kr // QK_ROPE_DIM, rank=kv_rank,
                                       tm=attn_t)
        attn = _flash_attention(k_nope, k_rope, q_t, v_t, n_heads=n_heads, t=attn_t)
        w_gate2d = w_branch_gate[i].astype(BF16).reshape(d_model, 2 * d_model)
        merged, w_out_b = _gated_merge(pm, attn, h, w_up_p, w_up_m, w_gate2d, w_out[i])
        mix = _matmul(merged, w_out_b, tm=1024, tn=1024, out_dtype=BF16, name="out_proj")
        xs, h2 = _residual_norm(mix, xs, row(norm_mix_post), row(norm_ffn_pre))
        act, w_down = _swiglu_up(h2, w_ffn_gate[i], w_ffn_up[i], w_ffn_down[i])
        ffn, w_pg = _matmul(act, w_down, tm=512, tn=512, out_dtype=BF16, name="ffn_down", side_cast=w_ple_gate[i])
        xs, h3 = _residual_norm(ffn, xs, row(norm_ffn_post), row(norm_ple_pre))
        t = _ple_gate(h3, p[i].reshape(seq, -1), w_pg, w_ple_proj[i], tn=1024)
        xs = _residual_norm(t, xs, row(norm_ple_post))
    return xs.reshape(batch, seq, d_model)
```

```python
import functools

import jax
import jax.numpy as jnp
from jax import lax
from jax.experimental import pallas as pl
from jax.experimental.pallas import tpu as pltpu

CHUNK = 64
EPS = 1e-6
POOL_WINDOWS = (2, 4, 8, 16)
QK_NOPE_DIM = 128
QK_ROPE_DIM = 64
V_HEAD_DIM = 128
ROPE_THETA = 10000.0
LOG2_E = 1.4426950408889634

V7X_LANES = 128
V7X_MXU_DIM = 256
V7X_VMEM_BYTES = 64 * 1024 * 1024
V7X_VMEM_USABLE_BYTES = 60000 * 1024
V7X_VMEM_DEFAULT_SCOPED_BYTES = 32 * 1024 * 1024

POOL_HALO = 16
POOL_PAD = 8
F32 = jnp.float32
BF16 = jnp.bfloat16


def _nbytes(shape, dtype):
    n = 1
    for s in shape:
        n *= s
    return n * jnp.dtype(dtype).itemsize


def _params(semantics, blocks, scratch=(), temps=()):
    need = 2 * sum(_nbytes(s, d) for s, d in blocks)
    need += sum(_nbytes(s, d) for s, d in scratch) + sum(_nbytes(s, d) for s, d in temps)
    need = max(int(need * 1.25) + (2 << 20), V7X_VMEM_DEFAULT_SCOPED_BYTES)
    return pltpu.CompilerParams(dimension_semantics=semantics,
                                vmem_limit_bytes=min(need, V7X_VMEM_USABLE_BYTES))


def _load_bf16(ref):
    v = ref[...]
    return v if v.dtype == BF16 else v.astype(BF16)


def _sigmoid(x):
    return 0.5 * jnp.tanh(0.5 * x) + 0.5


def _rms(xf, g):
    return xf * lax.rsqrt(jnp.mean(xf * xf, axis=-1, keepdims=True) + EPS) * g


def _rmsnorm_cast_kernel(x_ref, g_ref, o_ref):
    o_ref[...] = _rms(x_ref[...], g_ref[...]).astype(o_ref.dtype)


def _rmsnorm_cast(x, g, *, tm=512):
    s, d = x.shape
    return pl.pallas_call(
        _rmsnorm_cast_kernel,
        grid=(s // tm,),
        in_specs=[pl.BlockSpec((tm, d), lambda i: (i, 0)), pl.BlockSpec((1, d), lambda i: (0, 0))],
        out_specs=pl.BlockSpec((tm, d), lambda i: (i, 0)),
        out_shape=jax.ShapeDtypeStruct((s, d), BF16),
        compiler_params=_params(("parallel",), [((tm, d), F32), ((tm, d), BF16)], temps=[((tm, d), F32)]),
        name="rmsnorm_cast",
    )(x, g)


def _slab_spec(w, steps, nj):
    rows = w.shape[0]
    n_slabs = max(c for c in range(1, steps + 1) if rows % c == 0 and (rows // c) % 16 == 0)
    slab = rows // n_slabs
    return (slab, w.shape[1]), pl.BlockSpec((slab, w.shape[1]), lambda i, j: (jnp.minimum(i * nj + j, n_slabs - 1), 0))


def _matmul_kernel(a_ref, b_ref, *rest, b_is_nk):
    n_side = (len(rest) - 1) // 2
    o_ref = rest[n_side]
    contract_b = 1 if b_is_nk else 0
    o_ref[...] = lax.dot_general(a_ref[...], _load_bf16(b_ref), (((1,), (contract_b,)), ((), ())),
                                 preferred_element_type=F32).astype(o_ref.dtype)
    for src, dst in zip(rest[:n_side], rest[n_side + 1:]):
        dst[...] = src[...].astype(dst.dtype)


def _matmul(a, b, *, tm, tn, out_dtype, name, b_is_nk=False, n=None, side_cast=()):
    m, k = a.shape
    if n is None:
        n = b.shape[0] if b_is_nk else b.shape[1]
    if b_is_nk:
        b_block, b_spec = (tn, k), pl.BlockSpec((tn, k), lambda i, j: (j, 0))
    else:
        b_block, b_spec = (k, tn), pl.BlockSpec((k, tn), lambda i, j: (0, j))
    grid = (m // tm, n // tn)
    in_specs = [pl.BlockSpec((tm, k), lambda i, j: (i, 0)), b_spec]
    out_specs = pl.BlockSpec((tm, tn), lambda i, j: (i, j))
    out_shape = jax.ShapeDtypeStruct((m, n), out_dtype)
    blocks = [((tm, k), a.dtype), (b_block, b.dtype), ((tm, tn), out_dtype)]
    args = (a, b) + tuple(side_cast)
    if side_cast:
        slabs = [_slab_spec(w, grid[0] * grid[1], grid[1]) for w in side_cast]
        in_specs += [spec for _, spec in slabs]
        out_specs = [out_specs] + [spec for _, spec in slabs]
        out_shape = [out_shape] + [jax.ShapeDtypeStruct(w.shape, BF16) for w in side_cast]
        blocks += [(block, dt) for block, _ in slabs for dt in (F32, BF16)]
    return pl.pallas_call(
        functools.partial(_matmul_kernel, b_is_nk=b_is_nk),
        grid=grid, in_specs=in_specs, out_specs=out_specs, out_shape=out_shape,
        compiler_params=_params(("parallel", "parallel"), blocks, temps=[((tm, tn), F32), ((k, tn), BF16)]),
        name=name,
    )(*args)


def _pool_mixer_kernel(halo_ref, u_ref, w_ref, scale_ref, o_ref, ext_ref, lvl_a_ref, lvl_b_ref, *, tm, gw):
    i = pl.program_id(0)
    groups = len(POOL_WINDOWS)
    base = POOL_PAD + POOL_HALO
    n = POOL_HALO + tm
    width = ext_ref.shape[1]
    for ref in (ext_ref, lvl_a_ref, lvl_b_ref):
        ref[0:POOL_PAD, :] = jnp.zeros((POOL_PAD, width), F32)
    ext_ref[POOL_PAD:base, :] = jnp.where(i > 0, halo_ref[...], 0.0)
    ext_ref[base:base + tm, :] = u_ref[...]
    row = lax.broadcasted_iota(jnp.int32, (tm, 1), 0) + i * tm
    src = ext_ref
    for k, w in enumerate(POOL_WINDOWS):
        span = w // 2
        cols = slice(k * gw, (k + 1) * gw)
        u = u_ref[:, cols]
        win_sum = src[base:base + tm, cols] + src[base - span:base - span + tm, cols]
        inv_cnt = 1.0 / jnp.minimum(row + 1, w).astype(F32)
        pooled = win_sum * inv_cnt - u
        mixed = jnp.dot(pooled.astype(BF16), w_ref[k], preferred_element_type=F32)
        o_ref[:, cols] = (mixed * scale_ref[:, cols]).astype(o_ref.dtype)
        if k + 1 < groups:
            rest = slice((k + 1) * gw, width)
            dst = lvl_a_ref if k % 2 == 0 else lvl_b_ref
            dst[POOL_PAD:POOL_PAD + n, rest] = (src[POOL_PAD:POOL_PAD + n, rest]
                                                + src[POOL_PAD - span:POOL_PAD - span + n, rest])
            src = dst


def _pool_mixer(z, w_pool, pool_scale, *, pool_width, tm=512):
    s = z.shape[0]
    groups, gw, _ = w_pool.shape
    assert all(w == 2 ** (k + 1) for k, w in enumerate(POOL_WINDOWS)) and POOL_WINDOWS[-1] // 2 <= POOL_PAD
    halo_blocks = tm // POOL_HALO
    buf = (POOL_PAD + POOL_HALO + tm, pool_width)
    return pl.pallas_call(
        functools.partial(_pool_mixer_kernel, tm=tm, gw=gw),
        grid=(s // tm,),
        in_specs=[
            pl.BlockSpec((POOL_HALO, pool_width), lambda i: (jnp.maximum(i * halo_blocks - 1, 0), 0)),
            pl.BlockSpec((tm, pool_width), lambda i: (i, 0)),
            pl.BlockSpec((groups, gw, gw), lambda i: (0, 0, 0)),
            pl.BlockSpec((1, pool_width), lambda i: (0, 0)),
        ],
        out_specs=pl.BlockSpec((tm, pool_width), lambda i: (i, 0)),
        out_shape=jax.ShapeDtypeStruct((s, pool_width), BF16),
        scratch_shapes=[pltpu.VMEM(buf, F32)] * 3,
        compiler_params=_params(("parallel",),
                                [((tm, pool_width), F32), ((groups, gw, gw), BF16), ((tm, pool_width), BF16)],
                                scratch=[(buf, F32)] * 3,
                                temps=[((tm, pool_width), F32)]),
        name="pool_mixer",
    )(z, z, w_pool, pool_scale)


def _q_proj_kernel(ql_ref, g_ref, w_ref, pos_ref, freq_ref, o_ref, qn_ref, cos_ref, sin_ref, *, scale):
    @pl.when(pl.program_id(1) == 0)
    def _():
        qn_ref[...] = (_rms(ql_ref[...], g_ref[...]) * scale).astype(qn_ref.dtype)
        ang = freq_ref[...] * pos_ref[...]
        cos_ref[...] = jnp.cos(ang)
        sin_ref[...] = jnp.sin(ang)

    qt = lax.dot_general(w_ref[...], qn_ref[...], (((1,), (1,)), ((), ())), preferred_element_type=F32)
    half = QK_ROPE_DIM // 2
    r0, r1, r2 = QK_NOPE_DIM, QK_NOPE_DIM + half, QK_NOPE_DIM + QK_ROPE_DIM
    c, sn = cos_ref[...], sin_ref[...]
    rows = o_ref.shape[2]
    for hh in range(o_ref.shape[0]):
        q = qt[hh * rows:(hh + 1) * rows]
        x1, x2 = q[r0:r1], q[r1:r2]
        o_ref[hh, 0, 0:r0, :] = q[0:r0].astype(o_ref.dtype)
        o_ref[hh, 0, r0:r1, :] = (x1 * c - x2 * sn).astype(o_ref.dtype)
        o_ref[hh, 0, r1:r2, :] = (x1 * sn + x2 * c).astype(o_ref.dtype)
        o_ref[hh, 0, r2:, :] = q[r2:].astype(o_ref.dtype)


def _q_proj(z, q_norm, wq_t, pos_row, freq_col, *, col_block, rank, scale, tm, heads_per_step=8):
    s = z.shape[0]
    rows = V7X_MXU_DIM
    n_heads = wq_t.shape[0] // rows
    hps = heads_per_step
    half = QK_ROPE_DIM // 2
    return pl.pallas_call(
        functools.partial(_q_proj_kernel, scale=scale),
        grid=(s // tm, n_heads // hps),
        in_specs=[
            pl.BlockSpec((tm, rank), lambda i, h: (i, col_block)),
            pl.BlockSpec((1, rank), lambda i, h: (0, 0)),
            pl.BlockSpec((hps * rows, rank), lambda i, h: (h, 0)),
            pl.BlockSpec((1, tm), lambda i, h: (0, i)),
            pl.BlockSpec((half, 1), lambda i, h: (0, 0)),
        ],
        out_specs=pl.BlockSpec((hps, 1, rows, tm), lambda i, h: (h, i, 0, 0)),
        out_shape=jax.ShapeDtypeStruct((n_heads, s // tm, rows, tm), BF16),
        scratch_shapes=[pltpu.VMEM((tm, rank), BF16), pltpu.VMEM((half, tm), F32), pltpu.VMEM((half, tm), F32)],
        compiler_params=_params(("parallel", "arbitrary"),
                                [((tm, rank), F32), ((hps * rows, rank), BF16), ((hps * rows, tm), BF16)],
                                scratch=[((tm, rank), BF16), ((2 * half, tm), F32)],
                                temps=[((tm, rank), F32), ((hps * rows, tm), F32)]),
        name="q_proj",
    )(z, q_norm, wq_t, pos_row, freq_col)


def _kv_proj_kernel(kvl_ref, h_ref, wkr_ref, g_ref, wk_ref, wvt_ref, pos_ref, freq_ref, k_ref, vt_ref, kro_ref):
    kvn = _rms(kvl_ref[...], g_ref[...]).astype(BF16)
    k_ref[...] = jnp.dot(kvn, wk_ref[...], preferred_element_type=F32).astype(k_ref.dtype)
    vt = lax.dot_general(wvt_ref[...], kvn, (((1,), (1,)), ((), ())), preferred_element_type=F32)
    vt_ref[:, 0] = vt.reshape(vt_ref.shape[0], V_HEAD_DIM, vt.shape[1]).astype(vt_ref.dtype)
    half = QK_ROPE_DIM // 2
    wkr = _load_bf16(wkr_ref)
    wkr = jnp.concatenate([wkr, jnp.zeros((V7X_LANES - wkr.shape[0], wkr.shape[1]), BF16)], axis=0)
    x = lax.dot_general(h_ref[...], wkr, (((1,), (1,)), ((), ())), preferred_element_type=F32)
    ang = pos_ref[...] * freq_ref[...]
    lane = lax.broadcasted_iota(jnp.int32, x.shape, 1)
    x2_at_lo = pltpu.roll(x, V7X_LANES - half, axis=1)
    x1_at_hi = pltpu.roll(x, half, axis=1)
    partner = jnp.where(lane < half, -x2_at_lo, jnp.where(lane < QK_ROPE_DIM, x1_at_hi, 0.0))
    kro_ref[...] = (x * jnp.cos(ang) + partner * jnp.sin(ang)).astype(kro_ref.dtype)


def _kv_proj(z, h, w_in_t, kv_norm, wk, wv_t, pos_col, freq_row, *, lat_block, rope_row_block, rank, tm):
    s, d = h.shape
    nk = wk.shape[1]
    nv = wv_t.shape[0]
    n_heads = nv // V_HEAD_DIM
    return pl.pallas_call(
        _kv_proj_kernel,
        grid=(s // tm,),
        in_specs=[
            pl.BlockSpec((tm, rank), lambda i: (i, lat_block)),
            pl.BlockSpec((tm, d), lambda i: (i, 0)),
            pl.BlockSpec((QK_ROPE_DIM, d), lambda i: (rope_row_block, 0)),
            pl.BlockSpec((1, rank), lambda i: (0, 0)),
            pl.BlockSpec((rank, nk), lambda i: (0, 0)),
            pl.BlockSpec((nv, rank), lambda i: (0, 0)),
            pl.BlockSpec((tm, 1), lambda i: (i, 0)),
            pl.BlockSpec((1, V7X_LANES), lambda i: (0, 0)),
        ],
        out_specs=[
            pl.BlockSpec((tm, nk), lambda i: (i, 0)),
            pl.BlockSpec((n_heads, 1, V_HEAD_DIM, tm), lambda i: (0, i, 0, 0)),
            pl.BlockSpec((tm, V7X_LANES), lambda i: (i, 0)),
        ],
        out_shape=[
            jax.ShapeDtypeStruct((s, nk), BF16),
            jax.ShapeDtypeStruct((n_heads, s // tm, V_HEAD_DIM, tm), BF16),
            jax.ShapeDtypeStruct((s, V7X_LANES), BF16),
        ],
        compiler_params=_params(("parallel",),
                                [((tm, rank), F32), ((tm, d), BF16), ((QK_ROPE_DIM, d), w_in_t.dtype), ((rank, nk), BF16),
                                 ((nv, rank), BF16), ((tm, nk), BF16), ((nv, tm), BF16), ((tm, V7X_LANES), BF16),
                                 ((tm, V7X_LANES), F32)],
                                temps=[((tm, nk), F32), ((nv, tm), F32), ((V7X_LANES, d), BF16)]),
        name="kv_proj",
    )(z, h, w_in_t, kv_norm, wk, wv_t, pos_col, freq_row)


def _flash_kernel(kn_ref, kr_ref, q_ref, qnext_ref, vt_ref, o_ref, s_ref, smax_ref, bias_ref, m_ref, l_ref, acc_ref,
                  *, t, unroll):
    heads = q_ref.shape[0]
    qi = pl.program_id(1)
    first = 2

    def scores(kb, slot, q=q_ref):
        ks = pl.ds(pl.multiple_of(kb * t, t), t)
        for hh in range(heads):
            kn = kn_ref[ks, hh * QK_NOPE_DIM:(hh + 1) * QK_NOPE_DIM]
            kcat = jnp.concatenate([kn, kr_ref[ks, :]], axis=1)
            s = jnp.dot(kcat, q[hh, 0], preferred_element_type=F32)
            s_ref[hh, slot] = s
            smax_ref[hh, slot] = jnp.max(s, axis=0, keepdims=True)

    def softmax_pv(kb, slot, masked):
        for hh in range(heads):
            s = s_ref[hh, slot]
            if masked:
                s = s + bias_ref[...]
                block_max = jnp.max(s, axis=0, keepdims=True)
            else:
                block_max = smax_ref[hh, slot]
            m_prev = m_ref[hh]
            m_new = jnp.maximum(m_prev, block_max)
            alpha = jnp.exp2(m_prev - m_new)
            p = jnp.exp2(s - m_new)
            l_ref[hh] = alpha * l_ref[hh] + jnp.sum(p, axis=0, keepdims=True)
            pv = jnp.dot(vt_ref[hh, kb], p.astype(BF16), preferred_element_type=F32)
            acc_ref[hh] = alpha * acc_ref[hh] + pv
            m_ref[hh] = m_new

    m_ref[...] = jnp.full(m_ref.shape, -jnp.inf, F32)
    l_ref[...] = jnp.zeros(l_ref.shape, F32)
    acc_ref[...] = jnp.zeros(acc_ref.shape, F32)

    @pl.when(qi == 0)
    def _():
        kc = lax.broadcasted_iota(jnp.int32, (t, t), 0) // CHUNK
        qc = lax.broadcasted_iota(jnp.int32, (t, t), 1) // CHUNK
        bias_ref[...] = jnp.where(kc <= qc, 0.0, -jnp.inf).astype(F32)
        scores(0, first)
        softmax_pv(0, first, masked=True)
        scores(0, first, qnext_ref)

    @pl.when(qi > 0)
    def _():
        scores(1, 1)
        softmax_pv(0, first, masked=False)
        n_groups = (qi - 1) // unroll

        def group(g, c):
            b0 = 1 + g * unroll
            for u in range(unroll):
                scores(b0 + u + 1, u % 2)
                softmax_pv(b0 + u, (u + 1) % 2, masked=False)
            return c

        lax.fori_loop(0, n_groups, group, 0)
        base = 1 + n_groups * unroll
        rest = qi - base
        for r in range(unroll):
            @pl.when(rest == r)
            def _(r=r):
                for u in range(r):
                    scores(base + u + 1, u % 2)
                    softmax_pv(base + u, (u + 1) % 2, masked=False)
                scores(0, first, qnext_ref)
                softmax_pv(qi, (1 + r) % 2, masked=True)

    for hh in range(heads):
        out = (acc_ref[hh] * (1.0 / l_ref[hh])).T
        o_ref[:, hh * V_HEAD_DIM:(hh + 1) * V_HEAD_DIM] = out.astype(o_ref.dtype)


def _flash_attention(k_nope, k_rope, q_t, v_t, *, n_heads, t, unroll=2, heads_per_step=4):
    s = k_nope.shape[0]
    nq = s // t
    hps = heads_per_step
    assert t % CHUNK == 0 and unroll % 2 == 0 and nq >= 2 and n_heads % hps == 0
    assert v_t.shape == (n_heads, nq, V_HEAD_DIM, t) and q_t.shape == (n_heads, nq, V7X_MXU_DIM, t)
    scratch = [((hps, 3, t, t), F32), ((hps, 3, 1, t), F32), ((t, t), F32), ((hps, 1, t), F32), ((hps, 1, t), F32),
               ((hps, V_HEAD_DIM, t), F32)]
    return pl.pallas_call(
        functools.partial(_flash_kernel, t=t, unroll=unroll),
        grid=(n_heads // hps, nq),
        in_specs=[
            pl.BlockSpec((s, hps * QK_NOPE_DIM), lambda h, i: (0, h)),
            pl.BlockSpec((s, V7X_LANES), lambda h, i: (0, 0)),
            pl.BlockSpec((hps, 1, V7X_MXU_DIM, t), lambda h, i: (h, i, 0, 0)),
            pl.BlockSpec((hps, 1, V7X_MXU_DIM, t), lambda h, i: (h, jnp.minimum(i + 1, nq - 1), 0, 0)),
            pl.BlockSpec((hps, nq, V_HEAD_DIM, t), lambda h, i: (h, 0, 0, 0)),
        ],
        out_specs=pl.BlockSpec((t, hps * V_HEAD_DIM), lambda h, i: (i, h)),
        out_shape=jax.ShapeDtypeStruct((s, n_heads * V_HEAD_DIM), BF16),
        scratch_shapes=[pltpu.VMEM(shape, dtype) for shape, dtype in scratch],
        compiler_params=_params(("arbitrary", "arbitrary"),
                                [((s, hps * QK_NOPE_DIM), BF16), ((s, V7X_LANES), BF16),
                                 ((2 * hps * V7X_MXU_DIM, t), BF16), ((hps * V_HEAD_DIM, s), BF16),
                                 ((t, hps * V_HEAD_DIM), BF16)],
                                scratch=scratch,
                                temps=[((t, t), F32), ((t, t), BF16), ((t, V7X_MXU_DIM), BF16)]),
        name="flash_attention",
    )(k_nope, k_rope, q_t, q_t, v_t)


def _gated_merge_kernel(pm_ref, at_ref, h_ref, wup_ref, wum_ref, wga_ref, wgb_ref, wo_ref, o_ref, wo_o_ref):
    h = h_ref[...]
    ya = jnp.dot(pm_ref[...], wup_ref[...], preferred_element_type=F32)
    yb = jnp.dot(at_ref[...], wum_ref[...], preferred_element_type=F32)
    ga = _sigmoid(jnp.dot(h, wga_ref[...], preferred_element_type=F32))
    gb = _sigmoid(jnp.dot(h, wgb_ref[...], preferred_element_type=F32))
    o_ref[...] = (ga * ya + gb * yb).astype(o_ref.dtype)
    wo_o_ref[...] = wo_ref[...].astype(wo_o_ref.dtype)


def _gated_merge(pm, attn, h, w_up_pool, w_up_mla, w_gate2d, w_out, *, tm=1024, tn=256):
    s, d = h.shape
    kp, km = pm.shape[1], attn.shape[1]
    nb = d // tn
    slab_block, slab_spec = _slab_spec(w_out, (s // tm) * nb, nb)
    return pl.pallas_call(
        _gated_merge_kernel,
        grid=(s // tm, nb),
        in_specs=[
            pl.BlockSpec((tm, kp), lambda i, j: (i, 0)),
            pl.BlockSpec((tm, km), lambda i, j: (i, 0)),
            pl.BlockSpec((tm, d), lambda i, j: (i, 0)),
            pl.BlockSpec((kp, tn), lambda i, j: (0, j)),
            pl.BlockSpec((km, tn), lambda i, j: (0, j)),
            pl.BlockSpec((d, tn), lambda i, j: (0, j)),
            pl.BlockSpec((d, tn), lambda i, j: (0, j + nb)),
            slab_spec,
        ],
        out_specs=[pl.BlockSpec((tm, tn), lambda i, j: (i, j)), slab_spec],
        out_shape=[jax.ShapeDtypeStruct((s, d), BF16), jax.ShapeDtypeStruct(w_out.shape, BF16)],
        compiler_params=_params(("parallel", "parallel"),
                                [((tm, kp), BF16), ((tm, km), BF16), ((tm, d), BF16), ((kp, tn), BF16),
                                 ((km, tn), BF16), ((d, tn), BF16), ((d, tn), BF16), ((tm, tn), BF16),
                                 (slab_block, F32), (slab_block, BF16)],
                                temps=[((tm, tn), F32)] * 4),
        name="gated_merge",
    )(pm, attn, h, w_up_pool, w_up_mla, w_gate2d, w_gate2d, w_out)


def _residual_norm_kernel(t_ref, x_ref, gpost_ref, gnext_ref, xo_ref, ho_ref):
    xo = x_ref[...] + _rms(t_ref[...].astype(F32), gpost_ref[...])
    xo_ref[...] = xo
    ho_ref[...] = _rms(xo, gnext_ref[...]).astype(ho_ref.dtype)


def _residual_final_kernel(t_ref, x_ref, gpost_ref, xo_ref):
    xo_ref[...] = x_ref[...] + _rms(t_ref[...].astype(F32), gpost_ref[...])


def _residual_norm(t, x, g_post, g_next=None, *, tm=256):
    s, d = x.shape
    row = pl.BlockSpec((tm, d), lambda i: (i, 0))
    vec = pl.BlockSpec((1, d), lambda i: (0, 0))
    blocks = [((tm, d), t.dtype), ((tm, d), F32), ((tm, d), F32)]
    if g_next is None:
        return pl.pallas_call(
            _residual_final_kernel, grid=(s // tm,), in_specs=[row, row, vec], out_specs=row,
            out_shape=jax.ShapeDtypeStruct((s, d), F32),
            compiler_params=_params(("parallel",), blocks, temps=[((tm, d), F32)] * 2),
            name="residual_final",
        )(t, x, g_post)
    return pl.pallas_call(
        _residual_norm_kernel, grid=(s // tm,), in_specs=[row, row, vec, vec], out_specs=[row, row],
        out_shape=[jax.ShapeDtypeStruct((s, d), F32), jax.ShapeDtypeStruct((s, d), BF16)],
        compiler_params=_params(("parallel",), blocks + [((tm, d), BF16)], temps=[((tm, d), F32)] * 2),
        name="residual_norm",
    )(t, x, g_post, g_next)


def _swiglu_up_kernel(h_ref, wg_ref, wu_ref, wd_ref, o_ref, wd_o_ref):
    h = h_ref[...]
    gate = jnp.dot(h, _load_bf16(wg_ref), preferred_element_type=F32)
    up = jnp.dot(h, _load_bf16(wu_ref), preferred_element_type=F32)
    o_ref[...] = (gate * _sigmoid(gate) * up).astype(o_ref.dtype)
    wd_o_ref[...] = wd_ref[...].astype(wd_o_ref.dtype)


def _swiglu_up(h, w_gate, w_up, w_down, *, tm=1024, tn=256):
    s, d = h.shape
    f = w_gate.shape[1]
    nj = f // tn
    steps = (s // tm) * nj
    slab = w_down.shape[0] // steps
    assert slab * steps == w_down.shape[0] and slab % 16 == 0
    dn = w_down.shape[1]
    return pl.pallas_call(
        _swiglu_up_kernel,
        grid=(s // tm, nj),
        in_specs=[
            pl.BlockSpec((tm, d), lambda i, j: (i, 0)),
            pl.BlockSpec((d, tn), lambda i, j: (0, j)),
            pl.BlockSpec((d, tn), lambda i, j: (0, j)),
            pl.BlockSpec((slab, dn), lambda i, j: (i * nj + j, 0)),
        ],
        out_specs=[pl.BlockSpec((tm, tn), lambda i, j: (i, j)),
                   pl.BlockSpec((slab, dn), lambda i, j: (i * nj + j, 0))],
        out_shape=[jax.ShapeDtypeStruct((s, f), BF16), jax.ShapeDtypeStruct(w_down.shape, BF16)],
        compiler_params=_params(("parallel", "parallel"),
                                [((tm, d), BF16), ((d, tn), w_gate.dtype), ((d, tn), w_up.dtype), ((tm, tn), BF16),
                                 ((slab, dn), w_down.dtype), ((slab, dn), BF16)],
                                temps=[((tm, tn), F32)] * 3 + [((d, tn), BF16)] * 2),
        name="swiglu_up",
    )(h, w_gate, w_up, w_down)


def _ple_gate_kernel(h_ref, p_ref, wg_ref, wp_ref, o_ref):
    gate = _sigmoid(jnp.dot(h_ref[...], _load_bf16(wg_ref), preferred_element_type=F32))
    pe = jnp.dot(_load_bf16(p_ref), _load_bf16(wp_ref), preferred_element_type=F32)
    o_ref[...] = (pe * gate).astype(o_ref.dtype)


def _ple_gate(h, p, w_gate, w_proj, *, tm=1024, tn=512):
    s, d = h.shape
    r = p.shape[1]
    return pl.pallas_call(
        _ple_gate_kernel,
        grid=(s // tm, d // tn),
        in_specs=[
            pl.BlockSpec((tm, d), lambda i, j: (i, 0)),
            pl.BlockSpec((tm, r), lambda i, j: (i, 0)),
            pl.BlockSpec((d, tn), lambda i, j: (0, j)),
            pl.BlockSpec((r, tn), lambda i, j: (0, j)),
        ],
        out_specs=pl.BlockSpec((tm, tn), lambda i, j: (i, j)),
        out_shape=jax.ShapeDtypeStruct((s, d), BF16),
        compiler_params=_params(("parallel", "parallel"),
                                [((tm, d), BF16), ((tm, r), p.dtype), ((d, tn), w_gate.dtype), ((r, tn), w_proj.dtype),
                                 ((tm, tn), BF16)],
                                temps=[((tm, tn), F32)] * 2 + [((d, tn), BF16)]),
        name="ple_gate",
    )(h, p, w_gate, w_proj)


def kernel(x, p, positions, norm_mix_pre, norm_mix_post, w_in, q_norm, kv_norm, w_q_b, w_kv_b, w_pool, pool_scale,
           w_up_pool, w_up_mla, w_branch_gate, w_out, norm_ffn_pre, norm_ffn_post, w_ffn_gate, w_ffn_up,
           w_ffn_down, norm_ple_pre, w_ple_gate, w_ple_proj, norm_ple_post):
    batch, seq, d_model = x.shape
    depth = w_in.shape[0]
    in_width = w_in.shape[2]
    q_rank, n_heads, qk_dim = w_q_b.shape[1:]
    kv_rank = w_kv_b.shape[1]
    pool_width = w_up_pool.shape[1]
    assert batch == 1 and qk_dim == QK_NOPE_DIM + QK_ROPE_DIM
    assert w_kv_b.shape[3] == QK_NOPE_DIM + V_HEAD_DIM
    o_q, o_kv, o_kr = pool_width, pool_width + q_rank, pool_width + q_rank + kv_rank
    assert o_kr + QK_ROPE_DIM == in_width
    assert o_q % q_rank == 0 and o_kv % kv_rank == 0 and o_kr % QK_ROPE_DIM == 0
    z_tn = 512
    assert o_kr % z_tn == 0
    attn_t = 512
    half = QK_ROPE_DIM // 2
    scale = float(qk_dim) ** -0.5 * LOG2_E

    inv_freq = ROPE_THETA ** (-jnp.arange(0, QK_ROPE_DIM, 2, dtype=F32) / QK_ROPE_DIM)
    freq_col = inv_freq.reshape(half, 1)
    freq_row = jnp.zeros((1, V7X_LANES), F32).at[0, :half].set(inv_freq).at[0, half:QK_ROPE_DIM].set(inv_freq)
    pos_f = positions.astype(F32)
    pos_row, pos_col = pos_f.reshape(1, seq), pos_f.reshape(seq, 1)

    xs = x.reshape(seq, d_model)
    for i in range(depth):
        w_in_t = jnp.transpose(w_in[i])
        wq = w_q_b[i].astype(BF16)
        wq_t = jnp.pad(jnp.transpose(wq, (1, 2, 0)), ((0, 0), (0, V7X_MXU_DIM - qk_dim), (0, 0)))
        wq_t = wq_t.reshape(n_heads * V7X_MXU_DIM, q_rank)
        wkv = w_kv_b[i].astype(BF16)
        wk = wkv[:, :, :QK_NOPE_DIM].reshape(kv_rank, n_heads * QK_NOPE_DIM)
        wv_t = jnp.transpose(wkv[:, :, QK_NOPE_DIM:], (1, 2, 0)).reshape(n_heads * V_HEAD_DIM, kv_rank)

        def row(v):
            return v[i].reshape(1, -1)

        h = _rmsnorm_cast(xs, row(norm_mix_pre))
        z, w_up_p, w_up_m = _matmul(h, w_in_t, tm=1024, tn=z_tn, out_dtype=F32, name="in_proj", b_is_nk=True, n=o_kr,
                                    side_cast=(w_up_pool[i], w_up_mla[i]))
        pm = _pool_mixer(z, w_pool[i].astype(BF16), row(pool_scale), pool_width=pool_width)
        q_t = _q_proj(z, row(q_norm), wq_t, pos_row, freq_col, col_block=o_q // q_rank, rank=q_rank, scale=scale,
                      tm=attn_t)
        k_nope, v_t, k_rope = _kv_proj(z, h, w_in_t, row(kv_norm), wk, wv_t, pos_col, freq_row,
                                       lat_block=o_kv // kv_rank, rope_row_block=o_kr // QK_ROPE_DIM, rank=kv_rank,
                                       tm=attn_t)
        attn = _flash_attention(k_nope, k_rope, q_t, v_t, n_heads=n_heads, t=attn_t)
        w_gate2d = w_branch_gate[i].astype(BF16).reshape(d_model, 2 * d_model)
        merged, w_out_b = _gated_merge(pm, attn, h, w_up_p, w_up_m, w_gate2d, w_out[i])
        mix = _matmul(merged, w_out_b, tm=1024, tn=1024, out_dtype=BF16, name="out_proj")
        xs, h2 = _residual_norm(mix, xs, row(norm_mix_post), row(norm_ffn_pre))
        act, w_down = _swiglu_up(h2, w_ffn_gate[i], w_ffn_up[i], w_ffn_down[i])
        ffn, w_pg = _matmul(act, w_down, tm=512, tn=512, out_dtype=BF16, name="ffn_down", side_cast=(w_ple_gate[i],))
        xs, h3 = _residual_norm(ffn, xs, row(norm_ffn_post), row(norm_ple_pre))
        t = _ple_gate(h3, p[i].reshape(seq, -1), w_pg, w_ple_proj[i], tn=1024)
        xs = _residual_norm(t, xs, row(norm_ple_post))
    return xs.reshape(batch, seq, d_model)
```

```python
import functools

import jax
import jax.numpy as jnp
from jax import lax
from jax.experimental import pallas as pl
from jax.experimental.pallas import tpu as pltpu

CHUNK = 64
EPS = 1e-6
POOL_WINDOWS = (2, 4, 8, 16)
QK_NOPE_DIM = 128
QK_ROPE_DIM = 64
V_HEAD_DIM = 128
ROPE_THETA = 10000.0
LOG2_E = 1.4426950408889634

V7X_LANES = 128
V7X_MXU_DIM = 256
V7X_VMEM_BYTES = 64 * 1024 * 1024
V7X_VMEM_USABLE_BYTES = 60000 * 1024
V7X_VMEM_DEFAULT_SCOPED_BYTES = 32 * 1024 * 1024

POOL_HALO = 16
POOL_PAD = 8
F32 = jnp.float32
BF16 = jnp.bfloat16


def _nbytes(shape, dtype):
    n = 1
    for s in shape:
        n *= s
    return n * jnp.dtype(dtype).itemsize


def _params(semantics, blocks, scratch=(), temps=()):
    need = 2 * sum(_nbytes(s, d) for s, d in blocks)
    need += sum(_nbytes(s, d) for s, d in scratch) + sum(_nbytes(s, d) for s, d in temps)
    need = max(int(need * 1.25) + (2 << 20), V7X_VMEM_DEFAULT_SCOPED_BYTES)
    return pltpu.CompilerParams(dimension_semantics=semantics,
                                vmem_limit_bytes=min(need, V7X_VMEM_USABLE_BYTES))


def _load_bf16(ref):
    v = ref[...]
    return v if v.dtype == BF16 else v.astype(BF16)


def _sigmoid(x):
    return 0.5 * jnp.tanh(0.5 * x) + 0.5


def _rms(xf, g):
    return xf * lax.rsqrt(jnp.mean(xf * xf, axis=-1, keepdims=True) + EPS) * g


def _rmsnorm_cast_kernel(x_ref, g_ref, wq_ref, o_ref, wq_o_ref):
    o_ref[...] = _rms(x_ref[...], g_ref[...]).astype(o_ref.dtype)
    rows = wq_ref.shape[1]
    wq_o_ref[0:rows, :] = wq_ref[0].astype(wq_o_ref.dtype)
    wq_o_ref[rows:, :] = jnp.zeros((wq_o_ref.shape[0] - rows, wq_o_ref.shape[1]), wq_o_ref.dtype)


def _rmsnorm_cast(x, g, wq_t, *, pad_rows):
    s, d = x.shape
    n_heads, rows, r = wq_t.shape
    tm = s // n_heads
    assert tm * n_heads == s and rows % 16 == 0 and pad_rows % 16 == 0
    return pl.pallas_call(
        _rmsnorm_cast_kernel,
        grid=(n_heads,),
        in_specs=[pl.BlockSpec((tm, d), lambda i: (i, 0)), pl.BlockSpec((1, d), lambda i: (0, 0)),
                  pl.BlockSpec((1, rows, r), lambda i: (i, 0, 0))],
        out_specs=[pl.BlockSpec((tm, d), lambda i: (i, 0)), pl.BlockSpec((pad_rows, r), lambda i: (i, 0))],
        out_shape=[jax.ShapeDtypeStruct((s, d), BF16), jax.ShapeDtypeStruct((n_heads * pad_rows, r), BF16)],
        compiler_params=_params(("parallel",), [((tm, d), F32), ((tm, d), BF16), ((rows, r), F32),
                                                ((pad_rows, r), BF16)], temps=[((tm, d), F32)]),
        name="rmsnorm_cast",
    )(x, g, wq_t)


def _slab_spec(w, steps, nj):
    rows = w.shape[0]
    n_slabs = max(c for c in range(1, steps + 1) if rows % c == 0 and (rows // c) % 16 == 0)
    slab = rows // n_slabs
    return (slab, w.shape[1]), pl.BlockSpec((slab, w.shape[1]), lambda i, j: (jnp.minimum(i * nj + j, n_slabs - 1), 0))


def _matmul_kernel(a_ref, b_ref, *rest, b_is_nk):
    n_side = (len(rest) - 1) // 2
    o_ref = rest[n_side]
    contract_b = 1 if b_is_nk else 0
    o_ref[...] = lax.dot_general(a_ref[...], _load_bf16(b_ref), (((1,), (contract_b,)), ((), ())),
                                 preferred_element_type=F32).astype(o_ref.dtype)
    for src, dst in zip(rest[:n_side], rest[n_side + 1:]):
        dst[...] = src[...].astype(dst.dtype)


def _matmul(a, b, *, tm, tn, out_dtype, name, b_is_nk=False, n=None, side_cast=()):
    m, k = a.shape
    if n is None:
        n = b.shape[0] if b_is_nk else b.shape[1]
    if b_is_nk:
        b_block, b_spec = (tn, k), pl.BlockSpec((tn, k), lambda i, j: (j, 0))
    else:
        b_block, b_spec = (k, tn), pl.BlockSpec((k, tn), lambda i, j: (0, j))
    grid = (m // tm, n // tn)
    in_specs = [pl.BlockSpec((tm, k), lambda i, j: (i, 0)), b_spec]
    out_specs = pl.BlockSpec((tm, tn), lambda i, j: (i, j))
    out_shape = jax.ShapeDtypeStruct((m, n), out_dtype)
    blocks = [((tm, k), a.dtype), (b_block, b.dtype), ((tm, tn), out_dtype)]
    args = (a, b) + tuple(side_cast)
    if side_cast:
        slabs = [_slab_spec(w, grid[0] * grid[1], grid[1]) for w in side_cast]
        in_specs += [spec for _, spec in slabs]
        out_specs = [out_specs] + [spec for _, spec in slabs]
        out_shape = [out_shape] + [jax.ShapeDtypeStruct(w.shape, BF16) for w in side_cast]
        blocks += [(block, dt) for block, _ in slabs for dt in (F32, BF16)]
    return pl.pallas_call(
        functools.partial(_matmul_kernel, b_is_nk=b_is_nk),
        grid=grid, in_specs=in_specs, out_specs=out_specs, out_shape=out_shape,
        compiler_params=_params(("parallel", "parallel"), blocks, temps=[((tm, tn), F32), ((k, tn), BF16)]),
        name=name,
    )(*args)


def _pool_mixer_kernel(halo_ref, u_ref, w_ref, scale_ref, o_ref, ext_ref, lvl_a_ref, lvl_b_ref, *, tm, gw):
    i = pl.program_id(0)
    groups = len(POOL_WINDOWS)
    base = POOL_PAD + POOL_HALO
    n = POOL_HALO + tm
    width = ext_ref.shape[1]
    for ref in (ext_ref, lvl_a_ref, lvl_b_ref):
        ref[0:POOL_PAD, :] = jnp.zeros((POOL_PAD, width), F32)
    ext_ref[POOL_PAD:base, :] = jnp.where(i > 0, halo_ref[...], 0.0)
    ext_ref[base:base + tm, :] = u_ref[...]
    row = lax.broadcasted_iota(jnp.int32, (tm, 1), 0) + i * tm
    src = ext_ref
    for k, w in enumerate(POOL_WINDOWS):
        span = w // 2
        cols = slice(k * gw, (k + 1) * gw)
        u = u_ref[:, cols]
        win_sum = src[base:base + tm, cols] + src[base - span:base - span + tm, cols]
        inv_cnt = 1.0 / jnp.minimum(row + 1, w).astype(F32)
        pooled = win_sum * inv_cnt - u
        mixed = jnp.dot(pooled.astype(BF16), w_ref[k], preferred_element_type=F32)
        o_ref[:, cols] = (mixed * scale_ref[:, cols]).astype(o_ref.dtype)
        if k + 1 < groups:
            rest = slice((k + 1) * gw, width)
            dst = lvl_a_ref if k % 2 == 0 else lvl_b_ref
            dst[POOL_PAD:POOL_PAD + n, rest] = (src[POOL_PAD:POOL_PAD + n, rest]
                                                + src[POOL_PAD - span:POOL_PAD - span + n, rest])
            src = dst


def _pool_mixer(z, w_pool, pool_scale, *, pool_width, tm=512):
    s = z.shape[0]
    groups, gw, _ = w_pool.shape
    assert all(w == 2 ** (k + 1) for k, w in enumerate(POOL_WINDOWS)) and POOL_WINDOWS[-1] // 2 <= POOL_PAD
    halo_blocks = tm // POOL_HALO
    buf = (POOL_PAD + POOL_HALO + tm, pool_width)
    return pl.pallas_call(
        functools.partial(_pool_mixer_kernel, tm=tm, gw=gw),
        grid=(s // tm,),
        in_specs=[
            pl.BlockSpec((POOL_HALO, pool_width), lambda i: (jnp.maximum(i * halo_blocks - 1, 0), 0)),
            pl.BlockSpec((tm, pool_width), lambda i: (i, 0)),
            pl.BlockSpec((groups, gw, gw), lambda i: (0, 0, 0)),
            pl.BlockSpec((1, pool_width), lambda i: (0, 0)),
        ],
        out_specs=pl.BlockSpec((tm, pool_width), lambda i: (i, 0)),
        out_shape=jax.ShapeDtypeStruct((s, pool_width), BF16),
        scratch_shapes=[pltpu.VMEM(buf, F32)] * 3,
        compiler_params=_params(("parallel",),
                                [((tm, pool_width), F32), ((groups, gw, gw), BF16), ((tm, pool_width), BF16)],
                                scratch=[(buf, F32)] * 3,
                                temps=[((tm, pool_width), F32)]),
        name="pool_mixer",
    )(z, z, w_pool, pool_scale)


def _q_proj_kernel(ql_ref, g_ref, w_ref, pos_ref, freq_ref, o_ref, qn_ref, cos_ref, sin_ref, *, scale):
    @pl.when(pl.program_id(1) == 0)
    def _():
        qn_ref[...] = (_rms(ql_ref[...], g_ref[...]) * scale).astype(qn_ref.dtype)
        ang = freq_ref[...] * pos_ref[...]
        cos_ref[...] = jnp.cos(ang)
        sin_ref[...] = jnp.sin(ang)

    qt = lax.dot_general(w_ref[...], qn_ref[...], (((1,), (1,)), ((), ())), preferred_element_type=F32)
    half = QK_ROPE_DIM // 2
    r0, r1, r2 = QK_NOPE_DIM, QK_NOPE_DIM + half, QK_NOPE_DIM + QK_ROPE_DIM
    c, sn = cos_ref[...], sin_ref[...]
    rows = o_ref.shape[2]
    for hh in range(o_ref.shape[0]):
        q = qt[hh * rows:(hh + 1) * rows]
        x1, x2 = q[r0:r1], q[r1:r2]
        o_ref[hh, 0, 0:r0, :] = q[0:r0].astype(o_ref.dtype)
        o_ref[hh, 0, r0:r1, :] = (x1 * c - x2 * sn).astype(o_ref.dtype)
        o_ref[hh, 0, r1:r2, :] = (x1 * sn + x2 * c).astype(o_ref.dtype)
        o_ref[hh, 0, r2:, :] = q[r2:].astype(o_ref.dtype)


def _q_proj(z, q_norm, wq_t, pos_row, freq_col, *, col_block, rank, scale, tm, heads_per_step=8):
    s = z.shape[0]
    rows = V7X_MXU_DIM
    n_heads = wq_t.shape[0] // rows
    hps = heads_per_step
    half = QK_ROPE_DIM // 2
    return pl.pallas_call(
        functools.partial(_q_proj_kernel, scale=scale),
        grid=(s // tm, n_heads // hps),
        in_specs=[
            pl.BlockSpec((tm, rank), lambda i, h: (i, col_block)),
            pl.BlockSpec((1, rank), lambda i, h: (0, 0)),
            pl.BlockSpec((hps * rows, rank), lambda i, h: (h, 0)),
            pl.BlockSpec((1, tm), lambda i, h: (0, i)),
            pl.BlockSpec((half, 1), lambda i, h: (0, 0)),
        ],
        out_specs=pl.BlockSpec((hps, 1, rows, tm), lambda i, h: (h, i, 0, 0)),
        out_shape=jax.ShapeDtypeStruct((n_heads, s // tm, rows, tm), BF16),
        scratch_shapes=[pltpu.VMEM((tm, rank), BF16), pltpu.VMEM((half, tm), F32), pltpu.VMEM((half, tm), F32)],
        compiler_params=_params(("parallel", "arbitrary"),
                                [((tm, rank), F32), ((hps * rows, rank), BF16), ((hps * rows, tm), BF16)],
                                scratch=[((tm, rank), BF16), ((2 * half, tm), F32)],
                                temps=[((tm, rank), F32), ((hps * rows, tm), F32)]),
        name="q_proj",
    )(z, q_norm, wq_t, pos_row, freq_col)


def _kv_proj_kernel(kvl_ref, h_ref, wkr_ref, g_ref, wk_ref, wvt_ref, pos_ref, freq_ref, k_ref, vt_ref, kro_ref):
    kvn = _rms(kvl_ref[...], g_ref[...]).astype(BF16)
    k_ref[...] = jnp.dot(kvn, wk_ref[...], preferred_element_type=F32).astype(k_ref.dtype)
    vt = lax.dot_general(wvt_ref[...], kvn, (((1,), (1,)), ((), ())), preferred_element_type=F32)
    vt_ref[:, 0] = vt.reshape(vt_ref.shape[0], V_HEAD_DIM, vt.shape[1]).astype(vt_ref.dtype)
    half = QK_ROPE_DIM // 2
    wkr = _load_bf16(wkr_ref)
    wkr = jnp.concatenate([wkr, jnp.zeros((V7X_LANES - wkr.shape[0], wkr.shape[1]), BF16)], axis=0)
    x = lax.dot_general(h_ref[...], wkr, (((1,), (1,)), ((), ())), preferred_element_type=F32)
    ang = pos_ref[...] * freq_ref[...]
    lane = lax.broadcasted_iota(jnp.int32, x.shape, 1)
    x2_at_lo = pltpu.roll(x, V7X_LANES - half, axis=1)
    x1_at_hi = pltpu.roll(x, half, axis=1)
    partner = jnp.where(lane < half, -x2_at_lo, jnp.where(lane < QK_ROPE_DIM, x1_at_hi, 0.0))
    kro_ref[...] = (x * jnp.cos(ang) + partner * jnp.sin(ang)).astype(kro_ref.dtype)


def _kv_proj(z, h, w_in_t, kv_norm, wk, wv_t, pos_col, freq_row, *, lat_block, rope_row_block, rank, tm):
    s, d = h.shape
    nk = wk.shape[1]
    nv = wv_t.shape[0]
    n_heads = nv // V_HEAD_DIM
    return pl.pallas_call(
        _kv_proj_kernel,
        grid=(s // tm,),
        in_specs=[
            pl.BlockSpec((tm, rank), lambda i: (i, lat_block)),
            pl.BlockSpec((tm, d), lambda i: (i, 0)),
            pl.BlockSpec((QK_ROPE_DIM, d), lambda i: (rope_row_block, 0)),
            pl.BlockSpec((1, rank), lambda i: (0, 0)),
            pl.BlockSpec((rank, nk), lambda i: (0, 0)),
            pl.BlockSpec((nv, rank), lambda i: (0, 0)),
            pl.BlockSpec((tm, 1), lambda i: (i, 0)),
            pl.BlockSpec((1, V7X_LANES), lambda i: (0, 0)),
        ],
        out_specs=[
            pl.BlockSpec((tm, nk), lambda i: (i, 0)),
            pl.BlockSpec((n_heads, 1, V_HEAD_DIM, tm), lambda i: (0, i, 0, 0)),
            pl.BlockSpec((tm, V7X_LANES), lambda i: (i, 0)),
        ],
        out_shape=[
            jax.ShapeDtypeStruct((s, nk), BF16),
            jax.ShapeDtypeStruct((n_heads, s // tm, V_HEAD_DIM, tm), BF16),
            jax.ShapeDtypeStruct((s, V7X_LANES), BF16),
        ],
        compiler_params=_params(("parallel",),
                                [((tm, rank), F32), ((tm, d), BF16), ((QK_ROPE_DIM, d), w_in_t.dtype), ((rank, nk), BF16),
                                 ((nv, rank), BF16), ((tm, nk), BF16), ((nv, tm), BF16), ((tm, V7X_LANES), BF16),
                                 ((tm, V7X_LANES), F32)],
                                temps=[((tm, nk), F32), ((nv, tm), F32), ((V7X_LANES, d), BF16)]),
        name="kv_proj",
    )(z, h, w_in_t, kv_norm, wk, wv_t, pos_col, freq_row)


def _flash_kernel(kn_ref, kr_ref, q_ref, qnext_ref, vt_ref, o_ref, s_ref, smax_ref, bias_ref, m_ref, l_ref, acc_ref,
                  *, t, unroll):
    heads = q_ref.shape[0]
    qi = pl.program_id(1)
    first = 2

    def scores(kb, slot, q=q_ref):
        ks = pl.ds(pl.multiple_of(kb * t, t), t)
        for hh in range(heads):
            kn = kn_ref[ks, hh * QK_NOPE_DIM:(hh + 1) * QK_NOPE_DIM]
            kcat = jnp.concatenate([kn, kr_ref[ks, :]], axis=1)
            s = jnp.dot(kcat, q[hh, 0], preferred_element_type=F32)
            s_ref[hh, slot] = s
            smax_ref[hh, slot] = jnp.max(s, axis=0, keepdims=True)

    def softmax_pv(kb, slot, masked):
        for hh in range(heads):
            s = s_ref[hh, slot]
            if masked:
                s = s + bias_ref[...]
                block_max = jnp.max(s, axis=0, keepdims=True)
            else:
                block_max = smax_ref[hh, slot]
            m_prev = m_ref[hh]
            m_new = jnp.maximum(m_prev, block_max)
            alpha = jnp.exp2(m_prev - m_new)
            p = jnp.exp2(s - m_new)
            l_ref[hh] = alpha * l_ref[hh] + jnp.sum(p, axis=0, keepdims=True)
            pv = jnp.dot(vt_ref[hh, kb], p.astype(BF16), preferred_element_type=F32)
            acc_ref[hh] = alpha * acc_ref[hh] + pv
            m_ref[hh] = m_new

    m_ref[...] = jnp.full(m_ref.shape, -jnp.inf, F32)
    l_ref[...] = jnp.zeros(l_ref.shape, F32)
    acc_ref[...] = jnp.zeros(acc_ref.shape, F32)

    @pl.when(qi == 0)
    def _():
        kc = lax.broadcasted_iota(jnp.int32, (t, t), 0) // CHUNK
        qc = lax.broadcasted_iota(jnp.int32, (t, t), 1) // CHUNK
        bias_ref[...] = jnp.where(kc <= qc, 0.0, -jnp.inf).astype(F32)
        scores(0, first)
        softmax_pv(0, first, masked=True)
        scores(0, first, qnext_ref)

    @pl.when(qi > 0)
    def _():
        scores(1, 1)
        softmax_pv(0, first, masked=False)
        n_groups = (qi - 1) // unroll

        def group(g, c):
            b0 = 1 + g * unroll
            for u in range(unroll):
                scores(b0 + u + 1, u % 2)
                softmax_pv(b0 + u, (u + 1) % 2, masked=False)
            return c

        lax.fori_loop(0, n_groups, group, 0)
        base = 1 + n_groups * unroll
        rest = qi - base
        for r in range(unroll):
            @pl.when(rest == r)
            def _(r=r):
                for u in range(r):
                    scores(base + u + 1, u % 2)
                    softmax_pv(base + u, (u + 1) % 2, masked=False)
                scores(0, first, qnext_ref)
                softmax_pv(qi, (1 + r) % 2, masked=True)

    for hh in range(heads):
        out = (acc_ref[hh] * (1.0 / l_ref[hh])).T
        o_ref[:, hh * V_HEAD_DIM:(hh + 1) * V_HEAD_DIM] = out.astype(o_ref.dtype)


def _flash_attention(k_nope, k_rope, q_t, v_t, *, n_heads, t, unroll=2, heads_per_step=4):
    s = k_nope.shape[0]
    nq = s // t
    hps = heads_per_step
    assert t % CHUNK == 0 and unroll % 2 == 0 and nq >= 2 and n_heads % hps == 0
    assert v_t.shape == (n_heads, nq, V_HEAD_DIM, t) and q_t.shape == (n_heads, nq, V7X_MXU_DIM, t)
    scratch = [((hps, 3, t, t), F32), ((hps, 3, 1, t), F32), ((t, t), F32), ((hps, 1, t), F32), ((hps, 1, t), F32),
               ((hps, V_HEAD_DIM, t), F32)]
    return pl.pallas_call(
        functools.partial(_flash_kernel, t=t, unroll=unroll),
        grid=(n_heads // hps, nq),
        in_specs=[
            pl.BlockSpec((s, hps * QK_NOPE_DIM), lambda h, i: (0, h)),
            pl.BlockSpec((s, V7X_LANES), lambda h, i: (0, 0)),
            pl.BlockSpec((hps, 1, V7X_MXU_DIM, t), lambda h, i: (h, i, 0, 0)),
            pl.BlockSpec((hps, 1, V7X_MXU_DIM, t), lambda h, i: (h, jnp.minimum(i + 1, nq - 1), 0, 0)),
            pl.BlockSpec((hps, nq, V_HEAD_DIM, t), lambda h, i: (h, 0, 0, 0)),
        ],
        out_specs=pl.BlockSpec((t, hps * V_HEAD_DIM), lambda h, i: (i, h)),
        out_shape=jax.ShapeDtypeStruct((s, n_heads * V_HEAD_DIM), BF16),
        scratch_shapes=[pltpu.VMEM(shape, dtype) for shape, dtype in scratch],
        compiler_params=_params(("arbitrary", "arbitrary"),
                                [((s, hps * QK_NOPE_DIM), BF16), ((s, V7X_LANES), BF16),
                                 ((2 * hps * V7X_MXU_DIM, t), BF16), ((hps * V_HEAD_DIM, s), BF16),
                                 ((t, hps * V_HEAD_DIM), BF16)],
                                scratch=scratch,
                                temps=[((t, t), F32), ((t, t), BF16), ((t, V7X_MXU_DIM), BF16)]),
        name="flash_attention",
    )(k_nope, k_rope, q_t, q_t, v_t)


def _gated_merge_kernel(pm_ref, at_ref, h_ref, wup_ref, wum_ref, wga_ref, wgb_ref, wo_ref, o_ref, wo_o_ref):
    h = h_ref[...]
    ya = jnp.dot(pm_ref[...], wup_ref[...], preferred_element_type=F32)
    yb = jnp.dot(at_ref[...], wum_ref[...], preferred_element_type=F32)
    ga = _sigmoid(jnp.dot(h, wga_ref[...], preferred_element_type=F32))
    gb = _sigmoid(jnp.dot(h, wgb_ref[...], preferred_element_type=F32))
    o_ref[...] = (ga * ya + gb * yb).astype(o_ref.dtype)
    wo_o_ref[...] = wo_ref[...].astype(wo_o_ref.dtype)


def _gated_merge(pm, attn, h, w_up_pool, w_up_mla, w_gate2d, w_out, *, tm=1024, tn=256):
    s, d = h.shape
    kp, km = pm.shape[1], attn.shape[1]
    nb = d // tn
    slab_block, slab_spec = _slab_spec(w_out, (s // tm) * nb, nb)
    return pl.pallas_call(
        _gated_merge_kernel,
        grid=(s // tm, nb),
        in_specs=[
            pl.BlockSpec((tm, kp), lambda i, j: (i, 0)),
            pl.BlockSpec((tm, km), lambda i, j: (i, 0)),
            pl.BlockSpec((tm, d), lambda i, j: (i, 0)),
            pl.BlockSpec((kp, tn), lambda i, j: (0, j)),
            pl.BlockSpec((km, tn), lambda i, j: (0, j)),
            pl.BlockSpec((d, tn), lambda i, j: (0, j)),
            pl.BlockSpec((d, tn), lambda i, j: (0, j + nb)),
            slab_spec,
        ],
        out_specs=[pl.BlockSpec((tm, tn), lambda i, j: (i, j)), slab_spec],
        out_shape=[jax.ShapeDtypeStruct((s, d), BF16), jax.ShapeDtypeStruct(w_out.shape, BF16)],
        compiler_params=_params(("parallel", "parallel"),
                                [((tm, kp), BF16), ((tm, km), BF16), ((tm, d), BF16), ((kp, tn), BF16),
                                 ((km, tn), BF16), ((d, tn), BF16), ((d, tn), BF16), ((tm, tn), BF16),
                                 (slab_block, F32), (slab_block, BF16)],
                                temps=[((tm, tn), F32)] * 4),
        name="gated_merge",
    )(pm, attn, h, w_up_pool, w_up_mla, w_gate2d, w_gate2d, w_out)


def _residual_norm_kernel(t_ref, x_ref, gpost_ref, gnext_ref, xo_ref, ho_ref):
    xo = x_ref[...] + _rms(t_ref[...].astype(F32), gpost_ref[...])
    xo_ref[...] = xo
    ho_ref[...] = _rms(xo, gnext_ref[...]).astype(ho_ref.dtype)


def _residual_final_kernel(t_ref, x_ref, gpost_ref, xo_ref):
    xo_ref[...] = x_ref[...] + _rms(t_ref[...].astype(F32), gpost_ref[...])


def _residual_norm(t, x, g_post, g_next=None, *, tm=256):
    s, d = x.shape
    row = pl.BlockSpec((tm, d), lambda i: (i, 0))
    vec = pl.BlockSpec((1, d), lambda i: (0, 0))
    blocks = [((tm, d), t.dtype), ((tm, d), F32), ((tm, d), F32)]
    if g_next is None:
        return pl.pallas_call(
            _residual_final_kernel, grid=(s // tm,), in_specs=[row, row, vec], out_specs=row,
            out_shape=jax.ShapeDtypeStruct((s, d), F32),
            compiler_params=_params(("parallel",), blocks, temps=[((tm, d), F32)] * 2),
            name="residual_final",
        )(t, x, g_post)
    return pl.pallas_call(
        _residual_norm_kernel, grid=(s // tm,), in_specs=[row, row, vec, vec], out_specs=[row, row],
        out_shape=[jax.ShapeDtypeStruct((s, d), F32), jax.ShapeDtypeStruct((s, d), BF16)],
        compiler_params=_params(("parallel",), blocks + [((tm, d), BF16)], temps=[((tm, d), F32)] * 2),
        name="residual_norm",
    )(t, x, g_post, g_next)


def _swiglu_up_kernel(h_ref, wg_ref, wu_ref, wd_ref, o_ref, wd_o_ref):
    h = h_ref[...]
    gate = jnp.dot(h, _load_bf16(wg_ref), preferred_element_type=F32)
    up = jnp.dot(h, _load_bf16(wu_ref), preferred_element_type=F32)
    o_ref[...] = (gate * _sigmoid(gate) * up).astype(o_ref.dtype)
    wd_o_ref[...] = wd_ref[...].astype(wd_o_ref.dtype)


def _swiglu_up(h, w_gate, w_up, w_down, *, tm=1024, tn=256):
    s, d = h.shape
    f = w_gate.shape[1]
    nj = f // tn
    steps = (s // tm) * nj
    slab = w_down.shape[0] // steps
    assert slab * steps == w_down.shape[0] and slab % 16 == 0
    dn = w_down.shape[1]
    return pl.pallas_call(
        _swiglu_up_kernel,
        grid=(s // tm, nj),
        in_specs=[
            pl.BlockSpec((tm, d), lambda i, j: (i, 0)),
            pl.BlockSpec((d, tn), lambda i, j: (0, j)),
            pl.BlockSpec((d, tn), lambda i, j: (0, j)),
            pl.BlockSpec((slab, dn), lambda i, j: (i * nj + j, 0)),
        ],
        out_specs=[pl.BlockSpec((tm, tn), lambda i, j: (i, j)),
                   pl.BlockSpec((slab, dn), lambda i, j: (i * nj + j, 0))],
        out_shape=[jax.ShapeDtypeStruct((s, f), BF16), jax.ShapeDtypeStruct(w_down.shape, BF16)],
        compiler_params=_params(("parallel", "parallel"),
                                [((tm, d), BF16), ((d, tn), w_gate.dtype), ((d, tn), w_up.dtype), ((tm, tn), BF16),
                                 ((slab, dn), w_down.dtype), ((slab, dn), BF16)],
                                temps=[((tm, tn), F32)] * 3 + [((d, tn), BF16)] * 2),
        name="swiglu_up",
    )(h, w_gate, w_up, w_down)


def _ple_gate_kernel(h_ref, p_ref, wg_ref, wp_ref, o_ref):
    gate = _sigmoid(jnp.dot(h_ref[...], _load_bf16(wg_ref), preferred_element_type=F32))
    pe = jnp.dot(_load_bf16(p_ref), _load_bf16(wp_ref), preferred_element_type=F32)
    o_ref[...] = (pe * gate).astype(o_ref.dtype)


def _ple_gate(h, p, w_gate, w_proj, *, tm=1024, tn=512):
    s, d = h.shape
    r = p.shape[1]
    return pl.pallas_call(
        _ple_gate_kernel,
        grid=(s // tm, d // tn),
        in_specs=[
            pl.BlockSpec((tm, d), lambda i, j: (i, 0)),
            pl.BlockSpec((tm, r), lambda i, j: (i, 0)),
            pl.BlockSpec((d, tn), lambda i, j: (0, j)),
            pl.BlockSpec((r, tn), lambda i, j: (0, j)),
        ],
        out_specs=pl.BlockSpec((tm, tn), lambda i, j: (i, j)),
        out_shape=jax.ShapeDtypeStruct((s, d), BF16),
        compiler_params=_params(("parallel", "parallel"),
                                [((tm, d), BF16), ((tm, r), p.dtype), ((d, tn), w_gate.dtype), ((r, tn), w_proj.dtype),
                                 ((tm, tn), BF16)],
                                temps=[((tm, tn), F32)] * 2 + [((d, tn), BF16)]),
        name="ple_gate",
    )(h, p, w_gate, w_proj)


def kernel(x, p, positions, norm_mix_pre, norm_mix_post, w_in, q_norm, kv_norm, w_q_b, w_kv_b, w_pool, pool_scale,
           w_up_pool, w_up_mla, w_branch_gate, w_out, norm_ffn_pre, norm_ffn_post, w_ffn_gate, w_ffn_up,
           w_ffn_down, norm_ple_pre, w_ple_gate, w_ple_proj, norm_ple_post):
    batch, seq, d_model = x.shape
    depth = w_in.shape[0]
    in_width = w_in.shape[2]
    q_rank, n_heads, qk_dim = w_q_b.shape[1:]
    kv_rank = w_kv_b.shape[1]
    pool_width = w_up_pool.shape[1]
    assert batch == 1 and qk_dim == QK_NOPE_DIM + QK_ROPE_DIM
    assert w_kv_b.shape[3] == QK_NOPE_DIM + V_HEAD_DIM
    o_q, o_kv, o_kr = pool_width, pool_width + q_rank, pool_width + q_rank + kv_rank
    assert o_kr + QK_ROPE_DIM == in_width
    assert o_q % q_rank == 0 and o_kv % kv_rank == 0 and o_kr % QK_ROPE_DIM == 0
    z_tn = 512
    assert o_kr % z_tn == 0
    attn_t = 512
    half = QK_ROPE_DIM // 2
    scale = float(qk_dim) ** -0.5 * LOG2_E

    inv_freq = ROPE_THETA ** (-jnp.arange(0, QK_ROPE_DIM, 2, dtype=F32) / QK_ROPE_DIM)
    freq_col = inv_freq.reshape(half, 1)
    freq_row = jnp.zeros((1, V7X_LANES), F32).at[0, :half].set(inv_freq).at[0, half:QK_ROPE_DIM].set(inv_freq)
    pos_f = positions.astype(F32)
    pos_row, pos_col = pos_f.reshape(1, seq), pos_f.reshape(seq, 1)

    xs = x.reshape(seq, d_model)
    for i in range(depth):
        w_in_t = jnp.transpose(w_in[i])
        wq_f32_t = jnp.transpose(w_q_b[i], (1, 2, 0))
        wkv = w_kv_b[i].astype(BF16)
        wk = wkv[:, :, :QK_NOPE_DIM].reshape(kv_rank, n_heads * QK_NOPE_DIM)
        wv_t = jnp.transpose(wkv[:, :, QK_NOPE_DIM:], (1, 2, 0)).reshape(n_heads * V_HEAD_DIM, kv_rank)

        def row(v):
            return v[i].reshape(1, -1)

        h, wq_t = _rmsnorm_cast(xs, row(norm_mix_pre), wq_f32_t, pad_rows=V7X_MXU_DIM)
        z, w_up_p, w_up_m = _matmul(h, w_in_t, tm=1024, tn=z_tn, out_dtype=F32, name="in_proj", b_is_nk=True, n=o_kr,
                                    side_cast=(w_up_pool[i], w_up_mla[i]))
        pm = _pool_mixer(z, w_pool[i].astype(BF16), row(pool_scale), pool_width=pool_width)
        q_t = _q_proj(z, row(q_norm), wq_t, pos_row, freq_col, col_block=o_q // q_rank, rank=q_rank, scale=scale,
                      tm=attn_t)
        k_nope, v_t, k_rope = _kv_proj(z, h, w_in_t, row(kv_norm), wk, wv_t, pos_col, freq_row,
                                       lat_block=o_kv // kv_rank, rope_row_block=o_kr // QK_ROPE_DIM, rank=kv_rank,
                                       tm=attn_t)
        attn = _flash_attention(k_nope, k_rope, q_t, v_t, n_heads=n_heads, t=attn_t)
        w_gate2d = w_branch_gate[i].astype(BF16).reshape(d_model, 2 * d_model)
        merged, w_out_b = _gated_merge(pm, attn, h, w_up_p, w_up_m, w_gate2d, w_out[i])
        mix = _matmul(merged, w_out_b, tm=1024, tn=1024, out_dtype=BF16, name="out_proj")
        xs, h2 = _residual_norm(mix, xs, row(norm_mix_post), row(norm_ffn_pre))
        act, w_down = _swiglu_up(h2, w_ffn_gate[i], w_ffn_up[i], w_ffn_down[i])
        ffn, w_pg = _matmul(act, w_down, tm=512, tn=512, out_dtype=BF16, name="ffn_down", side_cast=(w_ple_gate[i],))
        xs, h3 = _residual_norm(ffn, xs, row(norm_ffn_post), row(norm_ple_pre))
        t = _ple_gate(h3, p[i].reshape(seq, -1), w_pg, w_ple_proj[i], tn=1024)
        xs = _residual_norm(t, xs, row(norm_ple_post))
    return xs.reshape(batch, seq, d_model)
```

```python
import functools

import jax
import jax.numpy as jnp
from jax import lax
from jax.experimental import pallas as pl
from jax.experimental.pallas import tpu as pltpu

CHUNK = 64
EPS = 1e-6
POOL_WINDOWS = (2, 4, 8, 16)
QK_NOPE_DIM = 128
QK_ROPE_DIM = 64
V_HEAD_DIM = 128
ROPE_THETA = 10000.0
LOG2_E = 1.4426950408889634

V7X_LANES = 128
V7X_MXU_DIM = 256
V7X_VMEM_BYTES = 64 * 1024 * 1024
V7X_VMEM_USABLE_BYTES = 60000 * 1024
V7X_VMEM_DEFAULT_SCOPED_BYTES = 32 * 1024 * 1024

POOL_HALO = 16
POOL_PAD = 8
F32 = jnp.float32
BF16 = jnp.bfloat16


def _nbytes(shape, dtype):
    n = 1
    for s in shape:
        n *= s
    return n * jnp.dtype(dtype).itemsize


def _params(semantics, blocks, scratch=(), temps=()):
    need = 2 * sum(_nbytes(s, d) for s, d in blocks)
    need += sum(_nbytes(s, d) for s, d in scratch) + sum(_nbytes(s, d) for s, d in temps)
    need = max(int(need * 1.25) + (2 << 20), V7X_VMEM_DEFAULT_SCOPED_BYTES)
    return pltpu.CompilerParams(dimension_semantics=semantics,
                                vmem_limit_bytes=min(need, V7X_VMEM_USABLE_BYTES))


def _load_bf16(ref):
    v = ref[...]
    return v if v.dtype == BF16 else v.astype(BF16)


def _sigmoid(x):
    return 0.5 * jnp.tanh(0.5 * x) + 0.5


def _rms(xf, g):
    return xf * lax.rsqrt(jnp.mean(xf * xf, axis=-1, keepdims=True) + EPS) * g


def _rmsnorm_cast_kernel(x_ref, g_ref, o_ref):
    o_ref[...] = _rms(x_ref[...], g_ref[...]).astype(o_ref.dtype)


def _rmsnorm_cast(x, g, *, tm=512):
    s, d = x.shape
    assert s % tm == 0
    return pl.pallas_call(
        _rmsnorm_cast_kernel,
        grid=(s // tm,),
        in_specs=[pl.BlockSpec((tm, d), lambda i: (i, 0)), pl.BlockSpec((1, d), lambda i: (0, 0))],
        out_specs=pl.BlockSpec((tm, d), lambda i: (i, 0)),
        out_shape=jax.ShapeDtypeStruct((s, d), BF16),
        compiler_params=_params(("parallel",), [((tm, d), F32), ((tm, d), BF16)], temps=[((tm, d), F32)]),
        name="rmsnorm_cast",
    )(x, g)


def _slab_spec(w, steps, nj):
    rows = w.shape[0]
    n_slabs = max(c for c in range(1, steps + 1) if rows % c == 0 and (rows // c) % 16 == 0)
    slab = rows // n_slabs
    return (slab, w.shape[1]), pl.BlockSpec((slab, w.shape[1]), lambda i, j: (jnp.minimum(i * nj + j, n_slabs - 1), 0))


def _matmul_kernel(a_ref, b_ref, *rest, b_is_nk):
    n_side = (len(rest) - 1) // 2
    o_ref = rest[n_side]
    contract_b = 1 if b_is_nk else 0
    o_ref[...] = lax.dot_general(a_ref[...], _load_bf16(b_ref), (((1,), (contract_b,)), ((), ())),
                                 preferred_element_type=F32).astype(o_ref.dtype)
    for src, dst in zip(rest[:n_side], rest[n_side + 1:]):
        dst[...] = src[...].astype(dst.dtype)


def _matmul(a, b, *, tm, tn, out_dtype, name, b_is_nk=False, n=None, side_cast=()):
    m, k = a.shape
    if n is None:
        n = b.shape[0] if b_is_nk else b.shape[1]
    if b_is_nk:
        b_block, b_spec = (tn, k), pl.BlockSpec((tn, k), lambda i, j: (j, 0))
    else:
        b_block, b_spec = (k, tn), pl.BlockSpec((k, tn), lambda i, j: (0, j))
    grid = (m // tm, n // tn)
    in_specs = [pl.BlockSpec((tm, k), lambda i, j: (i, 0)), b_spec]
    out_specs = pl.BlockSpec((tm, tn), lambda i, j: (i, j))
    out_shape = jax.ShapeDtypeStruct((m, n), out_dtype)
    blocks = [((tm, k), a.dtype), (b_block, b.dtype), ((tm, tn), out_dtype)]
    args = (a, b) + tuple(side_cast)
    if side_cast:
        slabs = [_slab_spec(w, grid[0] * grid[1], grid[1]) for w in side_cast]
        in_specs += [spec for _, spec in slabs]
        out_specs = [out_specs] + [spec for _, spec in slabs]
        out_shape = [out_shape] + [jax.ShapeDtypeStruct(w.shape, BF16) for w in side_cast]
        blocks += [(block, dt) for block, _ in slabs for dt in (F32, BF16)]
    return pl.pallas_call(
        functools.partial(_matmul_kernel, b_is_nk=b_is_nk),
        grid=grid, in_specs=in_specs, out_specs=out_specs, out_shape=out_shape,
        compiler_params=_params(("parallel", "parallel"), blocks, temps=[((tm, tn), F32), ((k, tn), BF16)]),
        name=name,
    )(*args)


def _pool_mixer_kernel(halo_ref, u_ref, w_ref, scale_ref, o_ref, ext_ref, lvl_a_ref, lvl_b_ref, *, tm, gw):
    i = pl.program_id(0)
    groups = len(POOL_WINDOWS)
    base = POOL_PAD + POOL_HALO
    n = POOL_HALO + tm
    width = ext_ref.shape[1]
    for ref in (ext_ref, lvl_a_ref, lvl_b_ref):
        ref[0:POOL_PAD, :] = jnp.zeros((POOL_PAD, width), F32)
    ext_ref[POOL_PAD:base, :] = jnp.where(i > 0, halo_ref[...], 0.0)
    ext_ref[base:base + tm, :] = u_ref[...]
    row = lax.broadcasted_iota(jnp.int32, (tm, 1), 0) + i * tm
    src = ext_ref
    for k, w in enumerate(POOL_WINDOWS):
        span = w // 2
        cols = slice(k * gw, (k + 1) * gw)
        u = u_ref[:, cols]
        win_sum = src[base:base + tm, cols] + src[base - span:base - span + tm, cols]
        inv_cnt = 1.0 / jnp.minimum(row + 1, w).astype(F32)
        pooled = win_sum * inv_cnt - u
        mixed = jnp.dot(pooled.astype(BF16), w_ref[k], preferred_element_type=F32)
        o_ref[:, cols] = (mixed * scale_ref[:, cols]).astype(o_ref.dtype)
        if k + 1 < groups:
            rest = slice((k + 1) * gw, width)
            dst = lvl_a_ref if k % 2 == 0 else lvl_b_ref
            dst[POOL_PAD:POOL_PAD + n, rest] = (src[POOL_PAD:POOL_PAD + n, rest]
                                                + src[POOL_PAD - span:POOL_PAD - span + n, rest])
            src = dst


def _pool_mixer(z, w_pool, pool_scale, *, pool_width, tm=512):
    s = z.shape[0]
    groups, gw, _ = w_pool.shape
    assert all(w == 2 ** (k + 1) for k, w in enumerate(POOL_WINDOWS)) and POOL_WINDOWS[-1] // 2 <= POOL_PAD
    halo_blocks = tm // POOL_HALO
    buf = (POOL_PAD + POOL_HALO + tm, pool_width)
    return pl.pallas_call(
        functools.partial(_pool_mixer_kernel, tm=tm, gw=gw),
        grid=(s // tm,),
        in_specs=[
            pl.BlockSpec((POOL_HALO, pool_width), lambda i: (jnp.maximum(i * halo_blocks - 1, 0), 0)),
            pl.BlockSpec((tm, pool_width), lambda i: (i, 0)),
            pl.BlockSpec((groups, gw, gw), lambda i: (0, 0, 0)),
            pl.BlockSpec((1, pool_width), lambda i: (0, 0)),
        ],
        out_specs=pl.BlockSpec((tm, pool_width), lambda i: (i, 0)),
        out_shape=jax.ShapeDtypeStruct((s, pool_width), BF16),
        scratch_shapes=[pltpu.VMEM(buf, F32)] * 3,
        compiler_params=_params(("parallel",),
                                [((tm, pool_width), F32), ((groups, gw, gw), BF16), ((tm, pool_width), BF16)],
                                scratch=[(buf, F32)] * 3,
                                temps=[((tm, pool_width), F32)]),
        name="pool_mixer",
    )(z, z, w_pool, pool_scale)


def _q_proj_kernel(ql_ref, g_ref, w_ref, pos_ref, freq_ref, o_ref, qn_ref, cos_ref, sin_ref, *, scale):
    @pl.when(pl.program_id(1) == 0)
    def _():
        qn_ref[...] = (_rms(ql_ref[...], g_ref[...]) * scale).astype(qn_ref.dtype)
        ang = freq_ref[...] * pos_ref[...]
        cos_ref[...] = jnp.cos(ang)
        sin_ref[...] = jnp.sin(ang)

    qt = lax.dot_general(w_ref[...], qn_ref[...], (((1,), (1,)), ((), ())), preferred_element_type=F32)
    half = QK_ROPE_DIM // 2
    r0, r1, r2 = QK_NOPE_DIM, QK_NOPE_DIM + half, QK_NOPE_DIM + QK_ROPE_DIM
    c, sn = cos_ref[...], sin_ref[...]
    heads, rows = o_ref.shape[0], o_ref.shape[2]
    for hh in range(heads):
        q = qt[hh * r2:(hh + 1) * r2]
        x1, x2 = q[r0:r1], q[r1:r2]
        o_ref[hh, 0, 0:r0, :] = q[0:r0].astype(o_ref.dtype)
        o_ref[hh, 0, r0:r1, :] = (x1 * c - x2 * sn).astype(o_ref.dtype)
        o_ref[hh, 0, r1:r2, :] = (x1 * sn + x2 * c).astype(o_ref.dtype)
        o_ref[hh, 0, r2:, :] = jnp.zeros((rows - r2, o_ref.shape[3]), o_ref.dtype)


def _q_proj(z, q_norm, wq_t, pos_row, freq_col, *, col_block, rank, scale, tm, heads_per_step=8):
    s = z.shape[0]
    rows = V7X_MXU_DIM
    w_rows = QK_NOPE_DIM + QK_ROPE_DIM
    n_heads = wq_t.shape[0] // w_rows
    hps = heads_per_step
    half = QK_ROPE_DIM // 2
    return pl.pallas_call(
        functools.partial(_q_proj_kernel, scale=scale),
        grid=(s // tm, n_heads // hps),
        in_specs=[
            pl.BlockSpec((tm, rank), lambda i, h: (i, col_block)),
            pl.BlockSpec((1, rank), lambda i, h: (0, 0)),
            pl.BlockSpec((hps * w_rows, rank), lambda i, h: (h, 0)),
            pl.BlockSpec((1, tm), lambda i, h: (0, i)),
            pl.BlockSpec((half, 1), lambda i, h: (0, 0)),
        ],
        out_specs=pl.BlockSpec((hps, 1, rows, tm), lambda i, h: (h, i, 0, 0)),
        out_shape=jax.ShapeDtypeStruct((n_heads, s // tm, rows, tm), BF16),
        scratch_shapes=[pltpu.VMEM((tm, rank), BF16), pltpu.VMEM((half, tm), F32), pltpu.VMEM((half, tm), F32)],
        compiler_params=_params(("parallel", "arbitrary"),
                                [((tm, rank), F32), ((hps * w_rows, rank), BF16), ((hps * rows, tm), BF16)],
                                scratch=[((tm, rank), BF16), ((2 * half, tm), F32)],
                                temps=[((tm, rank), F32), ((hps * rows, tm), F32)]),
        name="q_proj",
    )(z, q_norm, wq_t, pos_row, freq_col)


def _kv_proj_kernel(kvl_ref, h_ref, wkr_ref, g_ref, wk_ref, wvt_ref, pos_ref, freq_ref, k_ref, vt_ref, kro_ref):
    kvn = _rms(kvl_ref[...], g_ref[...]).astype(BF16)
    k_ref[...] = jnp.dot(kvn, wk_ref[...], preferred_element_type=F32).astype(k_ref.dtype)
    vt = lax.dot_general(wvt_ref[...], kvn, (((1,), (1,)), ((), ())), preferred_element_type=F32)
    vt_ref[:, 0] = vt.reshape(vt_ref.shape[0], V_HEAD_DIM, vt.shape[1]).astype(vt_ref.dtype)
    half = QK_ROPE_DIM // 2
    wkr = _load_bf16(wkr_ref)
    wkr = jnp.concatenate([wkr, jnp.zeros((V7X_LANES - wkr.shape[0], wkr.shape[1]), BF16)], axis=0)
    x = lax.dot_general(h_ref[...], wkr, (((1,), (1,)), ((), ())), preferred_element_type=F32)
    ang = pos_ref[...] * freq_ref[...]
    lane = lax.broadcasted_iota(jnp.int32, x.shape, 1)
    x2_at_lo = pltpu.roll(x, V7X_LANES - half, axis=1)
    x1_at_hi = pltpu.roll(x, half, axis=1)
    partner = jnp.where(lane < half, -x2_at_lo, jnp.where(lane < QK_ROPE_DIM, x1_at_hi, 0.0))
    kro_ref[...] = (x * jnp.cos(ang) + partner * jnp.sin(ang)).astype(kro_ref.dtype)


def _kv_proj(z, h, w_in_t, kv_norm, wk, wv_t, pos_col, freq_row, *, lat_block, rope_row_block, rank, tm):
    s, d = h.shape
    nk = wk.shape[1]
    nv = wv_t.shape[0]
    n_heads = nv // V_HEAD_DIM
    return pl.pallas_call(
        _kv_proj_kernel,
        grid=(s // tm,),
        in_specs=[
            pl.BlockSpec((tm, rank), lambda i: (i, lat_block)),
            pl.BlockSpec((tm, d), lambda i: (i, 0)),
            pl.BlockSpec((QK_ROPE_DIM, d), lambda i: (rope_row_block, 0)),
            pl.BlockSpec((1, rank), lambda i: (0, 0)),
            pl.BlockSpec((rank, nk), lambda i: (0, 0)),
            pl.BlockSpec((nv, rank), lambda i: (0, 0)),
            pl.BlockSpec((tm, 1), lambda i: (i, 0)),
            pl.BlockSpec((1, V7X_LANES), lambda i: (0, 0)),
        ],
        out_specs=[
            pl.BlockSpec((tm, nk), lambda i: (i, 0)),
            pl.BlockSpec((n_heads, 1, V_HEAD_DIM, tm), lambda i: (0, i, 0, 0)),
            pl.BlockSpec((tm, V7X_LANES), lambda i: (i, 0)),
        ],
        out_shape=[
            jax.ShapeDtypeStruct((s, nk), BF16),
            jax.ShapeDtypeStruct((n_heads, s // tm, V_HEAD_DIM, tm), BF16),
            jax.ShapeDtypeStruct((s, V7X_LANES), BF16),
        ],
        compiler_params=_params(("parallel",),
                                [((tm, rank), F32), ((tm, d), BF16), ((QK_ROPE_DIM, d), w_in_t.dtype), ((rank, nk), BF16),
                                 ((nv, rank), BF16), ((tm, nk), BF16), ((nv, tm), BF16), ((tm, V7X_LANES), BF16),
                                 ((tm, V7X_LANES), F32)],
                                temps=[((tm, nk), F32), ((nv, tm), F32), ((V7X_LANES, d), BF16)]),
        name="kv_proj",
    )(z, h, w_in_t, kv_norm, wk, wv_t, pos_col, freq_row)


def _flash_kernel(kn_ref, kr_ref, q_ref, qnext_ref, vt_ref, o_ref, s_ref, smax_ref, bias_ref, m_ref, l_ref, acc_ref,
                  *, t, unroll):
    heads = q_ref.shape[0]
    qi = pl.program_id(1)
    first = 2

    def scores(kb, slot, q=q_ref):
        ks = pl.ds(pl.multiple_of(kb * t, t), t)
        for hh in range(heads):
            kn = kn_ref[ks, hh * QK_NOPE_DIM:(hh + 1) * QK_NOPE_DIM]
            kcat = jnp.concatenate([kn, kr_ref[ks, :]], axis=1)
            s = jnp.dot(kcat, q[hh, 0], preferred_element_type=F32)
            s_ref[hh, slot] = s
            smax_ref[hh, slot] = jnp.max(s, axis=0, keepdims=True)

    def softmax_pv(kb, slot, masked):
        for hh in range(heads):
            s = s_ref[hh, slot]
            if masked:
                s = s + bias_ref[...]
                block_max = jnp.max(s, axis=0, keepdims=True)
            else:
                block_max = smax_ref[hh, slot]
            m_prev = m_ref[hh]
            m_new = jnp.maximum(m_prev, block_max)
            alpha = jnp.exp2(m_prev - m_new)
            p = jnp.exp2(s - m_new)
            l_ref[hh] = alpha * l_ref[hh] + jnp.sum(p, axis=0, keepdims=True)
            pv = jnp.dot(vt_ref[hh, kb], p.astype(BF16), preferred_element_type=F32)
            acc_ref[hh] = alpha * acc_ref[hh] + pv
            m_ref[hh] = m_new

    m_ref[...] = jnp.full(m_ref.shape, -jnp.inf, F32)
    l_ref[...] = jnp.zeros(l_ref.shape, F32)
    acc_ref[...] = jnp.zeros(acc_ref.shape, F32)

    @pl.when(qi == 0)
    def _():
        kc = lax.broadcasted_iota(jnp.int32, (t, t), 0) // CHUNK
        qc = lax.broadcasted_iota(jnp.int32, (t, t), 1) // CHUNK
        bias_ref[...] = jnp.where(kc <= qc, 0.0, -jnp.inf).astype(F32)
        scores(0, first)
        softmax_pv(0, first, masked=True)
        scores(0, first, qnext_ref)

    @pl.when(qi > 0)
    def _():
        scores(1, 1)
        softmax_pv(0, first, masked=False)
        n_groups = (qi - 1) // unroll

        def group(g, c):
            b0 = 1 + g * unroll
            for u in range(unroll):
                scores(b0 + u + 1, u % 2)
                softmax_pv(b0 + u, (u + 1) % 2, masked=False)
            return c

        lax.fori_loop(0, n_groups, group, 0)
        base = 1 + n_groups * unroll
        rest = qi - base
        for r in range(unroll):
            @pl.when(rest == r)
            def _(r=r):
                for u in range(r):
                    scores(base + u + 1, u % 2)
                    softmax_pv(base + u, (u + 1) % 2, masked=False)
                scores(0, first, qnext_ref)
                softmax_pv(qi, (1 + r) % 2, masked=True)

    for hh in range(heads):
        out = (acc_ref[hh] * (1.0 / l_ref[hh])).T
        o_ref[:, hh * V_HEAD_DIM:(hh + 1) * V_HEAD_DIM] = out.astype(o_ref.dtype)


def _flash_attention(k_nope, k_rope, q_t, v_t, *, n_heads, t, unroll=2, heads_per_step=4):
    s = k_nope.shape[0]
    nq = s // t
    hps = heads_per_step
    assert t % CHUNK == 0 and unroll % 2 == 0 and nq >= 2 and n_heads % hps == 0
    assert v_t.shape == (n_heads, nq, V_HEAD_DIM, t) and q_t.shape == (n_heads, nq, V7X_MXU_DIM, t)
    scratch = [((hps, 3, t, t), F32), ((hps, 3, 1, t), F32), ((t, t), F32), ((hps, 1, t), F32), ((hps, 1, t), F32),
               ((hps, V_HEAD_DIM, t), F32)]
    return pl.pallas_call(
        functools.partial(_flash_kernel, t=t, unroll=unroll),
        grid=(n_heads // hps, nq),
        in_specs=[
            pl.BlockSpec((s, hps * QK_NOPE_DIM), lambda h, i: (0, h)),
            pl.BlockSpec((s, V7X_LANES), lambda h, i: (0, 0)),
            pl.BlockSpec((hps, 1, V7X_MXU_DIM, t), lambda h, i: (h, i, 0, 0)),
            pl.BlockSpec((hps, 1, V7X_MXU_DIM, t), lambda h, i: (h, jnp.minimum(i + 1, nq - 1), 0, 0)),
            pl.BlockSpec((hps, nq, V_HEAD_DIM, t), lambda h, i: (h, 0, 0, 0)),
        ],
        out_specs=pl.BlockSpec((t, hps * V_HEAD_DIM), lambda h, i: (i, h)),
        out_shape=jax.ShapeDtypeStruct((s, n_heads * V_HEAD_DIM), BF16),
        scratch_shapes=[pltpu.VMEM(shape, dtype) for shape, dtype in scratch],
        compiler_params=_params(("arbitrary", "arbitrary"),
                                [((s, hps * QK_NOPE_DIM), BF16), ((s, V7X_LANES), BF16),
                                 ((2 * hps * V7X_MXU_DIM, t), BF16), ((hps * V_HEAD_DIM, s), BF16),
                                 ((t, hps * V_HEAD_DIM), BF16)],
                                scratch=scratch,
                                temps=[((t, t), F32), ((t, t), BF16), ((t, V7X_MXU_DIM), BF16)]),
        name="flash_attention",
    )(k_nope, k_rope, q_t, q_t, v_t)


def _gated_merge_kernel(pm_ref, at_ref, h_ref, wup_ref, wum_ref, wga_ref, wgb_ref, wo_ref, o_ref, wo_o_ref):
    h = h_ref[...]
    ya = jnp.dot(pm_ref[...], wup_ref[...], preferred_element_type=F32)
    yb = jnp.dot(at_ref[...], wum_ref[...], preferred_element_type=F32)
    ga = _sigmoid(jnp.dot(h, wga_ref[...], preferred_element_type=F32))
    gb = _sigmoid(jnp.dot(h, wgb_ref[...], preferred_element_type=F32))
    o_ref[...] = (ga * ya + gb * yb).astype(o_ref.dtype)
    wo_o_ref[...] = wo_ref[...].astype(wo_o_ref.dtype)


def _gated_merge(pm, attn, h, w_up_pool, w_up_mla, w_gate2d, w_out, *, tm=1024, tn=256):
    s, d = h.shape
    kp, km = pm.shape[1], attn.shape[1]
    nb = d // tn
    slab_block, slab_spec = _slab_spec(w_out, (s // tm) * nb, nb)
    return pl.pallas_call(
        _gated_merge_kernel,
        grid=(s // tm, nb),
        in_specs=[
            pl.BlockSpec((tm, kp), lambda i, j: (i, 0)),
            pl.BlockSpec((tm, km), lambda i, j: (i, 0)),
            pl.BlockSpec((tm, d), lambda i, j: (i, 0)),
            pl.BlockSpec((kp, tn), lambda i, j: (0, j)),
            pl.BlockSpec((km, tn), lambda i, j: (0, j)),
            pl.BlockSpec((d, tn), lambda i, j: (0, j)),
            pl.BlockSpec((d, tn), lambda i, j: (0, j + nb)),
            slab_spec,
        ],
        out_specs=[pl.BlockSpec((tm, tn), lambda i, j: (i, j)), slab_spec],
        out_shape=[jax.ShapeDtypeStruct((s, d), BF16), jax.ShapeDtypeStruct(w_out.shape, BF16)],
        compiler_params=_params(("parallel", "parallel"),
                                [((tm, kp), BF16), ((tm, km), BF16), ((tm, d), BF16), ((kp, tn), BF16),
                                 ((km, tn), BF16), ((d, tn), BF16), ((d, tn), BF16), ((tm, tn), BF16),
                                 (slab_block, F32), (slab_block, BF16)],
                                temps=[((tm, tn), F32)] * 4),
        name="gated_merge",
    )(pm, attn, h, w_up_pool, w_up_mla, w_gate2d, w_gate2d, w_out)


def _residual_norm_kernel(t_ref, x_ref, gpost_ref, gnext_ref, xo_ref, ho_ref):
    xo = x_ref[...] + _rms(t_ref[...].astype(F32), gpost_ref[...])
    xo_ref[...] = xo
    ho_ref[...] = _rms(xo, gnext_ref[...]).astype(ho_ref.dtype)


def _residual_final_kernel(t_ref, x_ref, gpost_ref, xo_ref):
    xo_ref[...] = x_ref[...] + _rms(t_ref[...].astype(F32), gpost_ref[...])


def _residual_norm(t, x, g_post, g_next=None, *, tm=256):
    s, d = x.shape
    row = pl.BlockSpec((tm, d), lambda i: (i, 0))
    vec = pl.BlockSpec((1, d), lambda i: (0, 0))
    blocks = [((tm, d), t.dtype), ((tm, d), F32), ((tm, d), F32)]
    if g_next is None:
        return pl.pallas_call(
            _residual_final_kernel, grid=(s // tm,), in_specs=[row, row, vec], out_specs=row,
            out_shape=jax.ShapeDtypeStruct((s, d), F32),
            compiler_params=_params(("parallel",), blocks, temps=[((tm, d), F32)] * 2),
            name="residual_final",
        )(t, x, g_post)
    return pl.pallas_call(
        _residual_norm_kernel, grid=(s // tm,), in_specs=[row, row, vec, vec], out_specs=[row, row],
        out_shape=[jax.ShapeDtypeStruct((s, d), F32), jax.ShapeDtypeStruct((s, d), BF16)],
        compiler_params=_params(("parallel",), blocks + [((tm, d), BF16)], temps=[((tm, d), F32)] * 2),
        name="residual_norm",
    )(t, x, g_post, g_next)


def _swiglu_up_kernel(h_ref, wg_ref, wu_ref, wd_ref, o_ref, wd_o_ref):
    h = h_ref[...]
    gate = jnp.dot(h, _load_bf16(wg_ref), preferred_element_type=F32)
    up = jnp.dot(h, _load_bf16(wu_ref), preferred_element_type=F32)
    o_ref[...] = (gate * _sigmoid(gate) * up).astype(o_ref.dtype)
    wd_o_ref[...] = wd_ref[...].astype(wd_o_ref.dtype)


def _swiglu_up(h, w_gate, w_up, w_down, *, tm=1024, tn=256):
    s, d = h.shape
    f = w_gate.shape[1]
    nj = f // tn
    steps = (s // tm) * nj
    slab = w_down.shape[0] // steps
    assert slab * steps == w_down.shape[0] and slab % 16 == 0
    dn = w_down.shape[1]
    return pl.pallas_call(
        _swiglu_up_kernel,
        grid=(s // tm, nj),
        in_specs=[
            pl.BlockSpec((tm, d), lambda i, j: (i, 0)),
            pl.BlockSpec((d, tn), lambda i, j: (0, j)),
            pl.BlockSpec((d, tn), lambda i, j: (0, j)),
            pl.BlockSpec((slab, dn), lambda i, j: (i * nj + j, 0)),
        ],
        out_specs=[pl.BlockSpec((tm, tn), lambda i, j: (i, j)),
                   pl.BlockSpec((slab, dn), lambda i, j: (i * nj + j, 0))],
        out_shape=[jax.ShapeDtypeStruct((s, f), BF16), jax.ShapeDtypeStruct(w_down.shape, BF16)],
        compiler_params=_params(("parallel", "parallel"),
                                [((tm, d), BF16), ((d, tn), w_gate.dtype), ((d, tn), w_up.dtype), ((tm, tn), BF16),
                                 ((slab, dn), w_down.dtype), ((slab, dn), BF16)],
                                temps=[((tm, tn), F32)] * 3 + [((d, tn), BF16)] * 2),
        name="swiglu_up",
    )(h, w_gate, w_up, w_down)


def _ple_gate_kernel(h_ref, p_ref, wg_ref, wp_ref, o_ref):
    gate = _sigmoid(jnp.dot(h_ref[...], _load_bf16(wg_ref), preferred_element_type=F32))
    pe = jnp.dot(_load_bf16(p_ref), _load_bf16(wp_ref), preferred_element_type=F32)
    o_ref[...] = (pe * gate).astype(o_ref.dtype)


def _ple_gate(h, p, w_gate, w_proj, *, tm=1024, tn=512):
    s, d = h.shape
    r = p.shape[1]
    return pl.pallas_call(
        _ple_gate_kernel,
        grid=(s // tm, d // tn),
        in_specs=[
            pl.BlockSpec((tm, d), lambda i, j: (i, 0)),
            pl.BlockSpec((tm, r), lambda i, j: (i, 0)),
            pl.BlockSpec((d, tn), lambda i, j: (0, j)),
            pl.BlockSpec((r, tn), lambda i, j: (0, j)),
        ],
        out_specs=pl.BlockSpec((tm, tn), lambda i, j: (i, j)),
        out_shape=jax.ShapeDtypeStruct((s, d), BF16),
        compiler_params=_params(("parallel", "parallel"),
                                [((tm, d), BF16), ((tm, r), p.dtype), ((d, tn), w_gate.dtype), ((r, tn), w_proj.dtype),
                                 ((tm, tn), BF16)],
                                temps=[((tm, tn), F32)] * 2 + [((d, tn), BF16)]),
        name="ple_gate",
    )(h, p, w_gate, w_proj)


def kernel(x, p, positions, norm_mix_pre, norm_mix_post, w_in, q_norm, kv_norm, w_q_b, w_kv_b, w_pool, pool_scale,
           w_up_pool, w_up_mla, w_branch_gate, w_out, norm_ffn_pre, norm_ffn_post, w_ffn_gate, w_ffn_up,
           w_ffn_down, norm_ple_pre, w_ple_gate, w_ple_proj, norm_ple_post):
    batch, seq, d_model = x.shape
    depth = w_in.shape[0]
    in_width = w_in.shape[2]
    q_rank, n_heads, qk_dim = w_q_b.shape[1:]
    kv_rank = w_kv_b.shape[1]
    pool_width = w_up_pool.shape[1]
    assert batch == 1 and qk_dim == QK_NOPE_DIM + QK_ROPE_DIM
    assert w_kv_b.shape[3] == QK_NOPE_DIM + V_HEAD_DIM
    o_q, o_kv, o_kr = pool_width, pool_width + q_rank, pool_width + q_rank + kv_rank
    assert o_kr + QK_ROPE_DIM == in_width
    assert o_q % q_rank == 0 and o_kv % kv_rank == 0 and o_kr % QK_ROPE_DIM == 0
    z_tn = 512
    assert o_kr % z_tn == 0
    attn_t = 512
    half = QK_ROPE_DIM // 2
    scale = float(qk_dim) ** -0.5 * LOG2_E

    inv_freq = ROPE_THETA ** (-jnp.arange(0, QK_ROPE_DIM, 2, dtype=F32) / QK_ROPE_DIM)
    freq_col = inv_freq.reshape(half, 1)
    freq_row = jnp.zeros((1, V7X_LANES), F32).at[0, :half].set(inv_freq).at[0, half:QK_ROPE_DIM].set(inv_freq)
    pos_f = positions.astype(F32)
    pos_row, pos_col = pos_f.reshape(1, seq), pos_f.reshape(seq, 1)

    xs = x.reshape(seq, d_model)
    for i in range(depth):
        w_in_t = jnp.transpose(w_in[i])
        wq_f32_t = jnp.transpose(w_q_b[i], (1, 2, 0)).reshape(-1, q_rank)
        wkv = w_kv_b[i].astype(BF16)
        wk = wkv[:, :, :QK_NOPE_DIM].reshape(kv_rank, n_heads * QK_NOPE_DIM)
        wv_t = jnp.transpose(wkv[:, :, QK_NOPE_DIM:], (1, 2, 0)).reshape(n_heads * V_HEAD_DIM, kv_rank)

        def row(v):
            return v[i].reshape(1, -1)

        h = _rmsnorm_cast(xs, row(norm_mix_pre))
        z, w_up_p, w_up_m, wq_t = _matmul(h, w_in_t, tm=1024, tn=z_tn, out_dtype=F32, name="in_proj", b_is_nk=True,
                                          n=o_kr, side_cast=(w_up_pool[i], w_up_mla[i], wq_f32_t))
        pm = _pool_mixer(z, w_pool[i].astype(BF16), row(pool_scale), pool_width=pool_width)
        q_t = _q_proj(z, row(q_norm), wq_t, pos_row, freq_col, col_block=o_q // q_rank, rank=q_rank, scale=scale,
                      tm=attn_t)
        k_nope, v_t, k_rope = _kv_proj(z, h, w_in_t, row(kv_norm), wk, wv_t, pos_col, freq_row,
                                       lat_block=o_kv // kv_rank, rope_row_block=o_kr // QK_ROPE_DIM, rank=kv_rank,
                                       tm=attn_t)
        attn = _flash_attention(k_nope, k_rope, q_t, v_t, n_heads=n_heads, t=attn_t)
        w_gate2d = w_branch_gate[i].astype(BF16).reshape(d_model, 2 * d_model)
        merged, w_out_b = _gated_merge(pm, attn, h, w_up_p, w_up_m, w_gate2d, w_out[i])
        mix = _matmul(merged, w_out_b, tm=1024, tn=1024, out_dtype=BF16, name="out_proj")
        xs, h2 = _residual_norm(mix, xs, row(norm_mix_post), row(norm_ffn_pre))
        act, w_down = _swiglu_up(h2, w_ffn_gate[i], w_ffn_up[i], w_ffn_down[i])
        ffn, w_pg = _matmul(act, w_down, tm=512, tn=512, out_dtype=BF16, name="ffn_down", side_cast=(w_ple_gate[i],))
        xs, h3 = _residual_norm(ffn, xs, row(norm_ffn_post), row(norm_ple_pre))
        t = _ple_gate(h3, p[i].reshape(seq, -1), w_pg, w_ple_proj[i], tn=1024)
        xs = _residual_norm(t, xs, row(norm_ple_post))
    return xs.reshape(batch, seq, d_model)
```

```python
import functools

import jax
import jax.numpy as jnp
from jax import lax
from jax.experimental import pallas as pl
from jax.experimental.pallas import tpu as pltpu

CHUNK = 64
EPS = 1e-6
POOL_WINDOWS = (2, 4, 8, 16)
QK_NOPE_DIM = 128
QK_ROPE_DIM = 64
V_HEAD_DIM = 128
ROPE_THETA = 10000.0
LOG2_E = 1.4426950408889634

V7X_LANES = 128
V7X_MXU_DIM = 256
V7X_VMEM_BYTES = 64 * 1024 * 1024
V7X_VMEM_USABLE_BYTES = 60000 * 1024
V7X_VMEM_DEFAULT_SCOPED_BYTES = 32 * 1024 * 1024

POOL_HALO = 16
POOL_PAD = 8
F32 = jnp.float32
BF16 = jnp.bfloat16


def _nbytes(shape, dtype):
    n = 1
    for s in shape:
        n *= s
    return n * jnp.dtype(dtype).itemsize


def _params(semantics, blocks, scratch=(), temps=()):
    need = 2 * sum(_nbytes(s, d) for s, d in blocks)
    need += sum(_nbytes(s, d) for s, d in scratch) + sum(_nbytes(s, d) for s, d in temps)
    need = max(int(need * 1.25) + (2 << 20), V7X_VMEM_DEFAULT_SCOPED_BYTES)
    return pltpu.CompilerParams(dimension_semantics=semantics,
                                vmem_limit_bytes=min(need, V7X_VMEM_USABLE_BYTES))


def _load_bf16(ref):
    v = ref[...]
    return v if v.dtype == BF16 else v.astype(BF16)


def _sigmoid(x):
    return 0.5 * jnp.tanh(0.5 * x) + 0.5


def _rms(xf, g):
    return xf * lax.rsqrt(jnp.mean(xf * xf, axis=-1, keepdims=True) + EPS) * g


def _rmsnorm_cast_kernel(x_ref, g_ref, o_ref):
    o_ref[...] = _rms(x_ref[...], g_ref[...]).astype(o_ref.dtype)


def _rmsnorm_cast(x, g, *, tm=512):
    s, d = x.shape
    assert s % tm == 0
    return pl.pallas_call(
        _rmsnorm_cast_kernel,
        grid=(s // tm,),
        in_specs=[pl.BlockSpec((tm, d), lambda i: (i, 0)), pl.BlockSpec((1, d), lambda i: (0, 0))],
        out_specs=pl.BlockSpec((tm, d), lambda i: (i, 0)),
        out_shape=jax.ShapeDtypeStruct((s, d), BF16),
        compiler_params=_params(("parallel",), [((tm, d), F32), ((tm, d), BF16)], temps=[((tm, d), F32)]),
        name="rmsnorm_cast",
    )(x, g)


def _slab_spec(w, steps, nj):
    rows = w.shape[0]
    n_slabs = max(c for c in range(1, steps + 1) if rows % c == 0 and (rows // c) % 16 == 0)
    slab = rows // n_slabs
    return (slab, w.shape[1]), pl.BlockSpec((slab, w.shape[1]), lambda i, j: (jnp.minimum(i * nj + j, n_slabs - 1), 0))


def _matmul_kernel(a_ref, b_ref, *rest, b_is_nk):
    n_side = (len(rest) - 1) // 2
    o_ref = rest[n_side]
    contract_b = 1 if b_is_nk else 0
    o_ref[...] = lax.dot_general(a_ref[...], _load_bf16(b_ref), (((1,), (contract_b,)), ((), ())),
                                 preferred_element_type=F32).astype(o_ref.dtype)
    for src, dst in zip(rest[:n_side], rest[n_side + 1:]):
        dst[...] = src[...].astype(dst.dtype)


def _matmul(a, b, *, tm, tn, out_dtype, name, b_is_nk=False, n=None, side_cast=()):
    m, k = a.shape
    if n is None:
        n = b.shape[0] if b_is_nk else b.shape[1]
    if b_is_nk:
        b_block, b_spec = (tn, k), pl.BlockSpec((tn, k), lambda i, j: (j, 0))
    else:
        b_block, b_spec = (k, tn), pl.BlockSpec((k, tn), lambda i, j: (0, j))
    grid = (m // tm, n // tn)
    in_specs = [pl.BlockSpec((tm, k), lambda i, j: (i, 0)), b_spec]
    out_specs = pl.BlockSpec((tm, tn), lambda i, j: (i, j))
    out_shape = jax.ShapeDtypeStruct((m, n), out_dtype)
    blocks = [((tm, k), a.dtype), (b_block, b.dtype), ((tm, tn), out_dtype)]
    args = (a, b) + tuple(side_cast)
    if side_cast:
        slabs = [_slab_spec(w, grid[0] * grid[1], grid[1]) for w in side_cast]
        in_specs += [spec for _, spec in slabs]
        out_specs = [out_specs] + [spec for _, spec in slabs]
        out_shape = [out_shape] + [jax.ShapeDtypeStruct(w.shape, BF16) for w in side_cast]
        blocks += [(block, dt) for block, _ in slabs for dt in (F32, BF16)]
    return pl.pallas_call(
        functools.partial(_matmul_kernel, b_is_nk=b_is_nk),
        grid=grid, in_specs=in_specs, out_specs=out_specs, out_shape=out_shape,
        compiler_params=_params(("parallel", "parallel"), blocks, temps=[((tm, tn), F32), ((k, tn), BF16)]),
        name=name,
    )(*args)


def _pool_mixer_kernel(halo_ref, u_ref, w_ref, scale_ref, o_ref, ext_ref, lvl_a_ref, lvl_b_ref, *, tm, gw):
    i = pl.program_id(0)
    groups = len(POOL_WINDOWS)
    base = POOL_PAD + POOL_HALO
    n = POOL_HALO + tm
    width = ext_ref.shape[1]
    for ref in (ext_ref, lvl_a_ref, lvl_b_ref):
        ref[0:POOL_PAD, :] = jnp.zeros((POOL_PAD, width), F32)
    ext_ref[POOL_PAD:base, :] = jnp.where(i > 0, halo_ref[...], 0.0)
    ext_ref[base:base + tm, :] = u_ref[...]
    row = lax.broadcasted_iota(jnp.int32, (tm, 1), 0) + i * tm
    src = ext_ref
    for k, w in enumerate(POOL_WINDOWS):
        span = w // 2
        cols = slice(k * gw, (k + 1) * gw)
        u = u_ref[:, cols]
        win_sum = src[base:base + tm, cols] + src[base - span:base - span + tm, cols]
        inv_cnt = 1.0 / jnp.minimum(row + 1, w).astype(F32)
        pooled = win_sum * inv_cnt - u
        mixed = jnp.dot(pooled.astype(BF16), w_ref[k], preferred_element_type=F32)
        o_ref[:, cols] = (mixed * scale_ref[:, cols]).astype(o_ref.dtype)
        if k + 1 < groups:
            rest = slice((k + 1) * gw, width)
            dst = lvl_a_ref if k % 2 == 0 else lvl_b_ref
            dst[POOL_PAD:POOL_PAD + n, rest] = (src[POOL_PAD:POOL_PAD + n, rest]
                                                + src[POOL_PAD - span:POOL_PAD - span + n, rest])
            src = dst


def _pool_mixer(z, w_pool, pool_scale, *, pool_width, tm=512):
    s = z.shape[0]
    groups, gw, _ = w_pool.shape
    assert all(w == 2 ** (k + 1) for k, w in enumerate(POOL_WINDOWS)) and POOL_WINDOWS[-1] // 2 <= POOL_PAD
    halo_blocks = tm // POOL_HALO
    buf = (POOL_PAD + POOL_HALO + tm, pool_width)
    return pl.pallas_call(
        functools.partial(_pool_mixer_kernel, tm=tm, gw=gw),
        grid=(s // tm,),
        in_specs=[
            pl.BlockSpec((POOL_HALO, pool_width), lambda i: (jnp.maximum(i * halo_blocks - 1, 0), 0)),
            pl.BlockSpec((tm, pool_width), lambda i: (i, 0)),
            pl.BlockSpec((groups, gw, gw), lambda i: (0, 0, 0)),
            pl.BlockSpec((1, pool_width), lambda i: (0, 0)),
        ],
        out_specs=pl.BlockSpec((tm, pool_width), lambda i: (i, 0)),
        out_shape=jax.ShapeDtypeStruct((s, pool_width), BF16),
        scratch_shapes=[pltpu.VMEM(buf, F32)] * 3,
        compiler_params=_params(("parallel",),
                                [((tm, pool_width), F32), ((groups, gw, gw), BF16), ((tm, pool_width), BF16)],
                                scratch=[(buf, F32)] * 3,
                                temps=[((tm, pool_width), F32)]),
        name="pool_mixer",
    )(z, z, w_pool, pool_scale)


def _q_proj_kernel(ql_ref, g_ref, w_ref, pos_ref, freq_ref, o_ref, qn_ref, cos_ref, sin_ref, *, scale):
    @pl.when(pl.program_id(1) == 0)
    def _():
        qn_ref[...] = (_rms(ql_ref[...], g_ref[...]) * scale).astype(qn_ref.dtype)
        ang = freq_ref[...] * pos_ref[...]
        cos_ref[...] = jnp.cos(ang)
        sin_ref[...] = jnp.sin(ang)

    qt = lax.dot_general(w_ref[...], qn_ref[...], (((1,), (1,)), ((), ())), preferred_element_type=F32)
    half = QK_ROPE_DIM // 2
    r0, r1, r2 = QK_NOPE_DIM, QK_NOPE_DIM + half, QK_NOPE_DIM + QK_ROPE_DIM
    c, sn = cos_ref[...], sin_ref[...]
    heads, rows = o_ref.shape[0], o_ref.shape[2]
    for hh in range(heads):
        q = qt[hh * r2:(hh + 1) * r2]
        x1, x2 = q[r0:r1], q[r1:r2]
        o_ref[hh, 0, 0:r0, :] = q[0:r0].astype(o_ref.dtype)
        o_ref[hh, 0, r0:r1, :] = (x1 * c - x2 * sn).astype(o_ref.dtype)
        o_ref[hh, 0, r1:r2, :] = (x1 * sn + x2 * c).astype(o_ref.dtype)
        o_ref[hh, 0, r2:, :] = jnp.zeros((rows - r2, o_ref.shape[3]), o_ref.dtype)


def _q_proj(z, q_norm, wq_t, pos_row, freq_col, *, col_block, rank, scale, tm, heads_per_step=8):
    s = z.shape[0]
    rows = V7X_MXU_DIM
    w_rows = QK_NOPE_DIM + QK_ROPE_DIM
    n_heads = wq_t.shape[0] // w_rows
    hps = heads_per_step
    half = QK_ROPE_DIM // 2
    return pl.pallas_call(
        functools.partial(_q_proj_kernel, scale=scale),
        grid=(s // tm, n_heads // hps),
        in_specs=[
            pl.BlockSpec((tm, rank), lambda i, h: (i, col_block)),
            pl.BlockSpec((1, rank), lambda i, h: (0, 0)),
            pl.BlockSpec((hps * w_rows, rank), lambda i, h: (h, 0)),
            pl.BlockSpec((1, tm), lambda i, h: (0, i)),
            pl.BlockSpec((half, 1), lambda i, h: (0, 0)),
        ],
        out_specs=pl.BlockSpec((hps, 1, rows, tm), lambda i, h: (h, i, 0, 0)),
        out_shape=jax.ShapeDtypeStruct((n_heads, s // tm, rows, tm), BF16),
        scratch_shapes=[pltpu.VMEM((tm, rank), BF16), pltpu.VMEM((half, tm), F32), pltpu.VMEM((half, tm), F32)],
        compiler_params=_params(("parallel", "arbitrary"),
                                [((tm, rank), F32), ((hps * w_rows, rank), BF16), ((hps * rows, tm), BF16)],
                                scratch=[((tm, rank), BF16), ((2 * half, tm), F32)],
                                temps=[((tm, rank), F32), ((hps * rows, tm), F32)]),
        name="q_proj",
    )(z, q_norm, wq_t, pos_row, freq_col)


def _kv_proj_kernel(kvl_ref, h_ref, wkr_ref, g_ref, wk_ref, wvt_ref, pos_ref, freq_ref, k_ref, vt_ref, kro_ref):
    kvn = _rms(kvl_ref[...], g_ref[...]).astype(BF16)
    k_ref[...] = jnp.dot(kvn, wk_ref[...], preferred_element_type=F32).astype(k_ref.dtype)
    vt = lax.dot_general(wvt_ref[...], kvn, (((1,), (1,)), ((), ())), preferred_element_type=F32)
    vt_ref[:, 0] = vt.reshape(vt_ref.shape[0], V_HEAD_DIM, vt.shape[1]).astype(vt_ref.dtype)
    half = QK_ROPE_DIM // 2
    wkr = _load_bf16(wkr_ref)
    wkr = jnp.concatenate([wkr, jnp.zeros((V7X_LANES - wkr.shape[0], wkr.shape[1]), BF16)], axis=0)
    x = lax.dot_general(h_ref[...], wkr, (((1,), (1,)), ((), ())), preferred_element_type=F32)
    ang = pos_ref[...] * freq_ref[...]
    lane = lax.broadcasted_iota(jnp.int32, x.shape, 1)
    x2_at_lo = pltpu.roll(x, V7X_LANES - half, axis=1)
    x1_at_hi = pltpu.roll(x, half, axis=1)
    partner = jnp.where(lane < half, -x2_at_lo, jnp.where(lane < QK_ROPE_DIM, x1_at_hi, 0.0))
    kro_ref[...] = (x * jnp.cos(ang) + partner * jnp.sin(ang)).astype(kro_ref.dtype)


def _kv_proj(z, h, w_in_t, kv_norm, wk, wv_t, pos_col, freq_row, *, lat_block, rope_row_block, rank, tm):
    s, d = h.shape
    nk = wk.shape[1]
    nv = wv_t.shape[0]
    n_heads = nv // V_HEAD_DIM
    return pl.pallas_call(
        _kv_proj_kernel,
        grid=(s // tm,),
        in_specs=[
            pl.BlockSpec((tm, rank), lambda i: (i, lat_block)),
            pl.BlockSpec((tm, d), lambda i: (i, 0)),
            pl.BlockSpec((QK_ROPE_DIM, d), lambda i: (rope_row_block, 0)),
            pl.BlockSpec((1, rank), lambda i: (0, 0)),
            pl.BlockSpec((rank, nk), lambda i: (0, 0)),
            pl.BlockSpec((nv, rank), lambda i: (0, 0)),
            pl.BlockSpec((tm, 1), lambda i: (i, 0)),
            pl.BlockSpec((1, V7X_LANES), lambda i: (0, 0)),
        ],
        out_specs=[
            pl.BlockSpec((tm, nk), lambda i: (i, 0)),
            pl.BlockSpec((n_heads, 1, V_HEAD_DIM, tm), lambda i: (0, i, 0, 0)),
            pl.BlockSpec((tm, V7X_LANES), lambda i: (i, 0)),
        ],
        out_shape=[
            jax.ShapeDtypeStruct((s, nk), BF16),
            jax.ShapeDtypeStruct((n_heads, s // tm, V_HEAD_DIM, tm), BF16),
            jax.ShapeDtypeStruct((s, V7X_LANES), BF16),
        ],
        compiler_params=_params(("parallel",),
                                [((tm, rank), F32), ((tm, d), BF16), ((QK_ROPE_DIM, d), w_in_t.dtype), ((rank, nk), BF16),
                                 ((nv, rank), BF16), ((tm, nk), BF16), ((nv, tm), BF16), ((tm, V7X_LANES), BF16),
                                 ((tm, V7X_LANES), F32)],
                                temps=[((tm, nk), F32), ((nv, tm), F32), ((V7X_LANES, d), BF16)]),
        name="kv_proj",
    )(z, h, w_in_t, kv_norm, wk, wv_t, pos_col, freq_row)


def _flash_kernel(kn_ref, kr_ref, q_ref, qnext_ref, vt_ref, o_ref, s_ref, smax_ref, bias_ref, m_ref, l_ref, acc_ref,
                  *, t, unroll):
    heads = q_ref.shape[0]
    qi = pl.program_id(1)
    first = 2

    def scores(kb, slot, q=q_ref):
        ks = pl.ds(pl.multiple_of(kb * t, t), t)
        for hh in range(heads):
            kn = kn_ref[ks, hh * QK_NOPE_DIM:(hh + 1) * QK_NOPE_DIM]
            kcat = jnp.concatenate([kn, kr_ref[ks, :]], axis=1)
            s = jnp.dot(kcat, q[hh, 0], preferred_element_type=F32)
            s_ref[hh, slot] = s
            smax_ref[hh, slot] = jnp.max(s, axis=0, keepdims=True)

    def softmax_pv(kb, slot, masked):
        for hh in range(heads):
            s = s_ref[hh, slot]
            if masked:
                s = s + bias_ref[...]
                block_max = jnp.max(s, axis=0, keepdims=True)
            else:
                block_max = smax_ref[hh, slot]
            m_prev = m_ref[hh]
            m_new = jnp.maximum(m_prev, block_max)
            alpha = jnp.exp2(m_prev - m_new)
            p = jnp.exp2(s - m_new)
            l_ref[hh] = alpha * l_ref[hh] + jnp.sum(p, axis=0, keepdims=True)
            pv = jnp.dot(vt_ref[hh, kb], p.astype(BF16), preferred_element_type=F32)
            acc_ref[hh] = alpha * acc_ref[hh] + pv
            m_ref[hh] = m_new

    m_ref[...] = jnp.full(m_ref.shape, -jnp.inf, F32)
    l_ref[...] = jnp.zeros(l_ref.shape, F32)
    acc_ref[...] = jnp.zeros(acc_ref.shape, F32)

    @pl.when(qi == 0)
    def _():
        kc = lax.broadcasted_iota(jnp.int32, (t, t), 0) // CHUNK
        qc = lax.broadcasted_iota(jnp.int32, (t, t), 1) // CHUNK
        bias_ref[...] = jnp.where(kc <= qc, 0.0, -jnp.inf).astype(F32)
        scores(0, first)
        softmax_pv(0, first, masked=True)
        scores(0, first, qnext_ref)

    @pl.when(qi > 0)
    def _():
        scores(1, 1)
        softmax_pv(0, first, masked=False)
        n_groups = (qi - 1) // unroll

        def group(g, c):
            b0 = 1 + g * unroll
            for u in range(unroll):
                scores(b0 + u + 1, u % 2)
                softmax_pv(b0 + u, (u + 1) % 2, masked=False)
            return c

        lax.fori_loop(0, n_groups, group, 0)
        base = 1 + n_groups * unroll
        rest = qi - base
        for r in range(unroll):
            @pl.when(rest == r)
            def _(r=r):
                for u in range(r):
                    scores(base + u + 1, u % 2)
                    softmax_pv(base + u, (u + 1) % 2, masked=False)
                scores(0, first, qnext_ref)
                softmax_pv(qi, (1 + r) % 2, masked=True)

    for hh in range(heads):
        out = (acc_ref[hh] * (1.0 / l_ref[hh])).T
        o_ref[:, hh * V_HEAD_DIM:(hh + 1) * V_HEAD_DIM] = out.astype(o_ref.dtype)


def _flash_attention(k_nope, k_rope, q_t, v_t, *, n_heads, t, unroll=2, heads_per_step=4):
    s = k_nope.shape[0]
    nq = s // t
    hps = heads_per_step
    assert t % CHUNK == 0 and unroll % 2 == 0 and nq >= 2 and n_heads % hps == 0
    assert v_t.shape == (n_heads, nq, V_HEAD_DIM, t) and q_t.shape == (n_heads, nq, V7X_MXU_DIM, t)
    scratch = [((hps, 3, t, t), F32), ((hps, 3, 1, t), F32), ((t, t), F32), ((hps, 1, t), F32), ((hps, 1, t), F32),
               ((hps, V_HEAD_DIM, t), F32)]
    return pl.pallas_call(
        functools.partial(_flash_kernel, t=t, unroll=unroll),
        grid=(n_heads // hps, nq),
        in_specs=[
            pl.BlockSpec((s, hps * QK_NOPE_DIM), lambda h, i: (0, h)),
            pl.BlockSpec((s, V7X_LANES), lambda h, i: (0, 0)),
            pl.BlockSpec((hps, 1, V7X_MXU_DIM, t), lambda h, i: (h, i, 0, 0)),
            pl.BlockSpec((hps, 1, V7X_MXU_DIM, t), lambda h, i: (h, jnp.minimum(i + 1, nq - 1), 0, 0)),
            pl.BlockSpec((hps, nq, V_HEAD_DIM, t), lambda h, i: (h, 0, 0, 0)),
        ],
        out_specs=pl.BlockSpec((t, hps * V_HEAD_DIM), lambda h, i: (i, h)),
        out_shape=jax.ShapeDtypeStruct((s, n_heads * V_HEAD_DIM), BF16),
        scratch_shapes=[pltpu.VMEM(shape, dtype) for shape, dtype in scratch],
        compiler_params=_params(("arbitrary", "arbitrary"),
                                [((s, hps * QK_NOPE_DIM), BF16), ((s, V7X_LANES), BF16),
                                 ((2 * hps * V7X_MXU_DIM, t), BF16), ((hps * V_HEAD_DIM, s), BF16),
                                 ((t, hps * V_HEAD_DIM), BF16)],
                                scratch=scratch,
                                temps=[((t, t), F32), ((t, t), BF16), ((t, V7X_MXU_DIM), BF16)]),
        name="flash_attention",
    )(k_nope, k_rope, q_t, q_t, v_t)


def _gated_merge_kernel(pm_ref, at_ref, h_ref, wup_ref, wum_ref, wga_ref, wgb_ref, wo_ref, o_ref, wo_o_ref):
    h = h_ref[...]
    ya = jnp.dot(pm_ref[...], wup_ref[...], preferred_element_type=F32)
    yb = jnp.dot(at_ref[...], wum_ref[...], preferred_element_type=F32)
    ga = _sigmoid(jnp.dot(h, wga_ref[...], preferred_element_type=F32))
    gb = _sigmoid(jnp.dot(h, wgb_ref[...], preferred_element_type=F32))
    o_ref[...] = (ga * ya + gb * yb).astype(o_ref.dtype)
    wo_o_ref[...] = wo_ref[...].astype(wo_o_ref.dtype)


def _gated_merge(pm, attn, h, w_up_pool, w_up_mla, w_gate2d, w_out, *, tm=1024, tn=256):
    s, d = h.shape
    kp, km = pm.shape[1], attn.shape[1]
    nb = d // tn
    slab_block, slab_spec = _slab_spec(w_out, (s // tm) * nb, nb)
    return pl.pallas_call(
        _gated_merge_kernel,
        grid=(s // tm, nb),
        in_specs=[
            pl.BlockSpec((tm, kp), lambda i, j: (i, 0)),
            pl.BlockSpec((tm, km), lambda i, j: (i, 0)),
            pl.BlockSpec((tm, d), lambda i, j: (i, 0)),
            pl.BlockSpec((kp, tn), lambda i, j: (0, j)),
            pl.BlockSpec((km, tn), lambda i, j: (0, j)),
            pl.BlockSpec((d, tn), lambda i, j: (0, j)),
            pl.BlockSpec((d, tn), lambda i, j: (0, j + nb)),
            slab_spec,
        ],
        out_specs=[pl.BlockSpec((tm, tn), lambda i, j: (i, j)), slab_spec],
        out_shape=[jax.ShapeDtypeStruct((s, d), BF16), jax.ShapeDtypeStruct(w_out.shape, BF16)],
        compiler_params=_params(("parallel", "parallel"),
                                [((tm, kp), BF16), ((tm, km), BF16), ((tm, d), BF16), ((kp, tn), BF16),
                                 ((km, tn), BF16), ((d, tn), BF16), ((d, tn), BF16), ((tm, tn), BF16),
                                 (slab_block, F32), (slab_block, BF16)],
                                temps=[((tm, tn), F32)] * 4),
        name="gated_merge",
    )(pm, attn, h, w_up_pool, w_up_mla, w_gate2d, w_gate2d, w_out)


def _residual_norm_kernel(t_ref, x_ref, gpost_ref, gnext_ref, xo_ref, ho_ref):
    xo = x_ref[...] + _rms(t_ref[...].astype(F32), gpost_ref[...])
    xo_ref[...] = xo
    ho_ref[...] = _rms(xo, gnext_ref[...]).astype(ho_ref.dtype)


def _residual_final_kernel(t_ref, x_ref, gpost_ref, xo_ref):
    xo_ref[...] = x_ref[...] + _rms(t_ref[...].astype(F32), gpost_ref[...])


def _residual_norm(t, x, g_post, g_next=None, *, tm=256):
    s, d = x.shape
    row = pl.BlockSpec((tm, d), lambda i: (i, 0))
    row_in = pl.BlockSpec((tm, d), lambda i: (i, 0), pipeline_mode=pl.Buffered(3))
    vec = pl.BlockSpec((1, d), lambda i: (0, 0))
    final = g_next is None
    body = _residual_final_kernel if final else _residual_norm_kernel
    args = (t, x, g_post) if final else (t, x, g_post, g_next)
    out_specs = [row] if final else [row, row]
    out_shape = [jax.ShapeDtypeStruct((s, d), F32)] + ([] if final else [jax.ShapeDtypeStruct((s, d), BF16)])
    blocks = [((tm, d), t.dtype), ((tm, d), F32), ((tm, d), F32)] + ([] if final else [((tm, d), BF16)])
    third = [((tm, d), t.dtype), ((tm, d), F32)]

    def stream(*refs):
        pltpu.emit_pipeline(body, grid=(s // tm,), in_specs=[row_in, row_in] + [vec] * (len(args) - 2),
                            out_specs=out_specs)(*refs)

    any_spec = pl.BlockSpec(memory_space=pl.ANY)
    out = pl.pallas_call(
        stream, in_specs=[any_spec] * len(args), out_specs=[any_spec] * len(out_shape), out_shape=out_shape,
        compiler_params=_params((), blocks, scratch=third, temps=[((tm, d), F32)] * 2),
        name="residual_final" if final else "residual_norm",
    )(*args)
    return out[0] if final else out


def _swiglu_up_kernel(h_ref, wg_ref, wu_ref, wd_ref, o_ref, wd_o_ref):
    h = h_ref[...]
    gate = jnp.dot(h, _load_bf16(wg_ref), preferred_element_type=F32)
    up = jnp.dot(h, _load_bf16(wu_ref), preferred_element_type=F32)
    o_ref[...] = (gate * _sigmoid(gate) * up).astype(o_ref.dtype)
    wd_o_ref[...] = wd_ref[...].astype(wd_o_ref.dtype)


def _swiglu_up(h, w_gate, w_up, w_down, *, tm=1024, tn=256):
    s, d = h.shape
    f = w_gate.shape[1]
    nj = f // tn
    steps = (s // tm) * nj
    slab = w_down.shape[0] // steps
    assert slab * steps == w_down.shape[0] and slab % 16 == 0
    dn = w_down.shape[1]
    return pl.pallas_call(
        _swiglu_up_kernel,
        grid=(s // tm, nj),
        in_specs=[
            pl.BlockSpec((tm, d), lambda i, j: (i, 0)),
            pl.BlockSpec((d, tn), lambda i, j: (0, j)),
            pl.BlockSpec((d, tn), lambda i, j: (0, j)),
            pl.BlockSpec((slab, dn), lambda i, j: (i * nj + j, 0)),
        ],
        out_specs=[pl.BlockSpec((tm, tn), lambda i, j: (i, j)),
                   pl.BlockSpec((slab, dn), lambda i, j: (i * nj + j, 0))],
        out_shape=[jax.ShapeDtypeStruct((s, f), BF16), jax.ShapeDtypeStruct(w_down.shape, BF16)],
        compiler_params=_params(("parallel", "parallel"),
                                [((tm, d), BF16), ((d, tn), w_gate.dtype), ((d, tn), w_up.dtype), ((tm, tn), BF16),
                                 ((slab, dn), w_down.dtype), ((slab, dn), BF16)],
                                temps=[((tm, tn), F32)] * 3 + [((d, tn), BF16)] * 2),
        name="swiglu_up",
    )(h, w_gate, w_up, w_down)


def _ple_gate_kernel(h_ref, p_ref, wg_ref, wp_ref, o_ref):
    gate = _sigmoid(jnp.dot(h_ref[...], _load_bf16(wg_ref), preferred_element_type=F32))
    pe = jnp.dot(_load_bf16(p_ref), _load_bf16(wp_ref), preferred_element_type=F32)
    o_ref[...] = (pe * gate).astype(o_ref.dtype)


def _ple_gate(h, p, w_gate, w_proj, *, tm=1024, tn=512):
    s, d = h.shape
    r = p.shape[1]
    return pl.pallas_call(
        _ple_gate_kernel,
        grid=(s // tm, d // tn),
        in_specs=[
            pl.BlockSpec((tm, d), lambda i, j: (i, 0)),
            pl.BlockSpec((tm, r), lambda i, j: (i, 0)),
            pl.BlockSpec((d, tn), lambda i, j: (0, j)),
            pl.BlockSpec((r, tn), lambda i, j: (0, j)),
        ],
        out_specs=pl.BlockSpec((tm, tn), lambda i, j: (i, j)),
        out_shape=jax.ShapeDtypeStruct((s, d), BF16),
        compiler_params=_params(("parallel", "parallel"),
                                [((tm, d), BF16), ((tm, r), p.dtype), ((d, tn), w_gate.dtype), ((r, tn), w_proj.dtype),
                                 ((tm, tn), BF16)],
                                temps=[((tm, tn), F32)] * 2 + [((d, tn), BF16)]),
        name="ple_gate",
    )(h, p, w_gate, w_proj)


def kernel(x, p, positions, norm_mix_pre, norm_mix_post, w_in, q_norm, kv_norm, w_q_b, w_kv_b, w_pool, pool_scale,
           w_up_pool, w_up_mla, w_branch_gate, w_out, norm_ffn_pre, norm_ffn_post, w_ffn_gate, w_ffn_up,
           w_ffn_down, norm_ple_pre, w_ple_gate, w_ple_proj, norm_ple_post):
    batch, seq, d_model = x.shape
    depth = w_in.shape[0]
    in_width = w_in.shape[2]
    q_rank, n_heads, qk_dim = w_q_b.shape[1:]
    kv_rank = w_kv_b.shape[1]
    pool_width = w_up_pool.shape[1]
    assert batch == 1 and qk_dim == QK_NOPE_DIM + QK_ROPE_DIM
    assert w_kv_b.shape[3] == QK_NOPE_DIM + V_HEAD_DIM
    o_q, o_kv, o_kr = pool_width, pool_width + q_rank, pool_width + q_rank + kv_rank
    assert o_kr + QK_ROPE_DIM == in_width
    assert o_q % q_rank == 0 and o_kv % kv_rank == 0 and o_kr % QK_ROPE_DIM == 0
    z_tn = 512
    assert o_kr % z_tn == 0
    attn_t = 512
    half = QK_ROPE_DIM // 2
    scale = float(qk_dim) ** -0.5 * LOG2_E

    inv_freq = ROPE_THETA ** (-jnp.arange(0, QK_ROPE_DIM, 2, dtype=F32) / QK_ROPE_DIM)
    freq_col = inv_freq.reshape(half, 1)
    freq_row = jnp.zeros((1, V7X_LANES), F32).at[0, :half].set(inv_freq).at[0, half:QK_ROPE_DIM].set(inv_freq)
    pos_f = positions.astype(F32)
    pos_row, pos_col = pos_f.reshape(1, seq), pos_f.reshape(seq, 1)

    xs = x.reshape(seq, d_model)
    for i in range(depth):
        w_in_t = jnp.transpose(w_in[i])
        wq_f32_t = jnp.transpose(w_q_b[i], (1, 2, 0)).reshape(-1, q_rank)
        wkv = w_kv_b[i].astype(BF16)
        wk = wkv[:, :, :QK_NOPE_DIM].reshape(kv_rank, n_heads * QK_NOPE_DIM)
        wv_t = jnp.transpose(wkv[:, :, QK_NOPE_DIM:], (1, 2, 0)).reshape(n_heads * V_HEAD_DIM, kv_rank)

        def row(v):
            return v[i].reshape(1, -1)

        h = _rmsnorm_cast(xs, row(norm_mix_pre))
        z, w_up_p, w_up_m, wq_t = _matmul(h, w_in_t, tm=1024, tn=z_tn, out_dtype=F32, name="in_proj", b_is_nk=True,
                                          n=o_kr, side_cast=(w_up_pool[i], w_up_mla[i], wq_f32_t))
        pm = _pool_mixer(z, w_pool[i].astype(BF16), row(pool_scale), pool_width=pool_width)
        q_t = _q_proj(z, row(q_norm), wq_t, pos_row, freq_col, col_block=o_q // q_rank, rank=q_rank, scale=scale,
                      tm=attn_t)
        k_nope, v_t, k_rope = _kv_proj(z, h, w_in_t, row(kv_norm), wk, wv_t, pos_col, freq_row,
                                       lat_block=o_kv // kv_rank, rope_row_block=o_kr // QK_ROPE_DIM, rank=kv_rank,
                                       tm=attn_t)
        attn = _flash_attention(k_nope, k_rope, q_t, v_t, n_heads=n_heads, t=attn_t)
        w_gate2d = w_branch_gate[i].astype(BF16).reshape(d_model, 2 * d_model)
        merged, w_out_b = _gated_merge(pm, attn, h, w_up_p, w_up_m, w_gate2d, w_out[i])
        mix = _matmul(merged, w_out_b, tm=1024, tn=1024, out_dtype=BF16, name="out_proj")
        xs, h2 = _residual_norm(mix, xs, row(norm_mix_post), row(norm_ffn_pre))
        act, w_down = _swiglu_up(h2, w_ffn_gate[i], w_ffn_up[i], w_ffn_down[i])
        ffn, w_pg = _matmul(act, w_down, tm=512, tn=512, out_dtype=BF16, name="ffn_down", side_cast=(w_ple_gate[i],))
        xs, h3 = _residual_norm(ffn, xs, row(norm_ffn_post), row(norm_ple_pre))
        t = _ple_gate(h3, p[i].reshape(seq, -1), w_pg, w_ple_proj[i], tn=1024)
        xs = _residual_norm(t, xs, row(norm_ple_post))
    return xs.reshape(batch, seq, d_model)
```
